```python
import math
import jax, jax.numpy as jnp
from jax import lax
import numpy as np

D_MODEL = 1024
BATCH = 8
SEQ = 4096
DEPTH = 4

EPS = 1e-6
D_MIX = D_MODEL

SSD_HEADS = 8
SSD_HEAD_DIM = 64
SSD_INNER = SSD_HEADS * SSD_HEAD_DIM
SSD_GROUPS = 2
SSD_HPG = SSD_HEADS // SSD_GROUPS
SSD_STATE = 128
SSD_CONV = 4
SSD_CHUNK = 128
SSD_XBC = SSD_INNER + 2 * SSD_GROUPS * SSD_STATE
DT_MIN = 0.001
DT_MAX = 0.1

MLA_HEADS = 4
MLA_NOPE = 64
MLA_ROPE = 32
MLA_V = 64
MLA_Q_RANK = 256
MLA_KV_RANK = 128
ROPE_THETA = 10000.0
MAX_POS_OFFSET = 1024

SWA_HEADS = 4
SWA_KV_HEADS = 2
SWA_HEAD_DIM = 64
WINDOW = 128

ATTN_BLOCK = 128

SSD_IN = SSD_INNER + SSD_XBC + SSD_HEADS
MLA_IN = MLA_Q_RANK + MLA_KV_RANK + MLA_ROPE
SWA_IN = (SWA_HEADS + 2 * SWA_KV_HEADS) * SWA_HEAD_DIM
D_IN = SSD_IN + MLA_IN + SWA_IN

D_FF = 2816
FFN_CONV = 3

kernel_name = "hybrid_ssd_mla_swa_convffn_adaln"


def rmsnorm(x, g):
    xf = x.astype(jnp.float32)
    y = xf * lax.rsqrt(jnp.mean(xf * xf, axis=-1, keepdims=True) + EPS)
    return (y * g.astype(jnp.float32)).astype(x.dtype)


def causal_dwconv(x, w, b):
    k_width = w.shape[0]
    s = x.shape[1]
    xp = jnp.pad(x, ((0, 0), (k_width - 1, 0), (0, 0)))
    y = b
    for k in range(k_width):
        y = y + xp[:, k:k + s] * w[k]
    return y


def apply_rope(x, cos, sin):
    xf = x.astype(jnp.float32)
    x1, x2 = jnp.split(xf, 2, axis=-1)
    return jnp.concatenate([x1 * cos - x2 * sin, x1 * sin + x2 * cos], axis=-1).astype(x.dtype)


def ssd_mixer(p, conv_w, conv_b, dt_bias, a_log, d_skip, norm_g):
    f32 = jnp.float32
    bsz, s, _ = p.shape
    nc, q = s // SSD_CHUNK, SSD_CHUNK
    z, xbc, dt = jnp.split(p, [SSD_INNER, SSD_INNER + SSD_XBC], axis=-1)
    xbc = jax.nn.silu(causal_dwconv(xbc, conv_w, conv_b))
    xs, bm, cm = jnp.split(xbc, [SSD_INNER, SSD_INNER + SSD_GROUPS * SSD_STATE], axis=-1)
    xs = xs.astype(f32).reshape(bsz, nc, q, SSD_GROUPS, SSD_HPG, SSD_HEAD_DIM)
    bm = bm.astype(f32).reshape(bsz, nc, q, SSD_GROUPS, SSD_STATE)
    cm = cm.astype(f32).reshape(bsz, nc, q, SSD_GROUPS, SSD_STATE)
    dt = jax.nn.softplus(dt.astype(f32) + dt_bias.astype(f32))
    a = -jnp.exp(a_log.astype(f32))
    dt_c = dt.reshape(bsz, nc, q, SSD_GROUPS, SSD_HPG)
    xdt = xs * dt_c[..., None]
    da = (dt * a).reshape(bsz, nc, q, SSD_HEADS).transpose(0, 1, 3, 2)
    a_h = jnp.cumsum(da, axis=-1).reshape(bsz, nc, SSD_GROUPS, SSD_HPG, q)
    causal = jnp.tril(jnp.ones((q, q), dtype=bool))
    seg = a_h[..., :, None] - a_h[..., None, :]
    decay = jnp.exp(jnp.where(causal, seg, -jnp.inf))
    cb = jnp.einsum('bclgn,bcsgn->bcgls', cm, bm)
    y_diag = jnp.einsum('bcghls,bcsghp->bclghp', cb[:, :, :, None] * decay, xdt)
    decay_to_end = jnp.exp(a_h[..., -1:] - a_h)
    states = jnp.einsum('bcsgn,bcghs,bcsghp->bcghpn', bm, decay_to_end, xdt)
    chunk_decay = jnp.exp(a_h[..., -1])

    def step(h, inp):
        st, dec = inp
        return h * dec[..., None, None] + st, h

    h0 = jnp.zeros((bsz, SSD_GROUPS, SSD_HPG, SSD_HEAD_DIM, SSD_STATE), f32)
    _, h_in = lax.scan(step, h0, (states.swapaxes(0, 1), chunk_decay.swapaxes(0, 1)))
    h_in = h_in.swapaxes(0, 1)
    y_off = jnp.einsum('bclgn,bcghpn,bcghl->bclghp', cm, h_in, jnp.exp(a_h))
    y = y_diag + y_off + xs * d_skip.astype(f32).reshape(SSD_GROUPS, SSD_HPG, 1)
    y = y.reshape(bsz, s, SSD_INNER) * jax.nn.silu(z.astype(f32))
    yg = y.reshape(bsz, s, SSD_GROUPS, SSD_INNER // SSD_GROUPS)
    yg = yg * lax.rsqrt(jnp.mean(yg * yg, axis=-1, keepdims=True) + EPS)
    y = yg.reshape(bsz, s, SSD_INNER) * norm_g.astype(f32)
    return y.astype(p.dtype)


def causal_block_attention(q, k, v, scale):
    bsz, s, h, dk = q.shape
    nb = s // ATTN_BLOCK
    qb = q.reshape(bsz, nb, ATTN_BLOCK, h, dk).swapaxes(0, 1)
    kpos = jnp.arange(s)

    def one_block(args):
        qi, i = args
        sc = jnp.einsum('bqhd,bkhd->bhqk', qi, k, preferred_element_type=jnp.float32) * scale
        qpos = i * ATTN_BLOCK + jnp.arange(ATTN_BLOCK)
        sc = jnp.where(kpos[None, :] <= qpos[:, None], sc, -jnp.inf)
        pr = jax.nn.softmax(sc, axis=-1).astype(v.dtype)
        return jnp.einsum('bhqk,bkhd->bqhd', pr, v)

    out = lax.map(one_block, (qb, jnp.arange(nb)))
    return out.swapaxes(0, 1).reshape(bsz, s, h * v.shape[-1])


def mla_mixer(p, q_norm_g, w_uq, kv_norm_g, w_ukv, cos, sin):
    bsz, s, _ = p.shape
    cq, ckv, k_rope = jnp.split(p, [MLA_Q_RANK, MLA_Q_RANK + MLA_KV_RANK], axis=-1)
    qh = (rmsnorm(cq, q_norm_g) @ w_uq).reshape(bsz, s, MLA_HEADS, MLA_NOPE + MLA_ROPE)
    q_nope, q_rope = jnp.split(qh, [MLA_NOPE], axis=-1)
    q_rope = apply_rope(q_rope, cos, sin)
    kv = (rmsnorm(ckv, kv_norm_g) @ w_ukv).reshape(bsz, s, MLA_HEADS, MLA_NOPE + MLA_V)
    k_nope, v = jnp.split(kv, [MLA_NOPE], axis=-1)
    k_rope = apply_rope(k_rope[:, :, None, :], cos, sin)
    qf = jnp.concatenate([q_nope, q_rope], axis=-1)
    kf = jnp.concatenate([k_nope, jnp.broadcast_to(k_rope, (bsz, s, MLA_HEADS, MLA_ROPE))], axis=-1)
    return causal_block_attention(qf, kf, v, 1.0 / math.sqrt(MLA_NOPE + MLA_ROPE))


def swa_sink_mixer(p, sinks):
    bsz, s, _ = p.shape
    grp = SWA_HEADS // SWA_KV_HEADS
    nb = s // WINDOW
    qd = SWA_HEADS * SWA_HEAD_DIM
    kd = SWA_KV_HEADS * SWA_HEAD_DIM
    q, k, v = jnp.split(p, [qd, qd + kd], axis=-1)
    qb = q.reshape(bsz, nb, WINDOW, SWA_KV_HEADS, grp, SWA_HEAD_DIM)

    def band(t):
        tp = jnp.pad(t.reshape(bsz, s, SWA_KV_HEADS, SWA_HEAD_DIM), ((0, 0), (WINDOW, 0), (0, 0), (0, 0)))
        tp = tp.reshape(bsz, nb + 1, WINDOW, SWA_KV_HEADS, SWA_HEAD_DIM)
        return jnp.concatenate([tp[:, :-1], tp[:, 1:]], axis=2)

    kb, vb = band(k), band(v)
    sc = jnp.einsum('bnqhgd,bnkhd->bnhgqk', qb, kb, preferred_element_type=jnp.float32)
    sc = sc * (1.0 / math.sqrt(SWA_HEAD_DIM))
    i = jnp.arange(WINDOW)[:, None]
    j = jnp.arange(2 * WINDOW)[None, :]
    rel = WINDOW + i - j
    blk = jnp.arange(nb)[:, None, None]
    valid = (rel >= 0) & (rel < WINDOW) & ((blk - 1) * WINDOW + j >= 0)
    sc = jnp.where(valid[None, :, None, None], sc, -jnp.inf)
    sink = jnp.broadcast_to(sinks.astype(jnp.float32).reshape(1, 1, SWA_KV_HEADS, grp, 1, 1),
                            sc.shape[:-1] + (1,))
    pr = jax.nn.softmax(jnp.concatenate([sc, sink], axis=-1), axis=-1)[..., :-1]
    o = jnp.einsum('bnhgqk,bnkhd->bnqhgd', pr.astype(vb.dtype), vb)
    return o.reshape(bsz, s, qd)


def _fwd_setup_inputs(seed: int = 0) -> dict:
    key = jax.random.key(seed)
    ks = iter(jax.random.split(key, 32))
    f32 = jnp.float32
    L = DEPTH

    def nrm(shape, scale):
        return jax.random.normal(next(ks), shape, f32) * scale

    def gain(shape):
        return 1.0 + nrm(shape, 0.02)

    x = nrm((BATCH, SEQ, D_MODEL), 1.0)
    c = nrm((BATCH, D_MODEL), 1.0)
    offs = jax.random.randint(next(ks), (BATCH, 1), 0, MAX_POS_OFFSET, dtype=jnp.int32)
    positions = offs + jnp.arange(SEQ, dtype=jnp.int32)[None, :]
    ada_w = nrm((L, D_MODEL, 6 * D_MODEL), 0.5 * D_MODEL ** -0.5)
    ada_b = nrm((L, 6 * D_MODEL), 0.02)
    norm1_g = gain((L, D_MODEL))
    norm2_g = gain((L, D_MODEL))
    w_in = nrm((L, D_MODEL, D_IN), D_MODEL ** -0.5)
    ssd_conv_w = nrm((L, SSD_CONV, SSD_XBC), SSD_CONV ** -0.5)
    ssd_conv_b = nrm((L, SSD_XBC), 0.02)
    dt0 = jnp.exp(jax.random.uniform(next(ks), (L, SSD_HEADS), f32)
                  * (math.log(DT_MAX) - math.log(DT_MIN)) + math.log(DT_MIN))
    ssd_dt_bias = dt0 + jnp.log(-jnp.expm1(-dt0))
    ssd_a_log = jnp.log(jax.random.uniform(next(ks), (L, SSD_HEADS), f32, 1.0, 16.0))
    ssd_d = 1.0 + nrm((L, SSD_HEADS), 0.1)
    ssd_norm_g = gain((L, SSD_INNER))
    mla_q_norm_g = gain((L, MLA_Q_RANK))
    mla_w_uq = nrm((L, MLA_Q_RANK, MLA_HEADS * (MLA_NOPE + MLA_ROPE)), MLA_Q_RANK ** -0.5)
    mla_kv_norm_g = gain((L, MLA_KV_RANK))
    mla_w_ukv = nrm((L, MLA_KV_RANK, MLA_HEADS * (MLA_NOPE + MLA_V)), MLA_KV_RANK ** -0.5)
    swa_sinks = nrm((L, SWA_HEADS), 1.0)
    w_out = nrm((L, D_MIX, D_MODEL), D_MIX ** -0.5)
    ffn_w_up = nrm((L, D_MODEL, 2 * D_FF), D_MODEL ** -0.5)
    ffn_conv_w = nrm((L, FFN_CONV, 2 * D_FF), FFN_CONV ** -0.5)
    ffn_conv_b = nrm((L, 2 * D_FF), 0.02)
    ffn_w_down = nrm((L, D_FF, D_MODEL), D_FF ** -0.5)
    final_norm_g = gain((D_MODEL,))
    return {"x": x, "c": c, "positions": positions,
            "ada_w": ada_w, "ada_b": ada_b, "norm1_g": norm1_g, "norm2_g": norm2_g,
            "w_in": w_in, "ssd_conv_w": ssd_conv_w, "ssd_conv_b": ssd_conv_b,
            "ssd_dt_bias": ssd_dt_bias, "ssd_a_log": ssd_a_log, "ssd_d": ssd_d,
            "ssd_norm_g": ssd_norm_g, "mla_q_norm_g": mla_q_norm_g, "mla_w_uq": mla_w_uq,
            "mla_kv_norm_g": mla_kv_norm_g, "mla_w_ukv": mla_w_ukv, "swa_sinks": swa_sinks,
            "w_out": w_out, "ffn_w_up": ffn_w_up, "ffn_conv_w": ffn_conv_w,
            "ffn_conv_b": ffn_conv_b, "ffn_w_down": ffn_w_down, "final_norm_g": final_norm_g}


def _fwd_reference(x, c, positions, ada_w, ada_b, norm1_g, norm2_g, w_in, ssd_conv_w, ssd_conv_b,
              ssd_dt_bias, ssd_a_log, ssd_d, ssd_norm_g, mla_q_norm_g, mla_w_uq, mla_kv_norm_g,
              mla_w_ukv, swa_sinks, w_out, ffn_w_up, ffn_conv_w, ffn_conv_b, ffn_w_down,
              final_norm_g):
    inv_freq = ROPE_THETA ** (-jnp.arange(0, MLA_ROPE, 2, dtype=jnp.float32) / MLA_ROPE)
    ang = positions.astype(jnp.float32)[..., None] * inv_freq
    cos = jnp.cos(ang)[:, :, None, :]
    sin = jnp.sin(ang)[:, :, None, :]
    c_act = jax.nn.silu(c)
    for l in range(DEPTH):
        mod = c_act @ ada_w[l] + ada_b[l]
        sh1, sc1, g1, sh2, sc2, g2 = [m[:, None, :] for m in jnp.split(mod, 6, axis=-1)]
        h = rmsnorm(x, norm1_g[l]) * (1.0 + sc1) + sh1
        proj = h @ w_in[l]
        p_ssd, p_mla, p_swa = jnp.split(proj, [SSD_IN, SSD_IN + MLA_IN], axis=-1)
        y_ssd = ssd_mixer(p_ssd, ssd_conv_w[l], ssd_conv_b[l], ssd_dt_bias[l], ssd_a_log[l],
                          ssd_d[l], ssd_norm_g[l])
        y_mla = mla_mixer(p_mla, mla_q_norm_g[l], mla_w_uq[l], mla_kv_norm_g[l], mla_w_ukv[l],
                          cos, sin)
        y_swa = swa_sink_mixer(p_swa, swa_sinks[l])
        y = jnp.concatenate([y_ssd, y_mla, y_swa], axis=-1) @ w_out[l]
        x = x + g1 * y
        h = rmsnorm(x, norm2_g[l]) * (1.0 + sc2) + sh2
        u = causal_dwconv(h @ ffn_w_up[l], ffn_conv_w[l], ffn_conv_b[l])
        a, b = jnp.split(u, 2, axis=-1)
        x = x + g2 * ((jax.nn.silu(a) * b) @ ffn_w_down[l])
    return rmsnorm(x, final_norm_g)


import jax as _jax
import jax.numpy as _jnp

TWIN_FORMAT = 'train_step'
FWD_PARAMS = ['x', 'c', 'positions', 'ada_w', 'ada_b', 'norm1_g', 'norm2_g', 'w_in', 'ssd_conv_w', 'ssd_conv_b', 'ssd_dt_bias', 'ssd_a_log', 'ssd_d', 'ssd_norm_g', 'mla_q_norm_g', 'mla_w_uq', 'mla_kv_norm_g', 'mla_w_ukv', 'swa_sinks', 'w_out', 'ffn_w_up', 'ffn_conv_w', 'ffn_conv_b', 'ffn_w_down', 'final_norm_g']
TWIN_WEIGHTS = ['ada_w', 'ada_b', 'norm1_g', 'norm2_g', 'w_in', 'ssd_conv_w', 'ssd_conv_b', 'ssd_dt_bias', 'ssd_a_log', 'ssd_d', 'ssd_norm_g', 'mla_q_norm_g', 'mla_w_uq', 'mla_kv_norm_g', 'mla_w_ukv', 'swa_sinks', 'w_out', 'ffn_w_up', 'ffn_conv_w', 'ffn_conv_b', 'ffn_w_down', 'final_norm_g']
TWIN_DIFF_INPUT = 'x'
TWIN_INPUTS = ['x', 'c', 'positions', 'ada_w', 'ada_b', 'norm1_g', 'norm2_g', 'w_in', 'ssd_conv_w', 'ssd_conv_b', 'ssd_dt_bias', 'ssd_a_log', 'ssd_d', 'ssd_norm_g', 'mla_q_norm_g', 'mla_w_uq', 'mla_kv_norm_g', 'mla_w_ukv', 'swa_sinks', 'w_out', 'ffn_w_up', 'ffn_conv_w', 'ffn_conv_b', 'ffn_w_down', 'final_norm_g', 'loss_target', 'm_ada_w', 'm_ada_b', 'm_norm1_g', 'm_norm2_g', 'm_w_in', 'm_ssd_conv_w', 'm_ssd_conv_b', 'm_ssd_dt_bias', 'm_ssd_a_log', 'm_ssd_d', 'm_ssd_norm_g', 'm_mla_q_norm_g', 'm_mla_w_uq', 'm_mla_kv_norm_g', 'm_mla_w_ukv', 'm_swa_sinks', 'm_w_out', 'm_ffn_w_up', 'm_ffn_conv_w', 'm_ffn_conv_b', 'm_ffn_w_down', 'm_final_norm_g', 'v_ada_w', 'v_ada_b', 'v_norm1_g', 'v_norm2_g', 'v_w_in', 'v_ssd_conv_w', 'v_ssd_conv_b', 'v_ssd_dt_bias', 'v_ssd_a_log', 'v_ssd_d', 'v_ssd_norm_g', 'v_mla_q_norm_g', 'v_mla_w_uq', 'v_mla_kv_norm_g', 'v_mla_w_ukv', 'v_swa_sinks', 'v_w_out', 'v_ffn_w_up', 'v_ffn_conv_w', 'v_ffn_conv_b', 'v_ffn_w_down', 'v_final_norm_g']
TWIN_OUTPUTS = ['loss', 'grad_x', 'grad_ada_w', 'grad_ada_b', 'grad_norm1_g', 'grad_norm2_g', 'grad_w_in', 'grad_ssd_conv_w', 'grad_ssd_conv_b', 'grad_ssd_dt_bias', 'grad_ssd_a_log', 'grad_ssd_d', 'grad_ssd_norm_g', 'grad_mla_q_norm_g', 'grad_mla_w_uq', 'grad_mla_kv_norm_g', 'grad_mla_w_ukv', 'grad_swa_sinks', 'grad_w_out', 'grad_ffn_w_up', 'grad_ffn_conv_w', 'grad_ffn_conv_b', 'grad_ffn_w_down', 'grad_final_norm_g', 'delta_ada_w', 'delta_ada_b', 'delta_norm1_g', 'delta_norm2_g', 'delta_w_in', 'delta_ssd_conv_w', 'delta_ssd_conv_b', 'delta_ssd_dt_bias', 'delta_ssd_a_log', 'delta_ssd_d', 'delta_ssd_norm_g', 'delta_mla_q_norm_g', 'delta_mla_w_uq', 'delta_mla_kv_norm_g', 'delta_mla_w_ukv', 'delta_swa_sinks', 'delta_w_out', 'delta_ffn_w_up', 'delta_ffn_conv_w', 'delta_ffn_conv_b', 'delta_ffn_w_down', 'delta_final_norm_g', 'new_m_ada_w', 'new_m_ada_b', 'new_m_norm1_g', 'new_m_norm2_g', 'new_m_w_in', 'new_m_ssd_conv_w', 'new_m_ssd_conv_b', 'new_m_ssd_dt_bias', 'new_m_ssd_a_log', 'new_m_ssd_d', 'new_m_ssd_norm_g', 'new_m_mla_q_norm_g', 'new_m_mla_w_uq', 'new_m_mla_kv_norm_g', 'new_m_mla_w_ukv', 'new_m_swa_sinks', 'new_m_w_out', 'new_m_ffn_w_up', 'new_m_ffn_conv_w', 'new_m_ffn_conv_b', 'new_m_ffn_w_down', 'new_m_final_norm_g', 'new_v_ada_w', 'new_v_ada_b', 'new_v_norm1_g', 'new_v_norm2_g', 'new_v_w_in', 'new_v_ssd_conv_w', 'new_v_ssd_conv_b', 'new_v_ssd_dt_bias', 'new_v_ssd_a_log', 'new_v_ssd_d', 'new_v_ssd_norm_g', 'new_v_mla_q_norm_g', 'new_v_mla_w_uq', 'new_v_mla_kv_norm_g', 'new_v_mla_w_ukv', 'new_v_swa_sinks', 'new_v_w_out', 'new_v_ffn_w_up', 'new_v_ffn_conv_w', 'new_v_ffn_conv_b', 'new_v_ffn_w_down', 'new_v_final_norm_g']
TWIN_LEAF_KINDS = {'loss': 'loss', 'grad_x': 'grad_x', 'grad_ada_w': 'grad_w', 'grad_ada_b': 'grad_w', 'grad_norm1_g': 'grad_w', 'grad_norm2_g': 'grad_w', 'grad_w_in': 'grad_w', 'grad_ssd_conv_w': 'grad_w', 'grad_ssd_conv_b': 'grad_w', 'grad_ssd_dt_bias': 'grad_w', 'grad_ssd_a_log': 'grad_w', 'grad_ssd_d': 'grad_w', 'grad_ssd_norm_g': 'grad_w', 'grad_mla_q_norm_g': 'grad_w', 'grad_mla_w_uq': 'grad_w', 'grad_mla_kv_norm_g': 'grad_w', 'grad_mla_w_ukv': 'grad_w', 'grad_swa_sinks': 'grad_w', 'grad_w_out': 'grad_w', 'grad_ffn_w_up': 'grad_w', 'grad_ffn_conv_w': 'grad_w', 'grad_ffn_conv_b': 'grad_w', 'grad_ffn_w_down': 'grad_w', 'grad_final_norm_g': 'grad_w', 'delta_ada_w': 'delta_w', 'delta_ada_b': 'delta_w', 'delta_norm1_g': 'delta_w', 'delta_norm2_g': 'delta_w', 'delta_w_in': 'delta_w', 'delta_ssd_conv_w': 'delta_w', 'delta_ssd_conv_b': 'delta_w', 'delta_ssd_dt_bias': 'delta_w', 'delta_ssd_a_log': 'delta_w', 'delta_ssd_d': 'delta_w', 'delta_ssd_norm_g': 'delta_w', 'delta_mla_q_norm_g': 'delta_w', 'delta_mla_w_uq': 'delta_w', 'delta_mla_kv_norm_g': 'delta_w', 'delta_mla_w_ukv': 'delta_w', 'delta_swa_sinks': 'delta_w', 'delta_w_out': 'delta_w', 'delta_ffn_w_up': 'delta_w', 'delta_ffn_conv_w': 'delta_w', 'delta_ffn_conv_b': 'delta_w', 'delta_ffn_w_down': 'delta_w', 'delta_final_norm_g': 'delta_w', 'new_m_ada_w': 'new_m', 'new_m_ada_b': 'new_m', 'new_m_norm1_g': 'new_m', 'new_m_norm2_g': 'new_m', 'new_m_w_in': 'new_m', 'new_m_ssd_conv_w': 'new_m', 'new_m_ssd_conv_b': 'new_m', 'new_m_ssd_dt_bias': 'new_m', 'new_m_ssd_a_log': 'new_m', 'new_m_ssd_d': 'new_m', 'new_m_ssd_norm_g': 'new_m', 'new_m_mla_q_norm_g': 'new_m', 'new_m_mla_w_uq': 'new_m', 'new_m_mla_kv_norm_g': 'new_m', 'new_m_mla_w_ukv': 'new_m', 'new_m_swa_sinks': 'new_m', 'new_m_w_out': 'new_m', 'new_m_ffn_w_up': 'new_m', 'new_m_ffn_conv_w': 'new_m', 'new_m_ffn_conv_b': 'new_m', 'new_m_ffn_w_down': 'new_m', 'new_m_final_norm_g': 'new_m', 'new_v_ada_w': 'new_v', 'new_v_ada_b': 'new_v', 'new_v_norm1_g': 'new_v', 'new_v_norm2_g': 'new_v', 'new_v_w_in': 'new_v', 'new_v_ssd_conv_w': 'new_v', 'new_v_ssd_conv_b': 'new_v', 'new_v_ssd_dt_bias': 'new_v', 'new_v_ssd_a_log': 'new_v', 'new_v_ssd_d': 'new_v', 'new_v_ssd_norm_g': 'new_v', 'new_v_mla_q_norm_g': 'new_v', 'new_v_mla_w_uq': 'new_v', 'new_v_mla_kv_norm_g': 'new_v', 'new_v_mla_w_ukv': 'new_v', 'new_v_swa_sinks': 'new_v', 'new_v_w_out': 'new_v', 'new_v_ffn_w_up': 'new_v', 'new_v_ffn_conv_w': 'new_v', 'new_v_ffn_conv_b': 'new_v', 'new_v_ffn_w_down': 'new_v', 'new_v_final_norm_g': 'new_v'}


def _forward(args):
    return _fwd_reference(*[args[k] for k in FWD_PARAMS])


def _output_shape():
    out = _jax.eval_shape(lambda: _forward(_fwd_setup_inputs(0)))
    return out.shape, out.dtype

N_MICROBATCH = 1
ADAM_LR = 0.001
ADAM_B1 = 0.9
ADAM_B2 = 0.999
ADAM_EPS = 1e-08
ADAM_WD = 0.01
ADAM_STEP = 10
PER_EXAMPLE_BATCH_AXIS = {'x': 0, 'c': 0, 'positions': 0, 'loss_target': 0}
SHARED_INPUTS = []
_WEIGHT_DTYPES = {'ada_w': _jnp.float32, 'ada_b': _jnp.float32, 'norm1_g': _jnp.float32, 'norm2_g': _jnp.float32, 'w_in': _jnp.float32, 'ssd_conv_w': _jnp.float32, 'ssd_conv_b': _jnp.float32, 'ssd_dt_bias': _jnp.float32, 'ssd_a_log': _jnp.float32, 'ssd_d': _jnp.float32, 'ssd_norm_g': _jnp.float32, 'mla_q_norm_g': _jnp.float32, 'mla_w_uq': _jnp.float32, 'mla_kv_norm_g': _jnp.float32, 'mla_w_ukv': _jnp.float32, 'swa_sinks': _jnp.float32, 'w_out': _jnp.float32, 'ffn_w_up': _jnp.float32, 'ffn_conv_w': _jnp.float32, 'ffn_conv_b': _jnp.float32, 'ffn_w_down': _jnp.float32, 'final_norm_g': _jnp.float32}
MOMENT_SCALE = {'ada_w': 5.628498e-02, 'ada_b': 9.183427e-02, 'norm1_g': 5.881183e-02, 'norm2_g': 5.463367e-02, 'w_in': 3.981632e-02, 'ssd_conv_w': 4.268221e-02, 'ssd_conv_b': 5.140457e-02, 'ssd_dt_bias': 7.740074e-02, 'ssd_a_log': 1.850552e-01, 'ssd_d': 2.579329e-01, 'ssd_norm_g': 5.322301e-02, 'mla_q_norm_g': 9.587922e-03, 'mla_w_uq': 7.921466e-03, 'mla_kv_norm_g': 3.162949e-02, 'mla_w_ukv': 1.567756e-02, 'swa_sinks': 1.038150e-02, 'w_out': 4.108485e-02, 'ffn_w_up': 2.418685e-02, 'ffn_conv_w': 2.397287e-02, 'ffn_conv_b': 2.169361e-02, 'ffn_w_down': 3.933589e-02, 'final_norm_g': 3.206146e+01}


def _to_microbatches(a, axis):
    t = _jnp.moveaxis(a, axis, 0)
    t = t.reshape((N_MICROBATCH, t.shape[0] // N_MICROBATCH) + t.shape[1:])
    return _jnp.moveaxis(t, 1, axis + 1)


def setup_inputs(seed: int = 0) -> dict:
    inp = _fwd_setup_inputs(seed)
    key = _jax.random.fold_in(_jax.random.key(seed), 7919)
    shape, _ = _output_shape()
    out = dict(inp)
    out["loss_target"] = _jax.random.normal(_jax.random.fold_in(key, 0), shape, _jnp.float32)
    for i, name in enumerate(TWIN_WEIGHTS):
        w = inp[name].astype(_jnp.float32)
        if MOMENT_SCALE is None:
            s = _jnp.sqrt(_jnp.mean(_jnp.square(w)) + 1e-30)
        else:
            s = MOMENT_SCALE[name]
        km, kv = _jax.random.split(_jax.random.fold_in(key, i + 1))
        out[name] = w
        out["m_" + name] = s * _jax.random.normal(km, w.shape, _jnp.float32)
        out["v_" + name] = (s * s) * _jax.random.uniform(kv, w.shape, _jnp.float32, 0.5, 1.5)
    if N_MICROBATCH > 1:
        for name, axis in PER_EXAMPLE_BATCH_AXIS.items():
            out[name] = _to_microbatches(out[name], axis)
    return {'x': out['x'], 'c': out['c'], 'positions': out['positions'], 'ada_w': out['ada_w'], 'ada_b': out['ada_b'], 'norm1_g': out['norm1_g'], 'norm2_g': out['norm2_g'], 'w_in': out['w_in'], 'ssd_conv_w': out['ssd_conv_w'], 'ssd_conv_b': out['ssd_conv_b'], 'ssd_dt_bias': out['ssd_dt_bias'], 'ssd_a_log': out['ssd_a_log'], 'ssd_d': out['ssd_d'], 'ssd_norm_g': out['ssd_norm_g'], 'mla_q_norm_g': out['mla_q_norm_g'], 'mla_w_uq': out['mla_w_uq'], 'mla_kv_norm_g': out['mla_kv_norm_g'], 'mla_w_ukv': out['mla_w_ukv'], 'swa_sinks': out['swa_sinks'], 'w_out': out['w_out'], 'ffn_w_up': out['ffn_w_up'], 'ffn_conv_w': out['ffn_conv_w'], 'ffn_conv_b': out['ffn_conv_b'], 'ffn_w_down': out['ffn_w_down'], 'final_norm_g': out['final_norm_g'], 'loss_target': out['loss_target'], 'm_ada_w': out['m_ada_w'], 'm_ada_b': out['m_ada_b'], 'm_norm1_g': out['m_norm1_g'], 'm_norm2_g': out['m_norm2_g'], 'm_w_in': out['m_w_in'], 'm_ssd_conv_w': out['m_ssd_conv_w'], 'm_ssd_conv_b': out['m_ssd_conv_b'], 'm_ssd_dt_bias': out['m_ssd_dt_bias'], 'm_ssd_a_log': out['m_ssd_a_log'], 'm_ssd_d': out['m_ssd_d'], 'm_ssd_norm_g': out['m_ssd_norm_g'], 'm_mla_q_norm_g': out['m_mla_q_norm_g'], 'm_mla_w_uq': out['m_mla_w_uq'], 'm_mla_kv_norm_g': out['m_mla_kv_norm_g'], 'm_mla_w_ukv': out['m_mla_w_ukv'], 'm_swa_sinks': out['m_swa_sinks'], 'm_w_out': out['m_w_out'], 'm_ffn_w_up': out['m_ffn_w_up'], 'm_ffn_conv_w': out['m_ffn_conv_w'], 'm_ffn_conv_b': out['m_ffn_conv_b'], 'm_ffn_w_down': out['m_ffn_w_down'], 'm_final_norm_g': out['m_final_norm_g'], 'v_ada_w': out['v_ada_w'], 'v_ada_b': out['v_ada_b'], 'v_norm1_g': out['v_norm1_g'], 'v_norm2_g': out['v_norm2_g'], 'v_w_in': out['v_w_in'], 'v_ssd_conv_w': out['v_ssd_conv_w'], 'v_ssd_conv_b': out['v_ssd_conv_b'], 'v_ssd_dt_bias': out['v_ssd_dt_bias'], 'v_ssd_a_log': out['v_ssd_a_log'], 'v_ssd_d': out['v_ssd_d'], 'v_ssd_norm_g': out['v_ssd_norm_g'], 'v_mla_q_norm_g': out['v_mla_q_norm_g'], 'v_mla_w_uq': out['v_mla_w_uq'], 'v_mla_kv_norm_g': out['v_mla_kv_norm_g'], 'v_mla_w_ukv': out['v_mla_w_ukv'], 'v_swa_sinks': out['v_swa_sinks'], 'v_w_out': out['v_w_out'], 'v_ffn_w_up': out['v_ffn_w_up'], 'v_ffn_conv_w': out['v_ffn_conv_w'], 'v_ffn_conv_b': out['v_ffn_conv_b'], 'v_ffn_w_down': out['v_ffn_w_down'], 'v_final_norm_g': out['v_final_norm_g']}


def _loss(weights, diff, rest, loss_target):
    with _jax.named_scope("forward"):
        args = {**rest, TWIN_DIFF_INPUT: diff, **{k: w.astype(_WEIGHT_DTYPES[k]) for k, w in weights.items()}}
        y = _forward(args)
    with _jax.named_scope("loss_head"):
        err = _jnp.square(y.astype(_jnp.float32) - loss_target)
        return 0.5 * _jnp.sum(_jnp.mean(err, axis=-1)) if err.ndim else 0.5 * err


def _adamw(w, g, m, v):
    m = ADAM_B1 * m + (1.0 - ADAM_B1) * g
    v = ADAM_B2 * v + (1.0 - ADAM_B2) * _jnp.square(g)
    m_hat = m / (1.0 - ADAM_B1 ** ADAM_STEP)
    v_hat = v / (1.0 - ADAM_B2 ** ADAM_STEP)
    delta = -ADAM_LR * (m_hat / (_jnp.sqrt(v_hat) + ADAM_EPS) + ADAM_WD * w)
    return delta, m, v


def reference(x, c, positions, ada_w, ada_b, norm1_g, norm2_g, w_in, ssd_conv_w, ssd_conv_b, ssd_dt_bias, ssd_a_log, ssd_d, ssd_norm_g, mla_q_norm_g, mla_w_uq, mla_kv_norm_g, mla_w_ukv, swa_sinks, w_out, ffn_w_up, ffn_conv_w, ffn_conv_b, ffn_w_down, final_norm_g, loss_target, m_ada_w, m_ada_b, m_norm1_g, m_norm2_g, m_w_in, m_ssd_conv_w, m_ssd_conv_b, m_ssd_dt_bias, m_ssd_a_log, m_ssd_d, m_ssd_norm_g, m_mla_q_norm_g, m_mla_w_uq, m_mla_kv_norm_g, m_mla_w_ukv, m_swa_sinks, m_w_out, m_ffn_w_up, m_ffn_conv_w, m_ffn_conv_b, m_ffn_w_down, m_final_norm_g, v_ada_w, v_ada_b, v_norm1_g, v_norm2_g, v_w_in, v_ssd_conv_w, v_ssd_conv_b, v_ssd_dt_bias, v_ssd_a_log, v_ssd_d, v_ssd_norm_g, v_mla_q_norm_g, v_mla_w_uq, v_mla_kv_norm_g, v_mla_w_ukv, v_swa_sinks, v_w_out, v_ffn_w_up, v_ffn_conv_w, v_ffn_conv_b, v_ffn_w_down, v_final_norm_g):
    given = dict(x=x, c=c, positions=positions, ada_w=ada_w, ada_b=ada_b, norm1_g=norm1_g, norm2_g=norm2_g, w_in=w_in, ssd_conv_w=ssd_conv_w, ssd_conv_b=ssd_conv_b, ssd_dt_bias=ssd_dt_bias, ssd_a_log=ssd_a_log, ssd_d=ssd_d, ssd_norm_g=ssd_norm_g, mla_q_norm_g=mla_q_norm_g, mla_w_uq=mla_w_uq, mla_kv_norm_g=mla_kv_norm_g, mla_w_ukv=mla_w_ukv, swa_sinks=swa_sinks, w_out=w_out, ffn_w_up=ffn_w_up, ffn_conv_w=ffn_conv_w, ffn_conv_b=ffn_conv_b, ffn_w_down=ffn_w_down, final_norm_g=final_norm_g, loss_target=loss_target, m_ada_w=m_ada_w, m_ada_b=m_ada_b, m_norm1_g=m_norm1_g, m_norm2_g=m_norm2_g, m_w_in=m_w_in, m_ssd_conv_w=m_ssd_conv_w, m_ssd_conv_b=m_ssd_conv_b, m_ssd_dt_bias=m_ssd_dt_bias, m_ssd_a_log=m_ssd_a_log, m_ssd_d=m_ssd_d, m_ssd_norm_g=m_ssd_norm_g, m_mla_q_norm_g=m_mla_q_norm_g, m_mla_w_uq=m_mla_w_uq, m_mla_kv_norm_g=m_mla_kv_norm_g, m_mla_w_ukv=m_mla_w_ukv, m_swa_sinks=m_swa_sinks, m_w_out=m_w_out, m_ffn_w_up=m_ffn_w_up, m_ffn_conv_w=m_ffn_conv_w, m_ffn_conv_b=m_ffn_conv_b, m_ffn_w_down=m_ffn_w_down, m_final_norm_g=m_final_norm_g, v_ada_w=v_ada_w, v_ada_b=v_ada_b, v_norm1_g=v_norm1_g, v_norm2_g=v_norm2_g, v_w_in=v_w_in, v_ssd_conv_w=v_ssd_conv_w, v_ssd_conv_b=v_ssd_conv_b, v_ssd_dt_bias=v_ssd_dt_bias, v_ssd_a_log=v_ssd_a_log, v_ssd_d=v_ssd_d, v_ssd_norm_g=v_ssd_norm_g, v_mla_q_norm_g=v_mla_q_norm_g, v_mla_w_uq=v_mla_w_uq, v_mla_kv_norm_g=v_mla_kv_norm_g, v_mla_w_ukv=v_mla_w_ukv, v_swa_sinks=v_swa_sinks, v_w_out=v_w_out, v_ffn_w_up=v_ffn_w_up, v_ffn_conv_w=v_ffn_conv_w, v_ffn_conv_b=v_ffn_conv_b, v_ffn_w_down=v_ffn_w_down, v_final_norm_g=v_final_norm_g)
    weights = {n: given[n] for n in TWIN_WEIGHTS}
    shared = {n: given[n] for n in SHARED_INPUTS}
    per_example = {n: given[n] for n in ['x', 'c', 'positions']}
    grad_fn = _jax.value_and_grad(_loss, argnums=(0, 1))

    def one_microbatch(ex, loss_target):
        ex = dict(ex)
        diff = ex.pop(TWIN_DIFF_INPUT)
        return grad_fn(weights, diff, {**shared, **ex}, loss_target)

    if N_MICROBATCH == 1:
        loss, (grad_w, grad_x) = one_microbatch(per_example, given["loss_target"])
    else:
        def body(carry, xs):
            loss_sum, grad_sum = carry
            l_k, (gw_k, gx_k) = one_microbatch(xs[0], xs[1])
            with _jax.named_scope("update"):
                return (loss_sum + l_k, _jax.tree.map(_jnp.add, grad_sum, gw_k)), gx_k

        init = (_jnp.zeros((), _jnp.float32), _jax.tree.map(_jnp.zeros_like, weights))
        (loss, grad_w), grad_x = _jax.lax.scan(body, init, (per_example, given["loss_target"]))
    with _jax.named_scope("update"):
        delta_w, new_m, new_v = {}, {}, {}
        for n in TWIN_WEIGHTS:
            delta_w[n], new_m[n], new_v[n] = _adamw(weights[n], grad_w[n], given["m_" + n], given["v_" + n])
    return (loss, grad_x, *[grad_w[n] for n in TWIN_WEIGHTS], *[delta_w[n] for n in TWIN_WEIGHTS],
            *[new_m[n] for n in TWIN_WEIGHTS], *[new_v[n] for n in TWIN_WEIGHTS])
```

```python
import functools
import math

import jax
import jax.numpy as jnp
from jax import lax
from jax.experimental import pallas as pl
from jax.experimental.pallas import tpu as pltpu

F32 = jnp.float32
BF16 = jnp.bfloat16
MXU_DTYPE = BF16

D = 1024
DEPTH = 4
EPS = 1e-6
N_DEV = 8
N_CHIP = 4

SSD_HEADS = 8
SSD_INNER = 512
SSD_STATE = 128
SSD_XBC = 1024
Q = 128
MLA_HEADS = 4
MLA_QK = 96
D_FF = 2816
D_IN = 2472

P_XBC, P_Z, P_CQ, P_SQ, P_CKV, P_DT, P_KR, P_SK, P_SV = 0, 1024, 1536, 1792, 2048, 2176, 2304, 2432, 2560
NP = 2688
_PACK = ((P_Z, 0, 512), (P_XBC, 512, 1024), (P_DT, 1536, 8), (P_CQ, 1544, 256), (P_CKV, 1800, 128),
         (P_KR, 1928, 32), (P_SQ, 1960, 256), (P_SK, 2216, 128), (P_SV, 2344, 128))

ADAM_LR, ADAM_B1, ADAM_B2, ADAM_EPS, ADAM_WD, ADAM_STEP = 0.001, 0.9, 0.999, 1e-08, 0.01, 10

VMEM_LIMIT = 56 * 1024 * 1024
NEG = -1e30


def _cp(sem=None):
    return pltpu.CompilerParams(dimension_semantics=sem, vmem_limit_bytes=VMEM_LIMIT)


def _dot(a, b, dims):
    return lax.dot_general(a.astype(MXU_DTYPE), b.astype(MXU_DTYPE), (dims, ((), ())), preferred_element_type=F32)


_NN = ((1,), (0,))
_NT = ((1,), (1,))
_TN = ((0,), (0,))


@jax.custom_vjp
def mm(a, b):
    return _dot(a, b, _NN)


mm.defvjp(lambda a, b: (_dot(a, b, _NN), (a, b)),
          lambda r, g: (_dot(g, r[1], _NT), _dot(r[0], g, _TN)))


@jax.custom_vjp
def mm_nt(a, b):
    return _dot(a, b, _NT)


mm_nt.defvjp(lambda a, b: (_dot(a, b, _NT), (a, b)),
             lambda r, g: (_dot(g, r[1], _NN), _dot(g, r[0], _TN)))


@jax.custom_vjp
def mm_tn(a, b):
    return _dot(a, b, _TN)


mm_tn.defvjp(lambda a, b: (_dot(a, b, _TN), (a, b)),
             lambda r, g: (_dot(r[1], g, _NT), _dot(r[0], g, _NN)))


def _silu(x):
    return x * jax.nn.sigmoid(x)


def _rms(x, g):
    return x * lax.rsqrt(jnp.mean(x * x, axis=-1, keepdims=True) + EPS) * g


def _modnorm(x, g, sh, sc):
    return _rms(x, g) * (1.0 + sc) + sh


def _blk(dim, target, mult=128):
    best = None
    for b in range(mult, min(dim, target) + 1, mult):
        if dim % b == 0:
            best = b
    return best if best is not None else dim


def matmul(a, b, mode, out_dtype, name):
    if mode == "nn":
        (m, k), n = a.shape, b.shape[1]
    elif mode == "nt":
        (m, k), n = a.shape, b.shape[0]
    else:
        (k, m), n = a.shape, b.shape[1]
    bm, bn, bk = _blk(m, 512), _blk(n, 1024), _blk(k, 512)
    nk = k // bk
    dims = {"nn": _NN, "nt": _NT, "tn": _TN}[mode]

    def body(a_ref, b_ref, o_ref, acc_ref):
        kk = pl.program_id(2)

        @pl.when(kk == 0)
        def _():
            acc_ref[...] = jnp.zeros_like(acc_ref)

        acc_ref[...] += _dot(a_ref[...], b_ref[...], dims)

        @pl.when(kk == nk - 1)
        def _():
            o_ref[...] = acc_ref[...].astype(o_ref.dtype)

    a_spec = pl.BlockSpec((bk, bm), lambda i, j, kk: (kk, i)) if mode == "tn" else pl.BlockSpec((bm, bk), lambda i, j, kk: (i, kk))
    b_spec = pl.BlockSpec((bn, bk), lambda i, j, kk: (j, kk)) if mode == "nt" else pl.BlockSpec((bk, bn), lambda i, j, kk: (kk, j))
    return pl.pallas_call(
        body, name=name, out_shape=jax.ShapeDtypeStruct((m, n), out_dtype), grid=(m // bm, n // bn, nk),
        in_specs=[a_spec, b_spec], out_specs=pl.BlockSpec((bm, bn), lambda i, j, kk: (i, j)),
        scratch_shapes=[pltpu.VMEM((bm, bn), F32)], compiler_params=_cp(("parallel", "parallel", "arbitrary")),
    )(a, b)


TM = 512


def _row(v):
    return v.reshape(1, -1)


def modnorm_fwd(x, g, sh, sc, name):
    t = x.shape[0]

    def body(x_ref, g_ref, sh_ref, sc_ref, o_ref):
        o_ref[...] = _modnorm(x_ref[...], g_ref[...], sh_ref[...], sc_ref[...]).astype(o_ref.dtype)

    vec = pl.BlockSpec((1, D), lambda i: (0, 0))
    return pl.pallas_call(
        body, name=name, out_shape=jax.ShapeDtypeStruct((t, D), BF16), grid=(t // TM,),
        in_specs=[pl.BlockSpec((TM, D), lambda i: (i, 0)), vec, vec, vec],
        out_specs=pl.BlockSpec((TM, D), lambda i: (i, 0)), compiler_params=_cp(("parallel",)),
    )(x, _row(g), _row(sh), _row(sc))


def modnorm_bwd(x, g, sh, sc, dh, dres, name):
    t = x.shape[0]

    def body(x_ref, g_ref, sh_ref, sc_ref, dh_ref, dres_ref, dx_ref, sums_ref):
        _, vjp = jax.vjp(_modnorm, x_ref[...], g_ref[...], sh_ref[...], sc_ref[...])
        dx, dg, dsh, dsc = vjp(dh_ref[...].astype(F32))
        dx_ref[...] = dx + dres_ref[...]

        @pl.when(pl.program_id(0) == 0)
        def _():
            sums_ref[...] = jnp.zeros_like(sums_ref)

        sums_ref[0:1, :] += dg
        sums_ref[1:2, :] += dsh
        sums_ref[2:3, :] += dsc

    vec = pl.BlockSpec((1, D), lambda i: (0, 0))
    tile = pl.BlockSpec((TM, D), lambda i: (i, 0))
    return pl.pallas_call(
        body, name=name, out_shape=(jax.ShapeDtypeStruct((t, D), F32), jax.ShapeDtypeStruct((8, D), F32)), grid=(t // TM,),
        in_specs=[tile, vec, vec, vec, tile, tile], out_specs=(tile, pl.BlockSpec((8, D), lambda i: (0, 0))),
        compiler_params=_cp(("arbitrary",)),
    )(x, _row(g), _row(sh), _row(sc), dh, dres)


def resid_fwd(x, y, gate, name):
    t = x.shape[0]

    def body(x_ref, y_ref, g_ref, o_ref):
        o_ref[...] = x_ref[...] + g_ref[...] * y_ref[...]

    tile = pl.BlockSpec((TM, D), lambda i: (i, 0))
    return pl.pallas_call(
        body, name=name, out_shape=jax.ShapeDtypeStruct((t, D), F32), grid=(t // TM,),
        in_specs=[tile, tile, pl.BlockSpec((1, D), lambda i: (0, 0))], out_specs=tile, compiler_params=_cp(("parallel",)),
    )(x, y, _row(gate))


def resid_bwd(dxo, y, gate, name):
    t = dxo.shape[0]

    def body(d_ref, y_ref, g_ref, dy_ref, dg_ref):
        d = d_ref[...]
        dy_ref[...] = (d * g_ref[...]).astype(BF16)

        @pl.when(pl.program_id(0) == 0)
        def _():
            dg_ref[...] = jnp.zeros_like(dg_ref)

        dg_ref[0:1, :] += jnp.sum(d * y_ref[...], axis=0, keepdims=True)

    tile = pl.BlockSpec((TM, D), lambda i: (i, 0))
    return pl.pallas_call(
        body, name=name, out_shape=(jax.ShapeDtypeStruct((t, D), BF16), jax.ShapeDtypeStruct((8, D), F32)), grid=(t // TM,),
        in_specs=[tile, tile, pl.BlockSpec((1, D), lambda i: (0, 0))], out_specs=(tile, pl.BlockSpec((8, D), lambda i: (0, 0))),
        compiler_params=_cp(("arbitrary",)),
    )(dxo, y, _row(gate))


CW = 256
NCW = D_FF // CW


def _conv3(u, halo, w, b):
    n = u.shape[0]
    win = jnp.concatenate([halo, u], axis=0)
    return b + w[0:1] * win[6:6 + n] + w[1:2] * win[7:7 + n] + w[2:3] * win[8:8 + n]


def _convglu(ua, ub, ha, hb, wa, wb, ba, bb):
    return _silu(_conv3(ua, ha, wa, ba)) * _conv3(ub, hb, wb, bb)


def _halo_specs(tm, cw, off):
    r = tm // 8
    return pl.BlockSpec((8, cw), lambda j, i, o=off: (jnp.maximum(i * r - 1, 0), j + o))


def convglu_fwd(u0, cw, cb, name):
    t = u0.shape[0]

    def body(ua_ref, ub_ref, ha_ref, hb_ref, wa_ref, wb_ref, ba_ref, bb_ref, o_ref):
        keep = (pl.program_id(1) > 0).astype(F32)
        o_ref[...] = _convglu(ua_ref[...], ub_ref[...], ha_ref[...] * keep, hb_ref[...] * keep,
                              wa_ref[...], wb_ref[...], ba_ref[...], bb_ref[...]).astype(o_ref.dtype)

    def col(rows, off):
        return pl.BlockSpec((rows, CW), lambda j, i, o=off: (0, j + o))

    return pl.pallas_call(
        body, name=name, out_shape=jax.ShapeDtypeStruct((t, D_FF), BF16), grid=(NCW, t // TM),
        in_specs=[pl.BlockSpec((TM, CW), lambda j, i: (i, j)), pl.BlockSpec((TM, CW), lambda j, i: (i, j + NCW)),
                  _halo_specs(TM, CW, 0), _halo_specs(TM, CW, NCW), col(3, 0), col(3, NCW), col(1, 0), col(1, NCW)],
        out_specs=pl.BlockSpec((TM, CW), lambda j, i: (i, j)), compiler_params=_cp(("parallel", "parallel")),
    )(u0, u0, u0, u0, cw, cw, _row(cb), _row(cb))


def convglu_bwd(u0, cw, cb, dgact, name):
    t = u0.shape[0]
    nt = t // TM

    def body(ua_ref, ub_ref, ha_ref, hb_ref, wa_ref, wb_ref, ba_ref, bb_ref, dg_ref,
             dua_ref, dub_ref, dwa_ref, dwb_ref, dba_ref, dbb_ref, ca_ref, cb_ref):
        step = pl.program_id(1)
        keep = (step < nt - 1).astype(F32)

        @pl.when(step == 0)
        def _():
            ca_ref[...] = jnp.zeros_like(ca_ref)
            cb_ref[...] = jnp.zeros_like(cb_ref)
            dwa_ref[...] = jnp.zeros_like(dwa_ref)
            dwb_ref[...] = jnp.zeros_like(dwb_ref)
            dba_ref[...] = jnp.zeros_like(dba_ref)
            dbb_ref[...] = jnp.zeros_like(dbb_ref)

        _, vjp = jax.vjp(_convglu, ua_ref[...], ub_ref[...], ha_ref[...] * keep, hb_ref[...] * keep,
                         wa_ref[...], wb_ref[...], ba_ref[...], bb_ref[...])
        dua, dub, dha, dhb, dwa, dwb, dba, dbb = vjp(dg_ref[...].astype(F32))
        zeros = jnp.zeros((TM - 8, CW), F32)
        dua_ref[...] = (dua + jnp.concatenate([zeros, ca_ref[...]], axis=0)).astype(BF16)
        dub_ref[...] = (dub + jnp.concatenate([zeros, cb_ref[...]], axis=0)).astype(BF16)
        ca_ref[...] = dha * keep
        cb_ref[...] = dhb * keep
        dwa_ref[...] += dwa
        dwb_ref[...] += dwb
        dba_ref[...] += dba
        dbb_ref[...] += dbb

    def rev(i):
        return nt - 1 - i

    def tile(off):
        return pl.BlockSpec((TM, CW), lambda j, i, o=off: (rev(i), j + o))

    def halo(off):
        r = TM // 8
        return pl.BlockSpec((8, CW), lambda j, i, o=off: (jnp.maximum(rev(i) * r - 1, 0), j + o))

    def col(rows, off):
        return pl.BlockSpec((rows, CW), lambda j, i, o=off: (0, j + o))

    outs = pl.pallas_call(
        body, name=name,
        out_shape=(jax.ShapeDtypeStruct((t, D_FF), BF16), jax.ShapeDtypeStruct((t, D_FF), BF16),
                   jax.ShapeDtypeStruct((3, D_FF), F32), jax.ShapeDtypeStruct((3, D_FF), F32),
                   jax.ShapeDtypeStruct((1, D_FF), F32), jax.ShapeDtypeStruct((1, D_FF), F32)),
        grid=(NCW, nt),
        in_specs=[tile(0), tile(NCW), halo(0), halo(NCW), col(3, 0), col(3, NCW), col(1, 0), col(1, NCW), tile(0)],
        out_specs=(tile(0), tile(0), col(3, 0), col(3, 0), col(1, 0), col(1, 0)),
        scratch_shapes=[pltpu.VMEM((8, CW), F32), pltpu.VMEM((8, CW), F32)],
        compiler_params=_cp(("parallel", "arbitrary")),
    )(u0, u0, u0, u0, cw, cw, _row(cb), _row(cb), dgact)
    dua, dub, dwa, dwb, dba, dbb = outs
    return (jnp.concatenate([dua, dub], axis=1), jnp.concatenate([dwa, dwb], axis=1), jnp.concatenate([dba, dbb], axis=1))


def _pick(v, h, axis):
    return v[:, h:h + 1] if axis == 1 else v[h:h + 1, :]


def _ssd_chunk(z, xh, xc, dtp, hin, cw, cb, dtb, alog, dsk, ng):
    lane_hi = lax.broadcasted_iota(jnp.int32, (Q, Q), 1) >= 64
    row_hi = lax.broadcasted_iota(jnp.int32, (Q, Q), 0) >= 64
    causal = lax.broadcasted_iota(jnp.int32, (Q, Q), 0) >= lax.broadcasted_iota(jnp.int32, (Q, Q), 1)
    win = jnp.concatenate([xh, xc], axis=0)
    xbc = cb
    for k in range(4):
        xbc = xbc + cw[k:k + 1] * win[5 + k:5 + k + Q]
    xbc = _silu(xbc)
    xs, bm, cm = xbc[:, 0:512], xbc[:, 512:768], xbc[:, 768:1024]
    dt = jax.nn.softplus(dtp + dtb)
    da = dt * (-jnp.exp(alog))
    ah = jnp.dot(causal.astype(F32), da, precision=lax.Precision.HIGHEST, preferred_element_type=F32)
    aht = ah.T
    alast = ah[Q - 1:Q, :]
    eah = jnp.exp(ah)
    dte = jnp.exp(alast - ah)
    elast = jnp.exp(alast)
    ys, houts = [], []
    for g in range(2):
        bg, cg = bm[:, 128 * g:128 * g + 128], cm[:, 128 * g:128 * g + 128]
        cbm = mm_nt(cg, bg)
        for jp in range(2):
            j = 2 * g + jp
            h0, h1 = 2 * j, 2 * j + 1
            xp = xs[:, 128 * j:128 * j + 128]
            xdt = xp * jnp.where(lane_hi, _pick(dt, h1, 1), _pick(dt, h0, 1))
            yd, st = [], []
            for h in (h0, h1):
                seg = _pick(ah, h, 1) - _pick(aht, h, 0)
                decay = jnp.exp(jnp.where(causal, seg, NEG))
                yd.append(mm(cbm * decay, xdt))
                st.append(mm_tn(xdt * _pick(dte, h, 1), bg))
            hj = hin[j]
            hout = hj * jnp.where(row_hi, _pick(elast, h1, 1), _pick(elast, h0, 1)) + jnp.where(row_hi, st[1], st[0])
            yoff = mm_nt(cg, hj) * jnp.where(lane_hi, _pick(eah, h1, 1), _pick(eah, h0, 1))
            skip = xp * jnp.where(lane_hi[0:1], _pick(dsk, h1, 1), _pick(dsk, h0, 1))
            ys.append(jnp.where(lane_hi, yd[1], yd[0]) + yoff + skip)
            houts.append(hout)
    y = jnp.concatenate(ys, axis=1) * _silu(z)
    yn = []
    for g in range(2):
        yg = y[:, 256 * g:256 * g + 256]
        yn.append(yg * lax.rsqrt(jnp.mean(yg * yg, axis=-1, keepdims=True) + EPS))
    return jnp.concatenate(yn, axis=1) * ng, jnp.stack(houts)


def _pad_lanes(v, n=128):
    v = v.reshape(1, -1)
    return jnp.pad(v, ((0, 0), (0, n - v.shape[1])))


def _ssd_in_specs(chunk_of):
    return [pl.BlockSpec((Q, 512), lambda i: (chunk_of(i), P_Z // 512)),
            pl.BlockSpec((8, 1024), lambda i: (jnp.maximum(chunk_of(i) * (Q // 8) - 1, 0), P_XBC // 1024)),
            pl.BlockSpec((Q, 1024), lambda i: (chunk_of(i), P_XBC // 1024)),
            pl.BlockSpec((Q, 128), lambda i: (chunk_of(i), P_DT // 128))]


def _full(shape):
    nd = len(shape)
    return pl.BlockSpec(shape, lambda i: (0,) * nd)


def ssd_fwd(proj, cw, cb, dtb, alog, dsk, ng, name):
    t = proj.shape[0]
    nc = t // Q

    def body(z_ref, xh_ref, xc_ref, dt_ref, cw_ref, cb_ref, dtb_ref, al_ref, dsk_ref, ng_ref, y_ref, hs_ref, h_ref):
        i = pl.program_id(0)

        @pl.when(i == 0)
        def _():
            h_ref[...] = jnp.zeros_like(h_ref)

        hin = h_ref[...]
        hs_ref[0] = hin
        y, hout = _ssd_chunk(z_ref[...], xh_ref[...] * (i > 0).astype(F32), xc_ref[...], dt_ref[...], hin,
                             cw_ref[...], cb_ref[...], dtb_ref[...], al_ref[...], dsk_ref[...], ng_ref[...])
        y_ref[...] = y
        h_ref[...] = hout

    return pl.pallas_call(
        body, name=name,
        out_shape=(jax.ShapeDtypeStruct((t, 512), F32), jax.ShapeDtypeStruct((nc, 4, 128, 128), F32)), grid=(nc,),
        in_specs=_ssd_in_specs(lambda i: i) + [_full((4, 1024)), _full((1, 1024)), _full((1, 128)), _full((1, 128)),
                                               _full((1, 128)), _full((1, 512))],
        out_specs=(pl.BlockSpec((Q, 512), lambda i: (i, 0)), pl.BlockSpec((1, 4, 128, 128), lambda i: (i, 0, 0, 0))),
        scratch_shapes=[pltpu.VMEM((4, 128, 128), F32)], compiler_params=_cp(("arbitrary",)),
    )(proj, proj, proj, proj, cw, _row(cb), _pad_lanes(dtb), _pad_lanes(alog), _pad_lanes(dsk), _row(ng))


def ssd_bwd(proj, hs, dy, cw, cb, dtb, alog, dsk, ng, name):
    t = proj.shape[0]
    nc = t // Q

    def body(z_ref, xh_ref, xc_ref, dt_ref, hs_ref, dy_ref, cw_ref, cb_ref, dtb_ref, al_ref, dsk_ref, ng_ref,
             dz_ref, dx_ref, ddt_ref, dcw_ref, vec_ref, dh_ref, carry_ref):
        step = pl.program_id(0)
        keep = (step < nc - 1).astype(F32)

        @pl.when(step == 0)
        def _():
            dh_ref[...] = jnp.zeros_like(dh_ref)
            carry_ref[...] = jnp.zeros_like(carry_ref)
            dcw_ref[...] = jnp.zeros_like(dcw_ref)
            vec_ref[...] = jnp.zeros_like(vec_ref)

        _, vjp = jax.vjp(_ssd_chunk, z_ref[...], xh_ref[...] * keep, xc_ref[...], dt_ref[...], hs_ref[0],
                         cw_ref[...], cb_ref[...], dtb_ref[...], al_ref[...], dsk_ref[...], ng_ref[...])
        dz, dxh, dxc, ddt, dhin, dcw, dcb, ddtb, dal, ddsk, dng = vjp((dy_ref[...], dh_ref[...]))
        dz_ref[...] = dz
        dx_ref[...] = dxc + jnp.concatenate([jnp.zeros((Q - 8, 1024), F32), carry_ref[...]], axis=0)
        ddt_ref[...] = ddt
        carry_ref[...] = dxh * keep
        dh_ref[...] = dhin
        dcw_ref[...] += dcw
        vec_ref[0:1, :] += dcb
        vec_ref[1:2, 0:128] += ddtb
        vec_ref[2:3, 0:128] += dal
        vec_ref[3:4, 0:128] += ddsk
        vec_ref[4:5, 0:512] += dng

    def rev(i):
        return nc - 1 - i

    return pl.pallas_call(
        body, name=name,
        out_shape=(jax.ShapeDtypeStruct((t, 512), F32), jax.ShapeDtypeStruct((t, 1024), F32), jax.ShapeDtypeStruct((t, 128), F32),
                   jax.ShapeDtypeStruct((4, 1024), F32), jax.ShapeDtypeStruct((8, 1024), F32)),
        grid=(nc,),
        in_specs=_ssd_in_specs(rev) + [pl.BlockSpec((1, 4, 128, 128), lambda i: (rev(i), 0, 0, 0)),
                                       pl.BlockSpec((Q, 512), lambda i: (rev(i), 0)),
                                       _full((4, 1024)), _full((1, 1024)), _full((1, 128)), _full((1, 128)), _full((1, 128)),
                                       _full((1, 512))],
        out_specs=(pl.BlockSpec((Q, 512), lambda i: (rev(i), 0)), pl.BlockSpec((Q, 1024), lambda i: (rev(i), 0)),
                   pl.BlockSpec((Q, 128), lambda i: (rev(i), 0)), _full((4, 1024)), _full((8, 1024))),
        scratch_shapes=[pltpu.VMEM((4, 128, 128), F32), pltpu.VMEM((8, 1024), F32)], compiler_params=_cp(("arbitrary",)),
    )(proj, proj, proj, proj, hs, dy, cw, _row(cb), _pad_lanes(dtb), _pad_lanes(alog), _pad_lanes(dsk), _row(ng))


TA = 256
MLA_SCALE = 1.0 / math.sqrt(MLA_QK)


def _rope(x1, x2, cos, sin):
    return x1 * cos - x2 * sin, x1 * sin + x2 * cos


def _mla_pre(cq, ckv, kr, cos, sin, qg, kvg, wuq, wukv):
    n = cq.shape[0]
    qh = mm(_rms(cq, qg), wuq)
    kv = mm(_rms(ckv, kvg), wukv)
    kr1, kr2 = _rope(kr[:, 0:16], kr[:, 16:32], cos, sin)
    pad = jnp.zeros((n, 32), F32)
    qs, ks, vs = [], [], []
    for h in range(MLA_HEADS):
        b = qh[:, 128 * h:128 * h + 128]
        q1, q2 = _rope(b[:, 64:80], b[:, 80:96], cos, sin)
        qs.append(jnp.concatenate([b[:, 0:64], q1, q2, pad], axis=1))
        ks.append(jnp.concatenate([kv[:, 128 * h:128 * h + 64], kr1, kr2, pad], axis=1))
        vs.append(kv[:, 128 * h + 64:128 * h + 128])
    return jnp.stack(qs), jnp.stack(ks), jnp.stack(vs)


def _mla_pre_specs():
    return [pl.BlockSpec((TM, 256), lambda i: (i, P_CQ // 256)), pl.BlockSpec((TM, 128), lambda i: (i, P_CKV // 128)),
            pl.BlockSpec((TM, 128), lambda i: (i, P_KR // 128)), pl.BlockSpec((TM, 16), lambda i: (i, 0)),
            pl.BlockSpec((TM, 16), lambda i: (i, 0)), _full((1, 256)), _full((1, 128)), _full((256, 512)), _full((128, 512))]


def _head_tile(w):
    return pl.BlockSpec((MLA_HEADS, TM, w), lambda i: (0, i, 0))


def mla_pre_fwd(proj, cos, sin, qg, kvg, wuq, wukv, name):
    t = proj.shape[0]

    def body(cq_ref, ckv_ref, kr_ref, cos_ref, sin_ref, qg_ref, kvg_ref, wuq_ref, wukv_ref, q_ref, k_ref, v_ref):
        q, k, v = _mla_pre(cq_ref[...], ckv_ref[...], kr_ref[...], cos_ref[...], sin_ref[...], qg_ref[...], kvg_ref[...],
                           wuq_ref[...], wukv_ref[...])
        q_ref[...] = q.astype(BF16)
        k_ref[...] = k.astype(BF16)
        v_ref[...] = v.astype(BF16)

    return pl.pallas_call(
        body, name=name,
        out_shape=(jax.ShapeDtypeStruct((MLA_HEADS, t, 128), BF16), jax.ShapeDtypeStruct((MLA_HEADS, t, 128), BF16),
                   jax.ShapeDtypeStruct((MLA_HEADS, t, 64), BF16)),
        grid=(t // TM,), in_specs=_mla_pre_specs(), out_specs=(_head_tile(128), _head_tile(128), _head_tile(64)),
        compiler_params=_cp(("parallel",)),
    )(proj, proj, proj, cos, sin, _row(qg), _row(kvg), wuq, wukv)


def mla_pre_bwd(proj, cos, sin, qg, kvg, wuq, wukv, dq, dk, dv, name):
    t = proj.shape[0]

    def body(cq_ref, ckv_ref, kr_ref, cos_ref, sin_ref, qg_ref, kvg_ref, wuq_ref, wukv_ref, dq_ref, dk_ref, dv_ref,
             dcq_ref, dckv_ref, dkr_ref, dwuq_ref, dwukv_ref, vec_ref):
        @pl.when(pl.program_id(0) == 0)
        def _():
            dwuq_ref[...] = jnp.zeros_like(dwuq_ref)
            dwukv_ref[...] = jnp.zeros_like(dwukv_ref)
            vec_ref[...] = jnp.zeros_like(vec_ref)

        cos, sin = cos_ref[...], sin_ref[...]
        f = lambda cq, ckv, kr, qg, kvg, wuq, wukv: _mla_pre(cq, ckv, kr, cos, sin, qg, kvg, wuq, wukv)
        _, vjp = jax.vjp(f, cq_ref[...], ckv_ref[...], kr_ref[...], qg_ref[...], kvg_ref[...], wuq_ref[...], wukv_ref[...])
        dcq, dckv, dkr, dqg, dkvg, dwuq, dwukv = vjp((dq_ref[...], dk_ref[...], dv_ref[...]))
        dcq_ref[...] = dcq
        dckv_ref[...] = dckv
        dkr_ref[...] = dkr
        dwuq_ref[...] += dwuq
        dwukv_ref[...] += dwukv
        vec_ref[0:1, :] += dqg
        vec_ref[1:2, 0:128] += dkvg

    return pl.pallas_call(
        body, name=name,
        out_shape=(jax.ShapeDtypeStruct((t, 256), F32), jax.ShapeDtypeStruct((t, 128), F32), jax.ShapeDtypeStruct((t, 128), F32),
                   jax.ShapeDtypeStruct((256, 512), F32), jax.ShapeDtypeStruct((128, 512), F32), jax.ShapeDtypeStruct((8, 256), F32)),
        grid=(t // TM,), in_specs=_mla_pre_specs() + [_head_tile(128), _head_tile(128), _head_tile(64)],
        out_specs=(pl.BlockSpec((TM, 256), lambda i: (i, 0)), pl.BlockSpec((TM, 128), lambda i: (i, 0)),
                   pl.BlockSpec((TM, 128), lambda i: (i, 0)), _full((256, 512)), _full((128, 512)), _full((8, 256))),
        compiler_params=_cp(("arbitrary",)),
    )(proj, proj, proj, cos, sin, _row(qg), _row(kvg), wuq, wukv, dq, dk, dv)


def _causal_mask(i, j):
    qpos = i * TA + lax.broadcasted_iota(jnp.int32, (TA, TA), 0)
    kpos = j * TA + lax.broadcasted_iota(jnp.int32, (TA, TA), 1)
    return kpos <= qpos


def mla_flash_fwd(q, k, v, name):
    h, t, _ = q.shape

    def body(q_ref, k_ref, v_ref, o_ref, lse_ref, m_ref, l_ref, acc_ref):
        i = pl.program_id(1)
        m_ref[...] = jnp.full_like(m_ref, NEG)
        l_ref[...] = jnp.zeros_like(l_ref)
        acc_ref[...] = jnp.zeros_like(acc_ref)
        qb = q_ref[0]

        def step(j, carry):
            rows = pl.ds(pl.multiple_of(j * TA, TA), TA)
            s = _dot(qb, k_ref[0, rows, :], _NT) * MLA_SCALE
            s = jnp.where(_causal_mask(i, j), s, NEG)
            m_new = jnp.maximum(m_ref[...], jnp.max(s, axis=-1, keepdims=True))
            p = jnp.exp(s - m_new)
            alpha = jnp.exp(m_ref[...] - m_new)
            l_ref[...] = alpha * l_ref[...] + jnp.sum(p, axis=-1, keepdims=True)
            acc_ref[...] = alpha * acc_ref[...] + _dot(p, v_ref[0, rows, :], _NN)
            m_ref[...] = m_new
            return carry

        lax.fori_loop(0, i + 1, step, 0)
        o_ref[0] = acc_ref[...] / l_ref[...]
        lse_ref[0] = m_ref[...] + jnp.log(l_ref[...])

    return pl.pallas_call(
        body, name=name,
        out_shape=(jax.ShapeDtypeStruct((h, t, 64), F32), jax.ShapeDtypeStruct((h, t, 1), F32)), grid=(h, t // TA),
        in_specs=[pl.BlockSpec((1, TA, 128), lambda hh, i: (hh, i, 0)), pl.BlockSpec((1, t, 128), lambda hh, i: (hh, 0, 0)),
                  pl.BlockSpec((1, t, 64), lambda hh, i: (hh, 0, 0))],
        out_specs=(pl.BlockSpec((1, TA, 64), lambda hh, i: (hh, i, 0)), pl.BlockSpec((1, TA, 1), lambda hh, i: (hh, i, 0))),
        scratch_shapes=[pltpu.VMEM((TA, 1), F32), pltpu.VMEM((TA, 1), F32), pltpu.VMEM((TA, 64), F32)],
        compiler_params=_cp(("parallel", "parallel")),
    )(q, k, v)


def mla_flash_bwd(q, k, v, o, lse, do, name):
    h, t, _ = q.shape
    nb = t // TA

    def body(q_ref, k_ref, v_ref, o_ref, lse_ref, do_ref, dq_ref, dk_ref, dv_ref):
        j = pl.program_id(1)

        @pl.when(j == 0)
        def _():
            dq_ref[...] = jnp.zeros_like(dq_ref)

        dk_ref[...] = jnp.zeros_like(dk_ref)
        dv_ref[...] = jnp.zeros_like(dv_ref)
        kb, vb = k_ref[0], v_ref[0]

        def step(i, carry):
            rows = pl.ds(pl.multiple_of(i * TA, TA), TA)
            qb, dob = q_ref[0, rows, :], do_ref[0, rows, :]
            s = _dot(qb, kb, _NT) * MLA_SCALE
            p = jnp.where(_causal_mask(i, j), jnp.exp(s - lse_ref[0, rows, :]), 0.0)
            delta = jnp.sum(dob * o_ref[0, rows, :], axis=-1, keepdims=True)
            dv_ref[0] += _dot(p, dob, _TN)
            ds = p * (_dot(dob, vb, _NT) - delta) * MLA_SCALE
            dk_ref[0] += _dot(ds, qb, _TN)
            dq_ref[0, rows, :] += _dot(ds, kb, _NN)
            return carry

        lax.fori_loop(j, nb, step, 0)

    def whole(w):
        return pl.BlockSpec((1, t, w), lambda hh, j: (hh, 0, 0))

    def blk(w):
        return pl.BlockSpec((1, TA, w), lambda hh, j: (hh, j, 0))

    return pl.pallas_call(
        body, name=name,
        out_shape=(jax.ShapeDtypeStruct((h, t, 128), F32), jax.ShapeDtypeStruct((h, t, 128), F32), jax.ShapeDtypeStruct((h, t, 64), F32)),
        grid=(h, nb), in_specs=[whole(128), blk(128), blk(64), whole(64), whole(1), whole(64)],
        out_specs=(whole(128), blk(128), blk(64)), compiler_params=_cp(("parallel", "arbitrary")),
    )(q, k, v, o, lse, do)


SWA_SCALE = 1.0 / 8.0


def _swa_block(q, kp, kc, vp, vc, sinks, has_prev):
    k2 = jnp.concatenate([kp, kc], axis=0)
    v2 = jnp.concatenate([vp, vc], axis=0)
    rel = Q + lax.broadcasted_iota(jnp.int32, (Q, 2 * Q), 0) - lax.broadcasted_iota(jnp.int32, (Q, 2 * Q), 1)
    valid = (rel >= 0) & (rel < Q) & ((lax.broadcasted_iota(jnp.int32, (Q, 2 * Q), 1) >= Q) | has_prev)
    outs = []
    for h in range(4):
        g = h // 2
        s = mm_nt(q[:, 64 * h:64 * h + 64], k2[:, 64 * g:64 * g + 64]) * SWA_SCALE
        s = jnp.where(valid, s, NEG)
        sink = sinks[:, h:h + 1]
        m = jnp.maximum(jnp.max(s, axis=-1, keepdims=True), sink)
        e = jnp.exp(s - m)
        p = e / (jnp.sum(e, axis=-1, keepdims=True) + jnp.exp(sink - m))
        outs.append(mm(p, v2[:, 64 * g:64 * g + 64]))
    return jnp.concatenate(outs, axis=1)


def _swa_specs(blk_of):
    def prev(i):
        return jnp.maximum(blk_of(i) - 1, 0)

    return [pl.BlockSpec((Q, 256), lambda i: (blk_of(i), P_SQ // 256)),
            pl.BlockSpec((Q, 128), lambda i: (prev(i), P_SK // 128)), pl.BlockSpec((Q, 128), lambda i: (blk_of(i), P_SK // 128)),
            pl.BlockSpec((Q, 128), lambda i: (prev(i), P_SV // 128)), pl.BlockSpec((Q, 128), lambda i: (blk_of(i), P_SV // 128)),
            _full((1, 128))]


def swa_fwd(proj, sinks, name):
    t = proj.shape[0]

    def body(q_ref, kp_ref, kc_ref, vp_ref, vc_ref, s_ref, o_ref):
        o_ref[...] = _swa_block(q_ref[...], kp_ref[...], kc_ref[...], vp_ref[...], vc_ref[...], s_ref[...], pl.program_id(0) > 0)

    return pl.pallas_call(
        body, name=name, out_shape=jax.ShapeDtypeStruct((t, 256), F32), grid=(t // Q,), in_specs=_swa_specs(lambda i: i),
        out_specs=pl.BlockSpec((Q, 256), lambda i: (i, 0)), compiler_params=_cp(("parallel",)),
    )(proj, proj, proj, proj, proj, _pad_lanes(sinks))


def swa_bwd(proj, sinks, do, name):
    t = proj.shape[0]
    nb = t // Q

    def body(q_ref, kp_ref, kc_ref, vp_ref, vc_ref, s_ref, do_ref, dq_ref, dk_ref, dv_ref, ds_ref, ck_ref, cv_ref):
        step = pl.program_id(0)

        @pl.when(step == 0)
        def _():
            ck_ref[...] = jnp.zeros_like(ck_ref)
            cv_ref[...] = jnp.zeros_like(cv_ref)
            ds_ref[...] = jnp.zeros_like(ds_ref)

        has_prev = step < nb - 1
        f = lambda q, kp, kc, vp, vc, s: _swa_block(q, kp, kc, vp, vc, s, has_prev)
        _, vjp = jax.vjp(f, q_ref[...], kp_ref[...], kc_ref[...], vp_ref[...], vc_ref[...], s_ref[...])
        dq, dkp, dkc, dvp, dvc, dsk = vjp(do_ref[...])
        dq_ref[...] = dq
        dk_ref[...] = dkc + ck_ref[...]
        dv_ref[...] = dvc + cv_ref[...]
        ck_ref[...] = dkp
        cv_ref[...] = dvp
        ds_ref[0:1, :] += dsk

    def rev(i):
        return nb - 1 - i

    return pl.pallas_call(
        body, name=name,
        out_shape=(jax.ShapeDtypeStruct((t, 256), F32), jax.ShapeDtypeStruct((t, 128), F32), jax.ShapeDtypeStruct((t, 128), F32),
                   jax.ShapeDtypeStruct((8, 128), F32)),
        grid=(nb,), in_specs=_swa_specs(rev) + [pl.BlockSpec((Q, 256), lambda i: (rev(i), 0))],
        out_specs=(pl.BlockSpec((Q, 256), lambda i: (rev(i), 0)), pl.BlockSpec((Q, 128), lambda i: (rev(i), 0)),
                   pl.BlockSpec((Q, 128), lambda i: (rev(i), 0)), _full((8, 128))),
        scratch_shapes=[pltpu.VMEM((Q, 128), F32), pltpu.VMEM((Q, 128), F32)], compiler_params=_cp(("arbitrary",)),
    )(proj, proj, proj, proj, proj, _pad_lanes(sinks), do)


def _loss_tile(x, g, tgt):
    err = jnp.square(_rms(x, g) - tgt)
    return 0.5 * jnp.sum(jnp.mean(err, axis=-1, keepdims=True), axis=0, keepdims=True)


def loss_fwd_bwd(x, g, tgt, name):
    t = x.shape[0]

    def body(x_ref, g_ref, t_ref, loss_ref, dx_ref, dg_ref):
        @pl.when(pl.program_id(0) == 0)
        def _():
            loss_ref[...] = jnp.zeros_like(loss_ref)
            dg_ref[...] = jnp.zeros_like(dg_ref)

        tgt = t_ref[...]
        val, vjp = jax.vjp(lambda x, g: _loss_tile(x, g, tgt), x_ref[...], g_ref[...])
        dx, dg = vjp(jnp.ones((1, 1), F32))
        dx_ref[...] = dx
        dg_ref[0:1, :] += dg
        loss_ref[...] += val

    tile = pl.BlockSpec((TM, D), lambda i: (i, 0))
    return pl.pallas_call(
        body, name=name,
        out_shape=(jax.ShapeDtypeStruct((8, 128), F32), jax.ShapeDtypeStruct((t, D), F32), jax.ShapeDtypeStruct((8, D), F32)),
        grid=(t // TM,), in_specs=[tile, _full((1, D)), tile], out_specs=(_full((8, 128)), tile, _full((8, D))),
        compiler_params=_cp(("arbitrary",)),
    )(x, _row(g), tgt)


def adamw(w, g, m, v, name):
    shape = w.shape
    cols = shape[-1] if w.ndim > 1 else shape[0]
    w2, g2, m2, v2 = (a.reshape(-1, cols) for a in (w, g, m, v))
    rows = w2.shape[0]
    br = _blk(rows, max(8, (1 << 19) // cols), 8)

    def body(w_ref, g_ref, m_ref, v_ref, d_ref, nm_ref, nv_ref):
        gg = g_ref[...]
        nm = ADAM_B1 * m_ref[...] + (1.0 - ADAM_B1) * gg
        nv = ADAM_B2 * v_ref[...] + (1.0 - ADAM_B2) * jnp.square(gg)
        m_hat = nm / (1.0 - ADAM_B1 ** ADAM_STEP)
        v_hat = nv / (1.0 - ADAM_B2 ** ADAM_STEP)
        d_ref[...] = -ADAM_LR * (m_hat / (jnp.sqrt(v_hat) + ADAM_EPS) + ADAM_WD * w_ref[...])
        nm_ref[...] = nm
        nv_ref[...] = nv

    spec = pl.BlockSpec((br, cols), lambda i: (i, 0))
    out = jax.ShapeDtypeStruct((rows, cols), F32)
    res = pl.pallas_call(body, name=name, out_shape=(out, out, out), grid=(rows // br,), in_specs=[spec] * 4,
                         out_specs=(spec, spec, spec), compiler_params=_cp(("parallel",)))(w2, g2, m2, v2)
    return tuple(r.reshape(shape) for r in res)


MESH = pl.DeviceIdType.MESH
CHIP_FLIPS = ((1, 0), (0, 1), (1, 1))


def _place():
    return lax.axis_index("x"), lax.axis_index("y"), lax.axis_index("c")


def allgather8(blk, name, in_vmem):
    space = pltpu.VMEM if in_vmem else pl.ANY

    def body(x_ref, out_ref, send_sems, recv_sems, local_sem):
        x, y, c = _place()
        me, sibling = (x, y, c), (x, y, 1 - c)
        chips = [(x ^ fx, y ^ fy) for fx, fy in CHIP_FLIPS]

        def slot(px, py, pc):
            return out_ref.at[4 * px + 2 * py + pc]

        def copy(k, block, to, src=None):
            return pltpu.make_async_remote_copy(
                src_ref=slot(*block) if src is None else src, dst_ref=slot(*block),
                send_sem=send_sems.at[k], recv_sem=recv_sems.at[k], device_id=to, device_id_type=MESH)

        mine = pltpu.make_async_copy(x_ref, slot(*me), local_sem)
        mine.start()
        first = [copy(0, me, sibling, src=x_ref)]
        first += [copy(1 + j, me, (*chip, c), src=x_ref) for j, chip in enumerate(chips)]
        for cp in first:
            cp.start()
        passed = [copy(4 + j, (*chip, c), sibling) for j, chip in enumerate(chips)]
        for j, chip in enumerate(chips):
            copy(1 + j, (*chip, c), me).wait_recv()
            passed[j].start()
        copy(0, sibling, me).wait_recv()
        for j, chip in enumerate(chips):
            copy(4 + j, (*chip, 1 - c), me).wait_recv()
        for cp in first + passed:
            cp.wait_send()
        mine.wait()

    return pl.pallas_call(
        body, name=name, out_shape=jax.ShapeDtypeStruct((N_DEV,) + blk.shape, blk.dtype),
        in_specs=[pl.BlockSpec(memory_space=space)], out_specs=pl.BlockSpec(memory_space=space),
        scratch_shapes=[pltpu.SemaphoreType.DMA((7,)), pltpu.SemaphoreType.DMA((7,)), pltpu.SemaphoreType.DMA],
        compiler_params=pltpu.CompilerParams(vmem_limit_bytes=VMEM_LIMIT),
    )(blk)


def flip_exchange(src, plan, n_out, name):
    def body(x_ref, out_ref, send_sems, recv_sems):
        x, y, c = _place()
        copies = []
        for k, (flip, src_index, dst_slot) in enumerate(plan):
            s, d = x_ref.at[src_index(x, y, c)], out_ref.at[dst_slot(x, y, c)]
            if flip is None:
                copies.append(pltpu.make_async_copy(s, d, send_sems.at[k]))
            else:
                copies.append(pltpu.make_async_remote_copy(
                    src_ref=s, dst_ref=d, send_sem=send_sems.at[k], recv_sem=recv_sems.at[k],
                    device_id=(x ^ flip[0], y ^ flip[1], c ^ flip[2]), device_id_type=MESH))
        for cp in copies:
            cp.start()
        for (flip, _, _), cp in zip(plan, copies):
            if flip is None:
                cp.wait()
            else:
                cp.wait_recv()
                cp.wait_send()

    n = len(plan)
    return pl.pallas_call(
        body, name=name, out_shape=jax.ShapeDtypeStruct((n_out,) + src.shape[1:], src.dtype),
        in_specs=[pl.BlockSpec(memory_space=pl.ANY)], out_specs=pl.BlockSpec(memory_space=pl.ANY),
        scratch_shapes=[pltpu.SemaphoreType.DMA((n,)), pltpu.SemaphoreType.DMA((n,))],
    )(src)


ROWS_ADD = 2048


def add_pairs(a, b, out_dtype, name):
    n = a.shape[0]
    br = _blk(n, ROWS_ADD, 16)

    def body(a_ref, b_ref, o_ref):
        o_ref[...] = (a_ref[...].astype(F32) + b_ref[...].astype(F32)).astype(o_ref.dtype)

    spec = pl.BlockSpec((br, 128), lambda i: (i, 0))
    return pl.pallas_call(body, name=name, out_shape=jax.ShapeDtypeStruct((n, 128), out_dtype), grid=(n // br,),
                          in_specs=[spec, spec], out_specs=spec, compiler_params=_cp(("parallel",)))(a, b)


def add_slots(own, others, name):
    n = own.shape[0]
    ns = others.shape[0]
    br = _blk(n, ROWS_ADD, 16)

    def body(a_ref, b_ref, o_ref):
        acc = a_ref[...].astype(F32)
        for s in range(ns):
            acc = acc + b_ref[s].astype(F32)
        o_ref[...] = acc

    return pl.pallas_call(body, name=name, out_shape=jax.ShapeDtypeStruct((n, 128), F32), grid=(n // br,),
                          in_specs=[pl.BlockSpec((br, 128), lambda i: (i, 0)), pl.BlockSpec((ns, br, 128), lambda i: (0, i, 0))],
                          out_specs=pl.BlockSpec((br, 128), lambda i: (i, 0)), compiler_params=_cp(("parallel",)))(own, others)


def sum8(g, name):
    r = g.shape[1]

    def body(g_ref, o_ref):
        acc = g_ref[0]
        for s in range(1, N_DEV):
            acc = acc + g_ref[s]
        o_ref[...] = acc

    return pl.pallas_call(body, name=name, out_shape=jax.ShapeDtypeStruct((r, 128), F32))(g)


def ada_mod(c_all, ada_w, ada_b_cols, name):
    def body(c_ref, w_ref, b_ref, o_ref):
        o_ref[0] = mm(_silu(c_ref[...]), w_ref[0]) + b_ref[0]

    n = ada_w.shape[2]
    return pl.pallas_call(
        body, name=name, out_shape=jax.ShapeDtypeStruct((DEPTH, N_DEV, n), F32), grid=(DEPTH,),
        in_specs=[pl.BlockSpec((N_DEV, D), lambda l: (0, 0)), pl.BlockSpec((1, D, n), lambda l: (l, 0, 0)),
                  pl.BlockSpec((1, 1, n), lambda l: (l, 0, 0))],
        out_specs=pl.BlockSpec((1, N_DEV, n), lambda l: (l, 0, 0)), compiler_params=_cp(("parallel",)),
    )(c_all, ada_w, ada_b_cols.reshape(DEPTH, 1, n))


def ada_grad(c_all, dmod_cols, name):
    def body(c_ref, d_ref, o_ref):
        o_ref[0] = mm_tn(_silu(c_ref[...]), d_ref[0])

    n = dmod_cols.shape[2]
    return pl.pallas_call(
        body, name=name, out_shape=jax.ShapeDtypeStruct((DEPTH, D, n), F32), grid=(DEPTH,),
        in_specs=[pl.BlockSpec((N_DEV, D), lambda l: (0, 0)), pl.BlockSpec((1, N_DEV, n), lambda l: (l, 0, 0))],
        out_specs=pl.BlockSpec((1, D, n), lambda l: (l, 0, 0)), compiler_params=_cp(("parallel",)),
    )(c_all, dmod_cols)


def pack_w_in(w):
    out = jnp.zeros(w.shape[:-1] + (NP,), w.dtype)
    for p_off, o_off, width in _PACK:
        out = out.at[..., p_off:p_off + width].set(w[..., o_off:o_off + width])
    return out


def unpack_w_in(w):
    return jnp.concatenate([w[..., p_off:p_off + width] for p_off, _, width in _PACK], axis=-1)


def pack_w_uq(w):
    return jnp.pad(w.reshape(w.shape[:-1] + (MLA_HEADS, MLA_QK)), [(0, 0)] * (w.ndim - 1) + [(0, 0), (0, 32)]).reshape(w.shape[:-1] + (512,))


def unpack_w_uq(w):
    return w.reshape(w.shape[:-1] + (MLA_HEADS, 128))[..., :MLA_QK].reshape(w.shape[:-1] + (MLA_HEADS * MLA_QK,))


def layer_fwd(x, mod, w, cos, sin, tag):
    h1 = modnorm_fwd(x, w["norm1_g"], mod[0], mod[1], tag + "norm1")
    proj = matmul(h1, w["w_in"], "nn", F32, tag + "w_in")
    y_ssd, hs = ssd_fwd(proj, w["ssd_conv_w"], w["ssd_conv_b"], w["ssd_dt_bias"], w["ssd_a_log"], w["ssd_d"], w["ssd_norm_g"], tag + "ssd")
    q, k, v = mla_pre_fwd(proj, cos, sin, w["mla_q_norm_g"], w["mla_kv_norm_g"], w["mla_w_uq"], w["mla_w_ukv"], tag + "mla_pre")
    o, lse = mla_flash_fwd(q, k, v, tag + "mla_attn")
    y_swa = swa_fwd(proj, w["swa_sinks"], tag + "swa")
    t = x.shape[0]
    ycat = jnp.concatenate([y_ssd, jnp.transpose(o, (1, 0, 2)).reshape(t, 256), y_swa], axis=1).astype(BF16)
    y = matmul(ycat, w["w_out"], "nn", F32, tag + "w_out")
    xm = resid_fwd(x, y, mod[2], tag + "res1")
    h2 = modnorm_fwd(xm, w["norm2_g"], mod[3], mod[4], tag + "norm2")
    u0 = matmul(h2, w["ffn_w_up"], "nn", F32, tag + "w_up")
    gact = convglu_fwd(u0, w["ffn_conv_w"], w["ffn_conv_b"], tag + "glu")
    yd = matmul(gact, w["ffn_w_down"], "nn", F32, tag + "w_down")
    xo = resid_fwd(xm, yd, mod[5], tag + "res2")
    return xo, dict(x=x, h1=h1, proj=proj, hs=hs, q=q, k=k, v=v, o=o, lse=lse, ycat=ycat, y=y, xm=xm, h2=h2, u0=u0, gact=gact, yd=yd)


def layer_bwd(dxo, s, mod, w, cos, sin, tag):
    t = dxo.shape[0]
    g = {}
    dyd, dg2 = resid_bwd(dxo, s["yd"], mod[5], tag + "res2_b")
    dgact = matmul(dyd, w["ffn_w_down"], "nt", BF16, tag + "w_down_dx")
    g["ffn_w_down"] = matmul(s["gact"], dyd, "tn", F32, tag + "w_down_dw")
    du0, g["ffn_conv_w"], dcb = convglu_bwd(s["u0"], w["ffn_conv_w"], w["ffn_conv_b"], dgact, tag + "glu_b")
    g["ffn_conv_b"] = dcb[0]
    dh2 = matmul(du0, w["ffn_w_up"], "nt", F32, tag + "w_up_dx")
    g["ffn_w_up"] = matmul(s["h2"], du0, "tn", F32, tag + "w_up_dw")
    dxm, sums2 = modnorm_bwd(s["xm"], w["norm2_g"], mod[3], mod[4], dh2, dxo, tag + "norm2_b")
    g["norm2_g"] = sums2[0]
    dy, dg1 = resid_bwd(dxm, s["y"], mod[2], tag + "res1_b")
    dycat = matmul(dy, w["w_out"], "nt", F32, tag + "w_out_dx")
    g["w_out"] = matmul(s["ycat"], dy, "tn", F32, tag + "w_out_dw")
    proj = s["proj"]
    dz, dxbc, ddt, g["ssd_conv_w"], vec = ssd_bwd(proj, s["hs"], dycat[:, 0:512], w["ssd_conv_w"], w["ssd_conv_b"], w["ssd_dt_bias"],
                                                 w["ssd_a_log"], w["ssd_d"], w["ssd_norm_g"], tag + "ssd_b")
    g["ssd_conv_b"], g["ssd_dt_bias"], g["ssd_a_log"], g["ssd_d"], g["ssd_norm_g"] = vec[0], vec[1, :8], vec[2, :8], vec[3, :8], vec[4, :512]
    do = jnp.transpose(dycat[:, 512:768].reshape(t, MLA_HEADS, 64), (1, 0, 2))
    dq, dk, dv = mla_flash_bwd(s["q"], s["k"], s["v"], s["o"], s["lse"], do, tag + "mla_attn_b")
    dcq, dckv, dkr, g["mla_w_uq"], g["mla_w_ukv"], mvec = mla_pre_bwd(proj, cos, sin, w["mla_q_norm_g"], w["mla_kv_norm_g"],
                                                                    w["mla_w_uq"], w["mla_w_ukv"], dq, dk, dv, tag + "mla_pre_b")
    g["mla_q_norm_g"], g["mla_kv_norm_g"] = mvec[0], mvec[1, :128]
    dsq, dsk, dsv, dsink = swa_bwd(proj, w["swa_sinks"], dycat[:, 768:1024], tag + "swa_b")
    g["swa_sinks"] = dsink[0, :4]
    dproj = jnp.concatenate([dxbc, dz, dcq, dsq, dckv, ddt, dkr, dsk, dsv], axis=1).astype(BF16)
    dh1 = matmul(dproj, w["w_in"], "nt", F32, tag + "w_in_dx")
    g["w_in"] = matmul(s["h1"], dproj, "tn", F32, tag + "w_in_dw")
    dx, sums1 = modnorm_bwd(s["x"], w["norm1_g"], mod[0], mod[1], dh1, dxm, tag + "norm1_b")
    g["norm1_g"] = sums1[0]
    dmod = jnp.stack([sums1[1], sums1[2], dg1[0], sums2[1], sums2[2], dg2[0]])
    return dx, dmod, g


def local_step(x, tgt, mods, ws, final_norm_g, cos, sin):
    saved = []
    for l in range(len(ws)):
        x, s = layer_fwd(x, mods[l], ws[l], cos, sin, f"l{l}_")
        saved.append(s)
    loss, dx, dfg = loss_fwd_bwd(x, final_norm_g, tgt, "loss")
    dmods, grads = [None] * len(ws), [None] * len(ws)
    for l in reversed(range(len(ws))):
        dx, dmods[l], grads[l] = layer_bwd(dx, saved[l], mods[l], ws[l], cos, sin, f"l{l}_")
    return loss, dx, dfg[0], jnp.stack(dmods), grads


WEIGHTS = ("ada_w", "ada_b", "norm1_g", "norm2_g", "w_in", "ssd_conv_w", "ssd_conv_b", "ssd_dt_bias", "ssd_a_log", "ssd_d",
           "ssd_norm_g", "mla_q_norm_g", "mla_w_uq", "mla_kv_norm_g", "mla_w_ukv", "swa_sinks", "w_out", "ffn_w_up",
           "ffn_conv_w", "ffn_conv_b", "ffn_w_down", "final_norm_g")
BIG = (("w_in", 2), ("w_out", 1), ("ffn_w_up", 2), ("ffn_w_down", 1), ("mla_w_uq", 2), ("mla_w_ukv", 2))
SMALL = (("dmod", 6 * D), ("norm1_g", D), ("norm2_g", D), ("ssd_conv_w", 4 * SSD_XBC), ("ssd_conv_b", SSD_XBC), ("ssd_dt_bias", 128),
         ("ssd_a_log", 128), ("ssd_d", 128), ("ssd_norm_g", SSD_INNER), ("mla_q_norm_g", 256), ("mla_kv_norm_g", 128),
         ("swa_sinks", 128), ("ffn_conv_w", 3 * 2 * D_FF), ("ffn_conv_b", 2 * D_FF))
SMALL_LAYER = sum(n for _, n in SMALL)
SMALL_SHAPES = {"norm1_g": (D,), "norm2_g": (D,), "ssd_conv_w": (4, SSD_XBC), "ssd_conv_b": (SSD_XBC,), "ssd_dt_bias": (8,),
                "ssd_a_log": (8,), "ssd_d": (8,), "ssd_norm_g": (SSD_INNER,), "mla_q_norm_g": (256,), "mla_kv_norm_g": (128,),
                "swa_sinks": (4,), "ffn_conv_w": (3, 2 * D_FF), "ffn_conv_b": (2 * D_FF,)}


def _lanes(v, n):
    v = v.reshape(-1)
    return jnp.pad(v, (0, n - v.shape[0]))


def _tile_rows(flat):
    n = -(-flat.shape[0] // 1024) * 1024
    return jnp.pad(flat, (0, n - flat.shape[0])).reshape(-1, 128)


def _rope_tables(positions):
    inv_freq = 10000.0 ** (-jnp.arange(0, 32, 2, dtype=F32) / 32)
    ang = positions.astype(F32).reshape(-1, 1) * inv_freq
    return jnp.cos(ang), jnp.sin(ang)


def _gather_big(shards, c):
    n_shard = sum(s.size for s in shards)
    rb = n_shard // 256
    flat = jnp.concatenate([s.reshape(-1) for s in shards]).astype(BF16).reshape(2, rb, 128)
    got = allgather8(lax.dynamic_index_in_dim(flat, c, 0, keepdims=False), "ag_weights", False).reshape(N_CHIP, n_shard)
    fulls, off = [], 0
    for (_, axis), s in zip(BIG, shards):
        pieces = got[:, off:off + s.size].reshape((N_CHIP,) + s.shape)
        fulls.append(jnp.concatenate([pieces[k] for k in range(N_CHIP)], axis=axis))
        off += s.size
    return fulls


def _reduce_big(grads, x, y, c):
    chips = []
    for k in range(N_CHIP):
        parts = []
        for (_, axis), g in zip(BIG, grads):
            n = g.shape[axis] // N_CHIP
            parts.append(lax.slice_in_dim(g, k * n, (k + 1) * n, axis=axis).reshape(-1))
        chips.append(jnp.concatenate(parts))
    n_shard = chips[0].shape[0]
    rb = n_shard // 256
    halves = jnp.stack(chips).astype(BF16).reshape(N_CHIP, 2, rb, 128).transpose(1, 0, 2, 3).reshape(2, N_CHIP * rb, 128)
    theirs = flip_exchange(halves, [((0, 0, 1), lambda x, y, c: 1 - c, lambda x, y, c: 0)], 1, "rs_sibling")
    mine = lax.dynamic_index_in_dim(halves, c, 0, keepdims=False)
    chip_sum = add_pairs(mine, theirs[0], BF16, "rs_add_sibling").reshape(N_CHIP, rb, 128)
    plan = [((fx, fy, 0), (lambda x, y, c, fx=fx, fy=fy: 2 * (x ^ fx) + (y ^ fy)), (lambda x, y, c, s=s: s))
            for s, (fx, fy) in enumerate(CHIP_FLIPS)]
    others = flip_exchange(chip_sum, plan, len(CHIP_FLIPS), "rs_chips")
    own = lax.dynamic_index_in_dim(chip_sum, 2 * x + y, 0, keepdims=False)
    half = add_slots(own, others, "rs_add_chips")
    both = flip_exchange(half[None], [(None, lambda x, y, c: 0, lambda x, y, c: c),
                                      ((0, 0, 1), lambda x, y, c: 0, lambda x, y, c: c)], 2, "rs_share")
    flat = both.reshape(-1)
    out, off = [], 0
    for (_, axis), g in zip(BIG, grads):
        shape = list(g.shape)
        shape[axis] //= N_CHIP
        n = math.prod(shape)
        out.append(flat[off:off + n].reshape(shape))
        off += n
    return out


def kernel(x, c, positions, ada_w, ada_b, norm1_g, norm2_g, w_in, ssd_conv_w, ssd_conv_b, ssd_dt_bias, ssd_a_log, ssd_d, ssd_norm_g, mla_q_norm_g, mla_w_uq, mla_kv_norm_g, mla_w_ukv, swa_sinks, w_out, ffn_w_up, ffn_conv_w, ffn_conv_b, ffn_w_down, final_norm_g, loss_target, m_ada_w, m_ada_b, m_norm1_g, m_norm2_g, m_w_in, m_ssd_conv_w, m_ssd_conv_b, m_ssd_dt_bias, m_ssd_a_log, m_ssd_d, m_ssd_norm_g, m_mla_q_norm_g, m_mla_w_uq, m_mla_kv_norm_g, m_mla_w_ukv, m_swa_sinks, m_w_out, m_ffn_w_up, m_ffn_conv_w, m_ffn_conv_b, m_ffn_w_down, m_final_norm_g, v_ada_w, v_ada_b, v_norm1_g, v_norm2_g, v_w_in, v_ssd_conv_w, v_ssd_conv_b, v_ssd_dt_bias, v_ssd_a_log, v_ssd_d, v_ssd_norm_g, v_mla_q_norm_g, v_mla_w_uq, v_mla_kv_norm_g, v_mla_w_ukv, v_swa_sinks, v_w_out, v_ffn_w_up, v_ffn_conv_w, v_ffn_conv_b, v_ffn_w_down, v_final_norm_g):
    args = locals()
    wt = {n: args[n] for n in WEIGHTS}
    mx, my, mc = _place()
    chip = 2 * mx + my
    dev = 2 * chip + mc
    n_ada = ada_w.shape[2]

    pack = _tile_rows(jnp.concatenate([c.reshape(-1), ssd_conv_w.reshape(-1), ffn_conv_w.reshape(-1)]))
    got = allgather8(pack, "ag_small_in", True).reshape(N_DEV, -1)
    c_all = got[:, :D]
    per_chip = got[0::2]
    n_scw = ssd_conv_w.size
    ssd_cw = jnp.concatenate([per_chip[k, D:D + n_scw].reshape(ssd_conv_w.shape) for k in range(N_CHIP)], axis=2)
    n_fcw = ffn_conv_w.size
    ffn_cw = jnp.concatenate([per_chip[k, D + n_scw:D + n_scw + n_fcw].reshape(ffn_conv_w.shape) for k in range(N_CHIP)], axis=2)

    ada_b_cols = lax.dynamic_slice_in_dim(ada_b, chip * n_ada, n_ada, axis=1)
    mod_cols = ada_mod(c_all, ada_w, ada_b_cols, "ada_mod")
    mod_all = allgather8(mod_cols.reshape(-1, 128), "ag_mod", True)[0::2].reshape(N_CHIP, DEPTH, N_DEV, n_ada)
    mods = lax.dynamic_index_in_dim(mod_all, dev, 2, keepdims=False).transpose(1, 0, 2).reshape(DEPTH, 6, D)

    full = dict(zip([n for n, _ in BIG], _gather_big([wt[n] for n, _ in BIG], mc)))
    full["w_in"] = pack_w_in(full["w_in"])
    full["mla_w_uq"] = pack_w_uq(full["mla_w_uq"])
    ws = []
    for l in range(DEPTH):
        w = {n: full[n][l] for n, _ in BIG}
        w.update(ssd_conv_w=ssd_cw[l], ffn_conv_w=ffn_cw[l])
        for n in ("norm1_g", "norm2_g", "ssd_conv_b", "ssd_dt_bias", "ssd_a_log", "ssd_d", "ssd_norm_g", "mla_q_norm_g",
                  "mla_kv_norm_g", "swa_sinks", "ffn_conv_b"):
            w[n] = wt[n][l]
        ws.append(w)

    cos, sin = _rope_tables(positions)
    t = x.shape[1]
    loss8, dx, dfg, dmods, lg = local_step(x.reshape(t, D), loss_target.reshape(t, D), mods, ws, final_norm_g, cos, sin)
    loss = lax.psum(loss8[0, 0], ("x", "y", "c"))

    rows = []
    for l in range(DEPTH):
        for name, n in SMALL:
            rows.append(_lanes(dmods[l] if name == "dmod" else lg[l][name], n))
    rows.append(dfg)
    small = allgather8(_tile_rows(jnp.concatenate(rows)), "ag_small_grads", True)
    total = sum8(small, "sum_small_grads").reshape(-1)
    grads = {}
    per_layer = {name: [] for name, _ in SMALL}
    for l in range(DEPTH):
        off = l * SMALL_LAYER
        for name, n in SMALL:
            per_layer[name].append(total[off:off + n])
            off += n
    grads["ada_b"] = jnp.stack(per_layer["dmod"])
    for name, shape in SMALL_SHAPES.items():
        grads[name] = jnp.stack([v[:math.prod(shape)].reshape(shape) for v in per_layer[name]])
    grads["final_norm_g"] = total[DEPTH * SMALL_LAYER:DEPTH * SMALL_LAYER + D]
    for name in ("ssd_conv_w", "ffn_conv_w"):
        n = grads[name].shape[2] // N_CHIP
        grads[name] = lax.dynamic_slice_in_dim(grads[name], chip * n, n, axis=2)
    dmod_all = small.reshape(N_DEV, -1)[:, :DEPTH * SMALL_LAYER].reshape(N_DEV, DEPTH, SMALL_LAYER)[:, :, :6 * D]
    dmod_cols = lax.dynamic_slice_in_dim(dmod_all, chip * n_ada, n_ada, axis=2).transpose(1, 0, 2)
    grads["ada_w"] = ada_grad(c_all, dmod_cols, "ada_grad")

    stacked = []
    for name, _ in BIG:
        g = jnp.stack([lg[l][name] for l in range(DEPTH)])
        if name == "w_in":
            g = unpack_w_in(g)
        if name == "mla_w_uq":
            g = unpack_w_uq(g)
        stacked.append(g)
    for (name, _), g in zip(BIG, _reduce_big(stacked, mx, my, mc)):
        grads[name] = g

    deltas, new_m, new_v = {}, {}, {}
    for n in WEIGHTS:
        deltas[n], new_m[n], new_v[n] = adamw(wt[n], grads[n], args["m_" + n], args["v_" + n], "adamw_" + n)
    return (loss, dx.reshape(x.shape), *[grads[n] for n in WEIGHTS], *[deltas[n] for n in WEIGHTS],
            *[new_m[n] for n in WEIGHTS], *[new_v[n] for n in WEIGHTS])
```

```python
import functools
import math

import jax
import jax.numpy as jnp
from jax import lax
from jax.experimental import pallas as pl
from jax.experimental.pallas import tpu as pltpu

F32 = jnp.float32
BF16 = jnp.bfloat16
MXU_DTYPE = BF16

D = 1024
DEPTH = 4
EPS = 1e-6
N_DEV = 8
N_CHIP = 4

SSD_HEADS = 8
SSD_INNER = 512
SSD_STATE = 128
SSD_XBC = 1024
Q = 128
MLA_HEADS = 4
MLA_QK = 96
D_FF = 2816
D_IN = 2472

P_XBC, P_Z, P_CQ, P_SQ, P_CKV, P_DT, P_KR, P_SK, P_SV = 0, 1024, 1536, 1792, 2048, 2176, 2304, 2432, 2560
NP = 2688
_PACK = ((P_Z, 0, 512), (P_XBC, 512, 1024), (P_DT, 1536, 8), (P_CQ, 1544, 256), (P_CKV, 1800, 128),
         (P_KR, 1928, 32), (P_SQ, 1960, 256), (P_SK, 2216, 128), (P_SV, 2344, 128))

ADAM_LR, ADAM_B1, ADAM_B2, ADAM_EPS, ADAM_WD, ADAM_STEP = 0.001, 0.9, 0.999, 1e-08, 0.01, 10

VMEM_LIMIT = 56 * 1024 * 1024
NEG = -1e30


def _cp(sem=None):
    return pltpu.CompilerParams(dimension_semantics=sem, vmem_limit_bytes=VMEM_LIMIT)


def _dot(a, b, dims):
    return lax.dot_general(a.astype(MXU_DTYPE), b.astype(MXU_DTYPE), (dims, ((), ())), preferred_element_type=F32)


_NN = ((1,), (0,))
_NT = ((1,), (1,))
_TN = ((0,), (0,))


@jax.custom_vjp
def mm(a, b):
    return _dot(a, b, _NN)


mm.defvjp(lambda a, b: (_dot(a, b, _NN), (a, b)),
          lambda r, g: (_dot(g, r[1], _NT), _dot(r[0], g, _TN)))


@jax.custom_vjp
def mm_nt(a, b):
    return _dot(a, b, _NT)


mm_nt.defvjp(lambda a, b: (_dot(a, b, _NT), (a, b)),
             lambda r, g: (_dot(g, r[1], _NN), _dot(g, r[0], _TN)))


@jax.custom_vjp
def mm_tn(a, b):
    return _dot(a, b, _TN)


mm_tn.defvjp(lambda a, b: (_dot(a, b, _TN), (a, b)),
             lambda r, g: (_dot(r[1], g, _NT), _dot(r[0], g, _NN)))


def _silu(x):
    return x * jax.nn.sigmoid(x)


def _rms(x, g):
    return x * lax.rsqrt(jnp.mean(x * x, axis=-1, keepdims=True) + EPS) * g


def _modnorm(x, g, sh, sc):
    return _rms(x, g) * (1.0 + sc) + sh


def _blk(dim, target, mult=128):
    best = None
    for b in range(mult, min(dim, target) + 1, mult):
        if dim % b == 0:
            best = b
    return best if best is not None else dim


def matmul(a, b, mode, out_dtype, name):
    if mode == "nn":
        (m, k), n = a.shape, b.shape[1]
    elif mode == "nt":
        (m, k), n = a.shape, b.shape[0]
    else:
        (k, m), n = a.shape, b.shape[1]
    bm, bn, bk = _blk(m, 512), _blk(n, 1024), _blk(k, 512)
    nk = k // bk
    dims = {"nn": _NN, "nt": _NT, "tn": _TN}[mode]

    def body(a_ref, b_ref, o_ref, acc_ref):
        kk = pl.program_id(2)

        @pl.when(kk == 0)
        def _():
            acc_ref[...] = jnp.zeros_like(acc_ref)

        acc_ref[...] += _dot(a_ref[...], b_ref[...], dims)

        @pl.when(kk == nk - 1)
        def _():
            o_ref[...] = acc_ref[...].astype(o_ref.dtype)

    a_spec = pl.BlockSpec((bk, bm), lambda i, j, kk: (kk, i)) if mode == "tn" else pl.BlockSpec((bm, bk), lambda i, j, kk: (i, kk))
    b_spec = pl.BlockSpec((bn, bk), lambda i, j, kk: (j, kk)) if mode == "nt" else pl.BlockSpec((bk, bn), lambda i, j, kk: (kk, j))
    return pl.pallas_call(
        body, name=name, out_shape=jax.ShapeDtypeStruct((m, n), out_dtype), grid=(m // bm, n // bn, nk),
        in_specs=[a_spec, b_spec], out_specs=pl.BlockSpec((bm, bn), lambda i, j, kk: (i, j)),
        scratch_shapes=[pltpu.VMEM((bm, bn), F32)], compiler_params=_cp(("parallel", "parallel", "arbitrary")),
    )(a, b)


TM = 512


def _row(v):
    return v.reshape(1, -1)


def modnorm_fwd(x, g, sh, sc, name):
    t = x.shape[0]

    def body(x_ref, g_ref, sh_ref, sc_ref, o_ref):
        o_ref[...] = _modnorm(x_ref[...], g_ref[...], sh_ref[...], sc_ref[...]).astype(o_ref.dtype)

    vec = pl.BlockSpec((1, D), lambda i: (0, 0))
    return pl.pallas_call(
        body, name=name, out_shape=jax.ShapeDtypeStruct((t, D), BF16), grid=(t // TM,),
        in_specs=[pl.BlockSpec((TM, D), lambda i: (i, 0)), vec, vec, vec],
        out_specs=pl.BlockSpec((TM, D), lambda i: (i, 0)), compiler_params=_cp(("parallel",)),
    )(x, _row(g), _row(sh), _row(sc))


def modnorm_bwd(x, g, sh, sc, dh, dres, name):
    t = x.shape[0]

    def body(x_ref, g_ref, sh_ref, sc_ref, dh_ref, dres_ref, dx_ref, sums_ref):
        _, vjp = jax.vjp(_modnorm, x_ref[...], g_ref[...], sh_ref[...], sc_ref[...])
        dx, dg, dsh, dsc = vjp(dh_ref[...].astype(F32))
        dx_ref[...] = dx + dres_ref[...]

        @pl.when(pl.program_id(0) == 0)
        def _():
            sums_ref[...] = jnp.zeros_like(sums_ref)

        sums_ref[0:1, :] += dg
        sums_ref[1:2, :] += dsh
        sums_ref[2:3, :] += dsc

    vec = pl.BlockSpec((1, D), lambda i: (0, 0))
    tile = pl.BlockSpec((TM, D), lambda i: (i, 0))
    return pl.pallas_call(
        body, name=name, out_shape=(jax.ShapeDtypeStruct((t, D), F32), jax.ShapeDtypeStruct((8, D), F32)), grid=(t // TM,),
        in_specs=[tile, vec, vec, vec, tile, tile], out_specs=(tile, pl.BlockSpec((8, D), lambda i: (0, 0))),
        compiler_params=_cp(("arbitrary",)),
    )(x, _row(g), _row(sh), _row(sc), dh, dres)


def resid_fwd(x, y, gate, name):
    t = x.shape[0]

    def body(x_ref, y_ref, g_ref, o_ref):
        o_ref[...] = x_ref[...] + g_ref[...] * y_ref[...]

    tile = pl.BlockSpec((TM, D), lambda i: (i, 0))
    return pl.pallas_call(
        body, name=name, out_shape=jax.ShapeDtypeStruct((t, D), F32), grid=(t // TM,),
        in_specs=[tile, tile, pl.BlockSpec((1, D), lambda i: (0, 0))], out_specs=tile, compiler_params=_cp(("parallel",)),
    )(x, y, _row(gate))


def resid_bwd(dxo, y, gate, name):
    t = dxo.shape[0]

    def body(d_ref, y_ref, g_ref, dy_ref, dg_ref):
        d = d_ref[...]
        dy_ref[...] = (d * g_ref[...]).astype(BF16)

        @pl.when(pl.program_id(0) == 0)
        def _():
            dg_ref[...] = jnp.zeros_like(dg_ref)

        dg_ref[0:1, :] += jnp.sum(d * y_ref[...], axis=0, keepdims=True)

    tile = pl.BlockSpec((TM, D), lambda i: (i, 0))
    return pl.pallas_call(
        body, name=name, out_shape=(jax.ShapeDtypeStruct((t, D), BF16), jax.ShapeDtypeStruct((8, D), F32)), grid=(t // TM,),
        in_specs=[tile, tile, pl.BlockSpec((1, D), lambda i: (0, 0))], out_specs=(tile, pl.BlockSpec((8, D), lambda i: (0, 0))),
        compiler_params=_cp(("arbitrary",)),
    )(dxo, y, _row(gate))


CW = 256
NCW = D_FF // CW


def _conv3(u, halo, w, b):
    n = u.shape[0]
    win = jnp.concatenate([halo, u], axis=0)
    return b + w[0:1] * win[6:6 + n] + w[1:2] * win[7:7 + n] + w[2:3] * win[8:8 + n]


def _convglu(ua, ub, ha, hb, wa, wb, ba, bb):
    return _silu(_conv3(ua, ha, wa, ba)) * _conv3(ub, hb, wb, bb)


def _halo_specs(tm, cw, off):
    r = tm // 8
    return pl.BlockSpec((8, cw), lambda j, i, o=off: (jnp.maximum(i * r - 1, 0), j + o))


def convglu_fwd(u0, cw, cb, name):
    t = u0.shape[0]

    def body(ua_ref, ub_ref, ha_ref, hb_ref, wa_ref, wb_ref, ba_ref, bb_ref, o_ref):
        keep = (pl.program_id(1) > 0).astype(F32)
        o_ref[...] = _convglu(ua_ref[...], ub_ref[...], ha_ref[...] * keep, hb_ref[...] * keep,
                              wa_ref[...], wb_ref[...], ba_ref[...], bb_ref[...]).astype(o_ref.dtype)

    def col(rows, off):
        return pl.BlockSpec((rows, CW), lambda j, i, o=off: (0, j + o))

    return pl.pallas_call(
        body, name=name, out_shape=jax.ShapeDtypeStruct((t, D_FF), BF16), grid=(NCW, t // TM),
        in_specs=[pl.BlockSpec((TM, CW), lambda j, i: (i, j)), pl.BlockSpec((TM, CW), lambda j, i: (i, j + NCW)),
                  _halo_specs(TM, CW, 0), _halo_specs(TM, CW, NCW), col(3, 0), col(3, NCW), col(1, 0), col(1, NCW)],
        out_specs=pl.BlockSpec((TM, CW), lambda j, i: (i, j)), compiler_params=_cp(("parallel", "parallel")),
    )(u0, u0, u0, u0, cw, cw, _row(cb), _row(cb))


def convglu_bwd(u0, cw, cb, dgact, name):
    t = u0.shape[0]
    nt = t // TM

    def body(ua_ref, ub_ref, ha_ref, hb_ref, wa_ref, wb_ref, ba_ref, bb_ref, dg_ref,
             dua_ref, dub_ref, dwa_ref, dwb_ref, dba_ref, dbb_ref, ca_ref, cb_ref):
        step = pl.program_id(1)
        keep = (step < nt - 1).astype(F32)

        @pl.when(step == 0)
        def _():
            ca_ref[...] = jnp.zeros_like(ca_ref)
            cb_ref[...] = jnp.zeros_like(cb_ref)
            dwa_ref[...] = jnp.zeros_like(dwa_ref)
            dwb_ref[...] = jnp.zeros_like(dwb_ref)
            dba_ref[...] = jnp.zeros_like(dba_ref)
            dbb_ref[...] = jnp.zeros_like(dbb_ref)

        _, vjp = jax.vjp(_convglu, ua_ref[...], ub_ref[...], ha_ref[...] * keep, hb_ref[...] * keep,
                         wa_ref[...], wb_ref[...], ba_ref[...], bb_ref[...])
        dua, dub, dha, dhb, dwa, dwb, dba, dbb = vjp(dg_ref[...].astype(F32))
        zeros = jnp.zeros((TM - 8, CW), F32)
        dua_ref[...] = (dua + jnp.concatenate([zeros, ca_ref[...]], axis=0)).astype(BF16)
        dub_ref[...] = (dub + jnp.concatenate([zeros, cb_ref[...]], axis=0)).astype(BF16)
        ca_ref[...] = dha * keep
        cb_ref[...] = dhb * keep
        dwa_ref[...] += dwa
        dwb_ref[...] += dwb
        dba_ref[...] += dba
        dbb_ref[...] += dbb

    def rev(i):
        return nt - 1 - i

    def tile(off):
        return pl.BlockSpec((TM, CW), lambda j, i, o=off: (rev(i), j + o))

    def halo(off):
        r = TM // 8
        return pl.BlockSpec((8, CW), lambda j, i, o=off: (jnp.maximum(rev(i) * r - 1, 0), j + o))

    def col(rows, off):
        return pl.BlockSpec((rows, CW), lambda j, i, o=off: (0, j + o))

    outs = pl.pallas_call(
        body, name=name,
        out_shape=(jax.ShapeDtypeStruct((t, D_FF), BF16), jax.ShapeDtypeStruct((t, D_FF), BF16),
                   jax.ShapeDtypeStruct((3, D_FF), F32), jax.ShapeDtypeStruct((3, D_FF), F32),
                   jax.ShapeDtypeStruct((1, D_FF), F32), jax.ShapeDtypeStruct((1, D_FF), F32)),
        grid=(NCW, nt),
        in_specs=[tile(0), tile(NCW), halo(0), halo(NCW), col(3, 0), col(3, NCW), col(1, 0), col(1, NCW), tile(0)],
        out_specs=(tile(0), tile(0), col(3, 0), col(3, 0), col(1, 0), col(1, 0)),
        scratch_shapes=[pltpu.VMEM((8, CW), F32), pltpu.VMEM((8, CW), F32)],
        compiler_params=_cp(("parallel", "arbitrary")),
    )(u0, u0, u0, u0, cw, cw, _row(cb), _row(cb), dgact)
    dua, dub, dwa, dwb, dba, dbb = outs
    return (jnp.concatenate([dua, dub], axis=1), jnp.concatenate([dwa, dwb], axis=1), jnp.concatenate([dba, dbb], axis=1))


def _pick(v, h, axis):
    return v[:, h:h + 1] if axis == 1 else v[h:h + 1, :]


def _ssd_chunk(z, xh, xc, dtp, hin, cw, cb, dtb, alog, dsk, ng):
    lane_hi = lax.broadcasted_iota(jnp.int32, (Q, Q), 1) >= 64
    row_hi = lax.broadcasted_iota(jnp.int32, (Q, Q), 0) >= 64
    causal = lax.broadcasted_iota(jnp.int32, (Q, Q), 0) >= lax.broadcasted_iota(jnp.int32, (Q, Q), 1)
    win = jnp.concatenate([xh, xc], axis=0)
    xbc = cb
    for k in range(4):
        xbc = xbc + cw[k:k + 1] * win[5 + k:5 + k + Q]
    xbc = _silu(xbc)
    xs, bm, cm = xbc[:, 0:512], xbc[:, 512:768], xbc[:, 768:1024]
    dt = jax.nn.softplus(dtp + dtb)
    da = dt * (-jnp.exp(alog))
    ah = jnp.dot(causal.astype(F32), da, precision=lax.Precision.HIGHEST, preferred_element_type=F32)
    aht = ah.T
    alast = ah[Q - 1:Q, :]
    eah = jnp.exp(ah)
    dte = jnp.exp(alast - ah)
    elast = jnp.exp(alast)
    ys, houts = [], []
    for g in range(2):
        bg, cg = bm[:, 128 * g:128 * g + 128], cm[:, 128 * g:128 * g + 128]
        cbm = mm_nt(cg, bg)
        for jp in range(2):
            j = 2 * g + jp
            h0, h1 = 2 * j, 2 * j + 1
            xp = xs[:, 128 * j:128 * j + 128]
            xdt = xp * jnp.where(lane_hi, _pick(dt, h1, 1), _pick(dt, h0, 1))
            yd, st = [], []
            for h in (h0, h1):
                seg = _pick(ah, h, 1) - _pick(aht, h, 0)
                decay = jnp.exp(jnp.where(causal, seg, NEG))
                yd.append(mm(cbm * decay, xdt))
                st.append(mm_tn(xdt * _pick(dte, h, 1), bg))
            hj = hin[j]
            hout = hj * jnp.where(row_hi, _pick(elast, h1, 1), _pick(elast, h0, 1)) + jnp.where(row_hi, st[1], st[0])
            yoff = mm_nt(cg, hj) * jnp.where(lane_hi, _pick(eah, h1, 1), _pick(eah, h0, 1))
            skip = xp * jnp.where(lane_hi[0:1], _pick(dsk, h1, 1), _pick(dsk, h0, 1))
            ys.append(jnp.where(lane_hi, yd[1], yd[0]) + yoff + skip)
            houts.append(hout)
    y = jnp.concatenate(ys, axis=1) * _silu(z)
    yn = []
    for g in range(2):
        yg = y[:, 256 * g:256 * g + 256]
        yn.append(yg * lax.rsqrt(jnp.mean(yg * yg, axis=-1, keepdims=True) + EPS))
    return jnp.concatenate(yn, axis=1) * ng, jnp.stack(houts)


def _pad_lanes(v, n=128):
    v = v.reshape(1, -1)
    return jnp.pad(v, ((0, 0), (0, n - v.shape[1])))


def _ssd_in_specs(chunk_of):
    return [pl.BlockSpec((Q, 512), lambda i: (chunk_of(i), P_Z // 512)),
            pl.BlockSpec((8, 1024), lambda i: (jnp.maximum(chunk_of(i) * (Q // 8) - 1, 0), P_XBC // 1024)),
            pl.BlockSpec((Q, 1024), lambda i: (chunk_of(i), P_XBC // 1024)),
            pl.BlockSpec((Q, 128), lambda i: (chunk_of(i), P_DT // 128))]


def _full(shape):
    nd = len(shape)
    return pl.BlockSpec(shape, lambda i: (0,) * nd)


def ssd_fwd(proj, cw, cb, dtb, alog, dsk, ng, name):
    t = proj.shape[0]
    nc = t // Q

    def body(z_ref, xh_ref, xc_ref, dt_ref, cw_ref, cb_ref, dtb_ref, al_ref, dsk_ref, ng_ref, y_ref, hs_ref, h_ref):
        i = pl.program_id(0)

        @pl.when(i == 0)
        def _():
            h_ref[...] = jnp.zeros_like(h_ref)

        hin = h_ref[...]
        hs_ref[0] = hin
        y, hout = _ssd_chunk(z_ref[...], xh_ref[...] * (i > 0).astype(F32), xc_ref[...], dt_ref[...], hin,
                             cw_ref[...], cb_ref[...], dtb_ref[...], al_ref[...], dsk_ref[...], ng_ref[...])
        y_ref[...] = y
        h_ref[...] = hout

    return pl.pallas_call(
        body, name=name,
        out_shape=(jax.ShapeDtypeStruct((t, 512), F32), jax.ShapeDtypeStruct((nc, 4, 128, 128), F32)), grid=(nc,),
        in_specs=_ssd_in_specs(lambda i: i) + [_full((4, 1024)), _full((1, 1024)), _full((1, 128)), _full((1, 128)),
                                               _full((1, 128)), _full((1, 512))],
        out_specs=(pl.BlockSpec((Q, 512), lambda i: (i, 0)), pl.BlockSpec((1, 4, 128, 128), lambda i: (i, 0, 0, 0))),
        scratch_shapes=[pltpu.VMEM((4, 128, 128), F32)], compiler_params=_cp(("arbitrary",)),
    )(proj, proj, proj, proj, cw, _row(cb), _pad_lanes(dtb), _pad_lanes(alog), _pad_lanes(dsk), _row(ng))


def ssd_bwd(proj, hs, dy, cw, cb, dtb, alog, dsk, ng, name):
    t = proj.shape[0]
    nc = t // Q

    def body(z_ref, xh_ref, xc_ref, dt_ref, hs_ref, dy_ref, cw_ref, cb_ref, dtb_ref, al_ref, dsk_ref, ng_ref,
             dz_ref, dx_ref, ddt_ref, dcw_ref, vec_ref, dh_ref, carry_ref):
        step = pl.program_id(0)
        keep = (step < nc - 1).astype(F32)

        @pl.when(step == 0)
        def _():
            dh_ref[...] = jnp.zeros_like(dh_ref)
            carry_ref[...] = jnp.zeros_like(carry_ref)
            dcw_ref[...] = jnp.zeros_like(dcw_ref)
            vec_ref[...] = jnp.zeros_like(vec_ref)

        _, vjp = jax.vjp(_ssd_chunk, z_ref[...], xh_ref[...] * keep, xc_ref[...], dt_ref[...], hs_ref[0],
                         cw_ref[...], cb_ref[...], dtb_ref[...], al_ref[...], dsk_ref[...], ng_ref[...])
        dz, dxh, dxc, ddt, dhin, dcw, dcb, ddtb, dal, ddsk, dng = vjp((dy_ref[...], dh_ref[...]))
        dz_ref[...] = dz
        dx_ref[...] = dxc + jnp.concatenate([jnp.zeros((Q - 8, 1024), F32), carry_ref[...]], axis=0)
        ddt_ref[...] = ddt
        carry_ref[...] = dxh * keep
        dh_ref[...] = dhin
        dcw_ref[...] += dcw
        vec_ref[0:1, :] += dcb
        vec_ref[1:2, 0:128] += ddtb
        vec_ref[2:3, 0:128] += dal
        vec_ref[3:4, 0:128] += ddsk
        vec_ref[4:5, 0:512] += dng

    def rev(i):
        return nc - 1 - i

    return pl.pallas_call(
        body, name=name,
        out_shape=(jax.ShapeDtypeStruct((t, 512), F32), jax.ShapeDtypeStruct((t, 1024), F32), jax.ShapeDtypeStruct((t, 128), F32),
                   jax.ShapeDtypeStruct((4, 1024), F32), jax.ShapeDtypeStruct((8, 1024), F32)),
        grid=(nc,),
        in_specs=_ssd_in_specs(rev) + [pl.BlockSpec((1, 4, 128, 128), lambda i: (rev(i), 0, 0, 0)),
                                       pl.BlockSpec((Q, 512), lambda i: (rev(i), 0)),
                                       _full((4, 1024)), _full((1, 1024)), _full((1, 128)), _full((1, 128)), _full((1, 128)),
                                       _full((1, 512))],
        out_specs=(pl.BlockSpec((Q, 512), lambda i: (rev(i), 0)), pl.BlockSpec((Q, 1024), lambda i: (rev(i), 0)),
                   pl.BlockSpec((Q, 128), lambda i: (rev(i), 0)), _full((4, 1024)), _full((8, 1024))),
        scratch_shapes=[pltpu.VMEM((4, 128, 128), F32), pltpu.VMEM((8, 1024), F32)], compiler_params=_cp(("arbitrary",)),
    )(proj, proj, proj, proj, hs, dy, cw, _row(cb), _pad_lanes(dtb), _pad_lanes(alog), _pad_lanes(dsk), _row(ng))


TA = 256
MLA_SCALE = 1.0 / math.sqrt(MLA_QK)


def _rope(x1, x2, cos, sin):
    return x1 * cos - x2 * sin, x1 * sin + x2 * cos


def _mla_pre(cq, ckv, kr, cos, sin, qg, kvg, wuq, wukv):
    n = cq.shape[0]
    qh = mm(_rms(cq, qg), wuq)
    kv = mm(_rms(ckv, kvg), wukv)
    kr1, kr2 = _rope(kr[:, 0:16], kr[:, 16:32], cos, sin)
    pad = jnp.zeros((n, 32), F32)
    qs, ks, vs = [], [], []
    for h in range(MLA_HEADS):
        b = qh[:, 128 * h:128 * h + 128]
        q1, q2 = _rope(b[:, 64:80], b[:, 80:96], cos, sin)
        qs.append(jnp.concatenate([b[:, 0:64], q1, q2, pad], axis=1))
        ks.append(jnp.concatenate([kv[:, 128 * h:128 * h + 64], kr1, kr2, pad], axis=1))
        vs.append(kv[:, 128 * h + 64:128 * h + 128])
    return jnp.stack(qs), jnp.stack(ks), jnp.stack(vs)


def _mla_pre_specs():
    return [pl.BlockSpec((TM, 256), lambda i: (i, P_CQ // 256)), pl.BlockSpec((TM, 128), lambda i: (i, P_CKV // 128)),
            pl.BlockSpec((TM, 128), lambda i: (i, P_KR // 128)), pl.BlockSpec((TM, 16), lambda i: (i, 0)),
            pl.BlockSpec((TM, 16), lambda i: (i, 0)), _full((1, 256)), _full((1, 128)), _full((256, 512)), _full((128, 512))]


def _head_tile(w):
    return pl.BlockSpec((MLA_HEADS, TM, w), lambda i: (0, i, 0))


def mla_pre_fwd(proj, cos, sin, qg, kvg, wuq, wukv, name):
    t = proj.shape[0]

    def body(cq_ref, ckv_ref, kr_ref, cos_ref, sin_ref, qg_ref, kvg_ref, wuq_ref, wukv_ref, q_ref, k_ref, v_ref):
        q, k, v = _mla_pre(cq_ref[...], ckv_ref[...], kr_ref[...], cos_ref[...], sin_ref[...], qg_ref[...], kvg_ref[...],
                           wuq_ref[...], wukv_ref[...])
        q_ref[...] = q.astype(BF16)
        k_ref[...] = k.astype(BF16)
        v_ref[...] = v.astype(BF16)

    return pl.pallas_call(
        body, name=name,
        out_shape=(jax.ShapeDtypeStruct((MLA_HEADS, t, 128), BF16), jax.ShapeDtypeStruct((MLA_HEADS, t, 128), BF16),
                   jax.ShapeDtypeStruct((MLA_HEADS, t, 64), BF16)),
        grid=(t // TM,), in_specs=_mla_pre_specs(), out_specs=(_head_tile(128), _head_tile(128), _head_tile(64)),
        compiler_params=_cp(("parallel",)),
    )(proj, proj, proj, cos, sin, _row(qg), _row(kvg), wuq, wukv)


def mla_pre_bwd(proj, cos, sin, qg, kvg, wuq, wukv, dq, dk, dv, name):
    t = proj.shape[0]

    def body(cq_ref, ckv_ref, kr_ref, cos_ref, sin_ref, qg_ref, kvg_ref, wuq_ref, wukv_ref, dq_ref, dk_ref, dv_ref,
             dcq_ref, dckv_ref, dkr_ref, dwuq_ref, dwukv_ref, vec_ref):
        @pl.when(pl.program_id(0) == 0)
        def _():
            dwuq_ref[...] = jnp.zeros_like(dwuq_ref)
            dwukv_ref[...] = jnp.zeros_like(dwukv_ref)
            vec_ref[...] = jnp.zeros_like(vec_ref)

        cos, sin = cos_ref[...], sin_ref[...]
        f = lambda cq, ckv, kr, qg, kvg, wuq, wukv: _mla_pre(cq, ckv, kr, cos, sin, qg, kvg, wuq, wukv)
        _, vjp = jax.vjp(f, cq_ref[...], ckv_ref[...], kr_ref[...], qg_ref[...], kvg_ref[...], wuq_ref[...], wukv_ref[...])
        dcq, dckv, dkr, dqg, dkvg, dwuq, dwukv = vjp((dq_ref[...], dk_ref[...], dv_ref[...]))
        dcq_ref[...] = dcq
        dckv_ref[...] = dckv
        dkr_ref[...] = dkr
        dwuq_ref[...] += dwuq
        dwukv_ref[...] += dwukv
        vec_ref[0:1, :] += dqg
        vec_ref[1:2, 0:128] += dkvg

    return pl.pallas_call(
        body, name=name,
        out_shape=(jax.ShapeDtypeStruct((t, 256), F32), jax.ShapeDtypeStruct((t, 128), F32), jax.ShapeDtypeStruct((t, 128), F32),
                   jax.ShapeDtypeStruct((256, 512), F32), jax.ShapeDtypeStruct((128, 512), F32), jax.ShapeDtypeStruct((8, 256), F32)),
        grid=(t // TM,), in_specs=_mla_pre_specs() + [_head_tile(128), _head_tile(128), _head_tile(64)],
        out_specs=(pl.BlockSpec((TM, 256), lambda i: (i, 0)), pl.BlockSpec((TM, 128), lambda i: (i, 0)),
                   pl.BlockSpec((TM, 128), lambda i: (i, 0)), _full((256, 512)), _full((128, 512)), _full((8, 256))),
        compiler_params=_cp(("arbitrary",)),
    )(proj, proj, proj, cos, sin, _row(qg), _row(kvg), wuq, wukv, dq, dk, dv)


def _causal_mask(i, j):
    qpos = i * TA + lax.broadcasted_iota(jnp.int32, (TA, TA), 0)
    kpos = j * TA + lax.broadcasted_iota(jnp.int32, (TA, TA), 1)
    return kpos <= qpos


def mla_flash_fwd(q, k, v, name):
    h, t, _ = q.shape

    def body(q_ref, k_ref, v_ref, o_ref, lse_ref, m_ref, l_ref, acc_ref):
        i = pl.program_id(1)
        m_ref[...] = jnp.full_like(m_ref, NEG)
        l_ref[...] = jnp.zeros_like(l_ref)
        acc_ref[...] = jnp.zeros_like(acc_ref)
        qb = q_ref[0]

        def step(j, carry, diagonal=False):
            rows = pl.ds(pl.multiple_of(j * TA, TA), TA)
            s = _dot(qb, k_ref[0, rows, :], _NT) * MLA_SCALE
            if diagonal:
                s = jnp.where(_causal_mask(i, j), s, NEG)
            m_new = jnp.maximum(m_ref[...], jnp.max(s, axis=-1, keepdims=True))
            p = jnp.exp(s - m_new)
            alpha = jnp.exp(m_ref[...] - m_new)
            l_ref[...] = alpha * l_ref[...] + jnp.sum(p, axis=-1, keepdims=True)
            acc_ref[...] = alpha * acc_ref[...] + _dot(p, v_ref[0, rows, :], _NN)
            m_ref[...] = m_new
            return carry

        lax.fori_loop(0, i, step, 0)
        step(i, 0, diagonal=True)
        o_ref[0] = acc_ref[...] / l_ref[...]
        lse_ref[0] = m_ref[...] + jnp.log(l_ref[...])

    return pl.pallas_call(
        body, name=name,
        out_shape=(jax.ShapeDtypeStruct((h, t, 64), F32), jax.ShapeDtypeStruct((h, t, 1), F32)), grid=(h, t // TA),
        in_specs=[pl.BlockSpec((1, TA, 128), lambda hh, i: (hh, i, 0)), pl.BlockSpec((1, t, 128), lambda hh, i: (hh, 0, 0)),
                  pl.BlockSpec((1, t, 64), lambda hh, i: (hh, 0, 0))],
        out_specs=(pl.BlockSpec((1, TA, 64), lambda hh, i: (hh, i, 0)), pl.BlockSpec((1, TA, 1), lambda hh, i: (hh, i, 0))),
        scratch_shapes=[pltpu.VMEM((TA, 1), F32), pltpu.VMEM((TA, 1), F32), pltpu.VMEM((TA, 64), F32)],
        compiler_params=_cp(("parallel", "parallel")),
    )(q, k, v)


def mla_flash_bwd(q, k, v, o, lse, do, name):
    h, t, _ = q.shape
    nb = t // TA

    def body(q_ref, k_ref, v_ref, o_ref, lse_ref, do_ref, dq_ref, dk_ref, dv_ref):
        j = pl.program_id(1)

        @pl.when(j == 0)
        def _():
            dq_ref[...] = jnp.zeros_like(dq_ref)

        dk_ref[...] = jnp.zeros_like(dk_ref)
        dv_ref[...] = jnp.zeros_like(dv_ref)
        kb, vb = k_ref[0], v_ref[0]

        def step(i, carry, diagonal=False):
            rows = pl.ds(pl.multiple_of(i * TA, TA), TA)
            qb, dob = q_ref[0, rows, :], do_ref[0, rows, :]
            s = _dot(qb, kb, _NT) * MLA_SCALE
            p = jnp.exp(s - lse_ref[0, rows, :])
            if diagonal:
                p = jnp.where(_causal_mask(i, j), p, 0.0)
            delta = jnp.sum(dob * o_ref[0, rows, :], axis=-1, keepdims=True)
            dv_ref[0] += _dot(p, dob, _TN)
            ds = p * (_dot(dob, vb, _NT) - delta) * MLA_SCALE
            dk_ref[0] += _dot(ds, qb, _TN)
            dq_ref[0, rows, :] += _dot(ds, kb, _NN)
            return carry

        step(j, 0, diagonal=True)
        lax.fori_loop(j + 1, nb, step, 0)

    def whole(w):
        return pl.BlockSpec((1, t, w), lambda hh, j: (hh, 0, 0))

    def blk(w):
        return pl.BlockSpec((1, TA, w), lambda hh, j: (hh, j, 0))

    return pl.pallas_call(
        body, name=name,
        out_shape=(jax.ShapeDtypeStruct((h, t, 128), F32), jax.ShapeDtypeStruct((h, t, 128), F32), jax.ShapeDtypeStruct((h, t, 64), F32)),
        grid=(h, nb), in_specs=[whole(128), blk(128), blk(64), whole(64), whole(1), whole(64)],
        out_specs=(whole(128), blk(128), blk(64)), compiler_params=_cp(("parallel", "arbitrary")),
    )(q, k, v, o, lse, do)


SWA_SCALE = 1.0 / 8.0


def _swa_block(q, kp, kc, vp, vc, sinks, has_prev):
    k2 = jnp.concatenate([kp, kc], axis=0)
    v2 = jnp.concatenate([vp, vc], axis=0)
    rel = Q + lax.broadcasted_iota(jnp.int32, (Q, 2 * Q), 0) - lax.broadcasted_iota(jnp.int32, (Q, 2 * Q), 1)
    valid = (rel >= 0) & (rel < Q) & ((lax.broadcasted_iota(jnp.int32, (Q, 2 * Q), 1) >= Q) | has_prev)
    outs = []
    for h in range(4):
        g = h // 2
        s = mm_nt(q[:, 64 * h:64 * h + 64], k2[:, 64 * g:64 * g + 64]) * SWA_SCALE
        s = jnp.where(valid, s, NEG)
        sink = sinks[:, h:h + 1]
        m = jnp.maximum(jnp.max(s, axis=-1, keepdims=True), sink)
        e = jnp.exp(s - m)
        p = e / (jnp.sum(e, axis=-1, keepdims=True) + jnp.exp(sink - m))
        outs.append(mm(p, v2[:, 64 * g:64 * g + 64]))
    return jnp.concatenate(outs, axis=1)


def _swa_specs(blk_of):
    def prev(i):
        return jnp.maximum(blk_of(i) - 1, 0)

    return [pl.BlockSpec((Q, 256), lambda i: (blk_of(i), P_SQ // 256)),
            pl.BlockSpec((Q, 128), lambda i: (prev(i), P_SK // 128)), pl.BlockSpec((Q, 128), lambda i: (blk_of(i), P_SK // 128)),
            pl.BlockSpec((Q, 128), lambda i: (prev(i), P_SV // 128)), pl.BlockSpec((Q, 128), lambda i: (blk_of(i), P_SV // 128)),
            _full((1, 128))]


def swa_fwd(proj, sinks, name):
    t = proj.shape[0]

    def body(q_ref, kp_ref, kc_ref, vp_ref, vc_ref, s_ref, o_ref):
        o_ref[...] = _swa_block(q_ref[...], kp_ref[...], kc_ref[...], vp_ref[...], vc_ref[...], s_ref[...], pl.program_id(0) > 0)

    return pl.pallas_call(
        body, name=name, out_shape=jax.ShapeDtypeStruct((t, 256), F32), grid=(t // Q,), in_specs=_swa_specs(lambda i: i),
        out_specs=pl.BlockSpec((Q, 256), lambda i: (i, 0)), compiler_params=_cp(("parallel",)),
    )(proj, proj, proj, proj, proj, _pad_lanes(sinks))


def swa_bwd(proj, sinks, do, name):
    t = proj.shape[0]
    nb = t // Q

    def body(q_ref, kp_ref, kc_ref, vp_ref, vc_ref, s_ref, do_ref, dq_ref, dk_ref, dv_ref, ds_ref, ck_ref, cv_ref):
        step = pl.program_id(0)

        @pl.when(step == 0)
        def _():
            ck_ref[...] = jnp.zeros_like(ck_ref)
            cv_ref[...] = jnp.zeros_like(cv_ref)
            ds_ref[...] = jnp.zeros_like(ds_ref)

        has_prev = step < nb - 1
        f = lambda q, kp, kc, vp, vc, s: _swa_block(q, kp, kc, vp, vc, s, has_prev)
        _, vjp = jax.vjp(f, q_ref[...], kp_ref[...], kc_ref[...], vp_ref[...], vc_ref[...], s_ref[...])
        dq, dkp, dkc, dvp, dvc, dsk = vjp(do_ref[...])
        dq_ref[...] = dq
        dk_ref[...] = dkc + ck_ref[...]
        dv_ref[...] = dvc + cv_ref[...]
        ck_ref[...] = dkp
        cv_ref[...] = dvp
        ds_ref[0:1, :] += dsk

    def rev(i):
        return nb - 1 - i

    return pl.pallas_call(
        body, name=name,
        out_shape=(jax.ShapeDtypeStruct((t, 256), F32), jax.ShapeDtypeStruct((t, 128), F32), jax.ShapeDtypeStruct((t, 128), F32),
                   jax.ShapeDtypeStruct((8, 128), F32)),
        grid=(nb,), in_specs=_swa_specs(rev) + [pl.BlockSpec((Q, 256), lambda i: (rev(i), 0))],
        out_specs=(pl.BlockSpec((Q, 256), lambda i: (rev(i), 0)), pl.BlockSpec((Q, 128), lambda i: (rev(i), 0)),
                   pl.BlockSpec((Q, 128), lambda i: (rev(i), 0)), _full((8, 128))),
        scratch_shapes=[pltpu.VMEM((Q, 128), F32), pltpu.VMEM((Q, 128), F32)], compiler_params=_cp(("arbitrary",)),
    )(proj, proj, proj, proj, proj, _pad_lanes(sinks), do)


def _loss_tile(x, g, tgt):
    err = jnp.square(_rms(x, g) - tgt)
    return 0.5 * jnp.sum(jnp.mean(err, axis=-1, keepdims=True), axis=0, keepdims=True)


def loss_fwd_bwd(x, g, tgt, name):
    t = x.shape[0]

    def body(x_ref, g_ref, t_ref, loss_ref, dx_ref, dg_ref):
        @pl.when(pl.program_id(0) == 0)
        def _():
            loss_ref[...] = jnp.zeros_like(loss_ref)
            dg_ref[...] = jnp.zeros_like(dg_ref)

        tgt = t_ref[...]
        val, vjp = jax.vjp(lambda x, g: _loss_tile(x, g, tgt), x_ref[...], g_ref[...])
        dx, dg = vjp(jnp.ones((1, 1), F32))
        dx_ref[...] = dx
        dg_ref[0:1, :] += dg
        loss_ref[...] += val

    tile = pl.BlockSpec((TM, D), lambda i: (i, 0))
    return pl.pallas_call(
        body, name=name,
        out_shape=(jax.ShapeDtypeStruct((8, 128), F32), jax.ShapeDtypeStruct((t, D), F32), jax.ShapeDtypeStruct((8, D), F32)),
        grid=(t // TM,), in_specs=[tile, _full((1, D)), tile], out_specs=(_full((8, 128)), tile, _full((8, D))),
        compiler_params=_cp(("arbitrary",)),
    )(x, _row(g), tgt)


def adamw(w, g, m, v, name):
    shape = w.shape
    cols = shape[-1] if w.ndim > 1 else shape[0]
    w2, g2, m2, v2 = (a.reshape(-1, cols) for a in (w, g, m, v))
    rows = w2.shape[0]
    br = _blk(rows, max(8, (1 << 19) // cols), 8)

    def body(w_ref, g_ref, m_ref, v_ref, d_ref, nm_ref, nv_ref):
        gg = g_ref[...]
        nm = ADAM_B1 * m_ref[...] + (1.0 - ADAM_B1) * gg
        nv = ADAM_B2 * v_ref[...] + (1.0 - ADAM_B2) * jnp.square(gg)
        m_hat = nm / (1.0 - ADAM_B1 ** ADAM_STEP)
        v_hat = nv / (1.0 - ADAM_B2 ** ADAM_STEP)
        d_ref[...] = -ADAM_LR * (m_hat / (jnp.sqrt(v_hat) + ADAM_EPS) + ADAM_WD * w_ref[...])
        nm_ref[...] = nm
        nv_ref[...] = nv

    spec = pl.BlockSpec((br, cols), lambda i: (i, 0))
    out = jax.ShapeDtypeStruct((rows, cols), F32)
    res = pl.pallas_call(body, name=name, out_shape=(out, out, out), grid=(rows // br,), in_specs=[spec] * 4,
                         out_specs=(spec, spec, spec), compiler_params=_cp(("parallel",)))(w2, g2, m2, v2)
    return tuple(r.reshape(shape) for r in res)


MESH = pl.DeviceIdType.MESH
CHIP_FLIPS = ((1, 0), (0, 1), (1, 1))


def _place():
    return lax.axis_index("x"), lax.axis_index("y"), lax.axis_index("c")


def allgather8(blk, name, in_vmem):
    space = pltpu.VMEM if in_vmem else pl.ANY

    def body(x_ref, out_ref, send_sems, recv_sems, local_sem):
        x, y, c = _place()
        me, sibling = (x, y, c), (x, y, 1 - c)
        chips = [(x ^ fx, y ^ fy) for fx, fy in CHIP_FLIPS]

        def slot(px, py, pc):
            return out_ref.at[4 * px + 2 * py + pc]

        def copy(k, block, to, src=None):
            return pltpu.make_async_remote_copy(
                src_ref=slot(*block) if src is None else src, dst_ref=slot(*block),
                send_sem=send_sems.at[k], recv_sem=recv_sems.at[k], device_id=to, device_id_type=MESH)

        mine = pltpu.make_async_copy(x_ref, slot(*me), local_sem)
        mine.start()
        first = [copy(0, me, sibling, src=x_ref)]
        first += [copy(1 + j, me, (*chip, c), src=x_ref) for j, chip in enumerate(chips)]
        for cp in first:
            cp.start()
        passed = [copy(4 + j, (*chip, c), sibling) for j, chip in enumerate(chips)]
        for j, chip in enumerate(chips):
            copy(1 + j, (*chip, c), me).wait_recv()
            passed[j].start()
        copy(0, sibling, me).wait_recv()
        for j, chip in enumerate(chips):
            copy(4 + j, (*chip, 1 - c), me).wait_recv()
        for cp in first + passed:
            cp.wait_send()
        mine.wait()

    return pl.pallas_call(
        body, name=name, out_shape=jax.ShapeDtypeStruct((N_DEV,) + blk.shape, blk.dtype),
        in_specs=[pl.BlockSpec(memory_space=space)], out_specs=pl.BlockSpec(memory_space=space),
        scratch_shapes=[pltpu.SemaphoreType.DMA((7,)), pltpu.SemaphoreType.DMA((7,)), pltpu.SemaphoreType.DMA],
        compiler_params=pltpu.CompilerParams(vmem_limit_bytes=VMEM_LIMIT),
    )(blk)


def flip_exchange(src, plan, n_out, name):
    def body(x_ref, out_ref, send_sems, recv_sems):
        x, y, c = _place()
        copies = []
        for k, (flip, src_index, dst_slot) in enumerate(plan):
            s, d = x_ref.at[src_index(x, y, c)], out_ref.at[dst_slot(x, y, c)]
            if flip is None:
                copies.append(pltpu.make_async_copy(s, d, send_sems.at[k]))
            else:
                copies.append(pltpu.make_async_remote_copy(
                    src_ref=s, dst_ref=d, send_sem=send_sems.at[k], recv_sem=recv_sems.at[k],
                    device_id=(x ^ flip[0], y ^ flip[1], c ^ flip[2]), device_id_type=MESH))
        for cp in copies:
            cp.start()
        for (flip, _, _), cp in zip(plan, copies):
            if flip is None:
                cp.wait()
            else:
                cp.wait_recv()
                cp.wait_send()

    n = len(plan)
    return pl.pallas_call(
        body, name=name, out_shape=jax.ShapeDtypeStruct((n_out,) + src.shape[1:], src.dtype),
        in_specs=[pl.BlockSpec(memory_space=pl.ANY)], out_specs=pl.BlockSpec(memory_space=pl.ANY),
        scratch_shapes=[pltpu.SemaphoreType.DMA((n,)), pltpu.SemaphoreType.DMA((n,))],
    )(src)


ROWS_ADD = 2048


def add_pairs(a, b, out_dtype, name):
    n = a.shape[0]
    br = _blk(n, ROWS_ADD, 16)

    def body(a_ref, b_ref, o_ref):
        o_ref[...] = (a_ref[...].astype(F32) + b_ref[...].astype(F32)).astype(o_ref.dtype)

    spec = pl.BlockSpec((br, 128), lambda i: (i, 0))
    return pl.pallas_call(body, name=name, out_shape=jax.ShapeDtypeStruct((n, 128), out_dtype), grid=(n // br,),
                          in_specs=[spec, spec], out_specs=spec, compiler_params=_cp(("parallel",)))(a, b)


def add_slots(own, others, name):
    n = own.shape[0]
    ns = others.shape[0]
    br = _blk(n, ROWS_ADD, 16)

    def body(a_ref, b_ref, o_ref):
        acc = a_ref[...].astype(F32)
        for s in range(ns):
            acc = acc + b_ref[s].astype(F32)
        o_ref[...] = acc

    return pl.pallas_call(body, name=name, out_shape=jax.ShapeDtypeStruct((n, 128), F32), grid=(n // br,),
                          in_specs=[pl.BlockSpec((br, 128), lambda i: (i, 0)), pl.BlockSpec((ns, br, 128), lambda i: (0, i, 0))],
                          out_specs=pl.BlockSpec((br, 128), lambda i: (i, 0)), compiler_params=_cp(("parallel",)))(own, others)


def sum8(g, name):
    r = g.shape[1]

    def body(g_ref, o_ref):
        acc = g_ref[0]
        for s in range(1, N_DEV):
            acc = acc + g_ref[s]
        o_ref[...] = acc

    return pl.pallas_call(body, name=name, out_shape=jax.ShapeDtypeStruct((r, 128), F32))(g)


def ada_mod(c_all, ada_w, ada_b_cols, name):
    def body(c_ref, w_ref, b_ref, o_ref):
        o_ref[0] = mm(_silu(c_ref[...]), w_ref[0]) + b_ref[0]

    n = ada_w.shape[2]
    return pl.pallas_call(
        body, name=name, out_shape=jax.ShapeDtypeStruct((DEPTH, N_DEV, n), F32), grid=(DEPTH,),
        in_specs=[pl.BlockSpec((N_DEV, D), lambda l: (0, 0)), pl.BlockSpec((1, D, n), lambda l: (l, 0, 0)),
                  pl.BlockSpec((1, 1, n), lambda l: (l, 0, 0))],
        out_specs=pl.BlockSpec((1, N_DEV, n), lambda l: (l, 0, 0)), compiler_params=_cp(("parallel",)),
    )(c_all, ada_w, ada_b_cols.reshape(DEPTH, 1, n))


def ada_grad(c_all, dmod_cols, name):
    def body(c_ref, d_ref, o_ref):
        o_ref[0] = mm_tn(_silu(c_ref[...]), d_ref[0])

    n = dmod_cols.shape[2]
    return pl.pallas_call(
        body, name=name, out_shape=jax.ShapeDtypeStruct((DEPTH, D, n), F32), grid=(DEPTH,),
        in_specs=[pl.BlockSpec((N_DEV, D), lambda l: (0, 0)), pl.BlockSpec((1, N_DEV, n), lambda l: (l, 0, 0))],
        out_specs=pl.BlockSpec((1, D, n), lambda l: (l, 0, 0)), compiler_params=_cp(("parallel",)),
    )(c_all, dmod_cols)


def pack_w_in(w):
    out = jnp.zeros(w.shape[:-1] + (NP,), w.dtype)
    for p_off, o_off, width in _PACK:
        out = out.at[..., p_off:p_off + width].set(w[..., o_off:o_off + width])
    return out


def unpack_w_in(w):
    return jnp.concatenate([w[..., p_off:p_off + width] for p_off, _, width in _PACK], axis=-1)


def pack_w_uq(w):
    return jnp.pad(w.reshape(w.shape[:-1] + (MLA_HEADS, MLA_QK)), [(0, 0)] * (w.ndim - 1) + [(0, 0), (0, 32)]).reshape(w.shape[:-1] + (512,))


def unpack_w_uq(w):
    return w.reshape(w.shape[:-1] + (MLA_HEADS, 128))[..., :MLA_QK].reshape(w.shape[:-1] + (MLA_HEADS * MLA_QK,))


def layer_fwd(x, mod, w, cos, sin, tag):
    h1 = modnorm_fwd(x, w["norm1_g"], mod[0], mod[1], tag + "norm1")
    proj = matmul(h1, w["w_in"], "nn", F32, tag + "w_in")
    y_ssd, hs = ssd_fwd(proj, w["ssd_conv_w"], w["ssd_conv_b"], w["ssd_dt_bias"], w["ssd_a_log"], w["ssd_d"], w["ssd_norm_g"], tag + "ssd")
    q, k, v = mla_pre_fwd(proj, cos, sin, w["mla_q_norm_g"], w["mla_kv_norm_g"], w["mla_w_uq"], w["mla_w_ukv"], tag + "mla_pre")
    o, lse = mla_flash_fwd(q, k, v, tag + "mla_attn")
    y_swa = swa_fwd(proj, w["swa_sinks"], tag + "swa")
    t = x.shape[0]
    ycat = jnp.concatenate([y_ssd, jnp.transpose(o, (1, 0, 2)).reshape(t, 256), y_swa], axis=1).astype(BF16)
    y = matmul(ycat, w["w_out"], "nn", F32, tag + "w_out")
    xm = resid_fwd(x, y, mod[2], tag + "res1")
    h2 = modnorm_fwd(xm, w["norm2_g"], mod[3], mod[4], tag + "norm2")
    u0 = matmul(h2, w["ffn_w_up"], "nn", F32, tag + "w_up")
    gact = convglu_fwd(u0, w["ffn_conv_w"], w["ffn_conv_b"], tag + "glu")
    yd = matmul(gact, w["ffn_w_down"], "nn", F32, tag + "w_down")
    xo = resid_fwd(xm, yd, mod[5], tag + "res2")
    return xo, dict(x=x, h1=h1, proj=proj, hs=hs, q=q, k=k, v=v, o=o, lse=lse, ycat=ycat, y=y, xm=xm, h2=h2, u0=u0, gact=gact, yd=yd)


def layer_bwd(dxo, s, mod, w, cos, sin, tag):
    t = dxo.shape[0]
    g = {}
    dyd, dg2 = resid_bwd(dxo, s["yd"], mod[5], tag + "res2_b")
    dgact = matmul(dyd, w["ffn_w_down"], "nt", BF16, tag + "w_down_dx")
    g["ffn_w_down"] = matmul(s["gact"], dyd, "tn", F32, tag + "w_down_dw")
    du0, g["ffn_conv_w"], dcb = convglu_bwd(s["u0"], w["ffn_conv_w"], w["ffn_conv_b"], dgact, tag + "glu_b")
    g["ffn_conv_b"] = dcb[0]
    dh2 = matmul(du0, w["ffn_w_up"], "nt", F32, tag + "w_up_dx")
    g["ffn_w_up"] = matmul(s["h2"], du0, "tn", F32, tag + "w_up_dw")
    dxm, sums2 = modnorm_bwd(s["xm"], w["norm2_g"], mod[3], mod[4], dh2, dxo, tag + "norm2_b")
    g["norm2_g"] = sums2[0]
    dy, dg1 = resid_bwd(dxm, s["y"], mod[2], tag + "res1_b")
    dycat = matmul(dy, w["w_out"], "nt", F32, tag + "w_out_dx")
    g["w_out"] = matmul(s["ycat"], dy, "tn", F32, tag + "w_out_dw")
    proj = s["proj"]
    dz, dxbc, ddt, g["ssd_conv_w"], vec = ssd_bwd(proj, s["hs"], dycat[:, 0:512], w["ssd_conv_w"], w["ssd_conv_b"], w["ssd_dt_bias"],
                                                 w["ssd_a_log"], w["ssd_d"], w["ssd_norm_g"], tag + "ssd_b")
    g["ssd_conv_b"], g["ssd_dt_bias"], g["ssd_a_log"], g["ssd_d"], g["ssd_norm_g"] = vec[0], vec[1, :8], vec[2, :8], vec[3, :8], vec[4, :512]
    do = jnp.transpose(dycat[:, 512:768].reshape(t, MLA_HEADS, 64), (1, 0, 2))
    dq, dk, dv = mla_flash_bwd(s["q"], s["k"], s["v"], s["o"], s["lse"], do, tag + "mla_attn_b")
    dcq, dckv, dkr, g["mla_w_uq"], g["mla_w_ukv"], mvec = mla_pre_bwd(proj, cos, sin, w["mla_q_norm_g"], w["mla_kv_norm_g"],
                                                                    w["mla_w_uq"], w["mla_w_ukv"], dq, dk, dv, tag + "mla_pre_b")
    g["mla_q_norm_g"], g["mla_kv_norm_g"] = mvec[0], mvec[1, :128]
    dsq, dsk, dsv, dsink = swa_bwd(proj, w["swa_sinks"], dycat[:, 768:1024], tag + "swa_b")
    g["swa_sinks"] = dsink[0, :4]
    dproj = jnp.concatenate([dxbc, dz, dcq, dsq, dckv, ddt, dkr, dsk, dsv], axis=1).astype(BF16)
    dh1 = matmul(dproj, w["w_in"], "nt", F32, tag + "w_in_dx")
    g["w_in"] = matmul(s["h1"], dproj, "tn", F32, tag + "w_in_dw")
    dx, sums1 = modnorm_bwd(s["x"], w["norm1_g"], mod[0], mod[1], dh1, dxm, tag + "norm1_b")
    g["norm1_g"] = sums1[0]
    dmod = jnp.stack([sums1[1], sums1[2], dg1[0], sums2[1], sums2[2], dg2[0]])
    return dx, dmod, g


def local_step(x, tgt, mods, ws, final_norm_g, cos, sin):
    saved = []
    for l in range(len(ws)):
        x, s = layer_fwd(x, mods[l], ws[l], cos, sin, f"l{l}_")
        saved.append(s)
    loss, dx, dfg = loss_fwd_bwd(x, final_norm_g, tgt, "loss")
    dmods, grads = [None] * len(ws), [None] * len(ws)
    for l in reversed(range(len(ws))):
        dx, dmods[l], grads[l] = layer_bwd(dx, saved[l], mods[l], ws[l], cos, sin, f"l{l}_")
    return loss, dx, dfg[0], jnp.stack(dmods), grads


WEIGHTS = ("ada_w", "ada_b", "norm1_g", "norm2_g", "w_in", "ssd_conv_w", "ssd_conv_b", "ssd_dt_bias", "ssd_a_log", "ssd_d",
           "ssd_norm_g", "mla_q_norm_g", "mla_w_uq", "mla_kv_norm_g", "mla_w_ukv", "swa_sinks", "w_out", "ffn_w_up",
           "ffn_conv_w", "ffn_conv_b", "ffn_w_down", "final_norm_g")
BIG = (("w_in", 2), ("w_out", 1), ("ffn_w_up", 2), ("ffn_w_down", 1), ("mla_w_uq", 2), ("mla_w_ukv", 2))
SMALL = (("dmod", 6 * D), ("norm1_g", D), ("norm2_g", D), ("ssd_conv_w", 4 * SSD_XBC), ("ssd_conv_b", SSD_XBC), ("ssd_dt_bias", 128),
         ("ssd_a_log", 128), ("ssd_d", 128), ("ssd_norm_g", SSD_INNER), ("mla_q_norm_g", 256), ("mla_kv_norm_g", 128),
         ("swa_sinks", 128), ("ffn_conv_w", 3 * 2 * D_FF), ("ffn_conv_b", 2 * D_FF))
SMALL_LAYER = sum(n for _, n in SMALL)
SMALL_SHAPES = {"norm1_g": (D,), "norm2_g": (D,), "ssd_conv_w": (4, SSD_XBC), "ssd_conv_b": (SSD_XBC,), "ssd_dt_bias": (8,),
                "ssd_a_log": (8,), "ssd_d": (8,), "ssd_norm_g": (SSD_INNER,), "mla_q_norm_g": (256,), "mla_kv_norm_g": (128,),
                "swa_sinks": (4,), "ffn_conv_w": (3, 2 * D_FF), "ffn_conv_b": (2 * D_FF,)}


def _lanes(v, n):
    v = v.reshape(-1)
    return jnp.pad(v, (0, n - v.shape[0]))


def _tile_rows(flat):
    n = -(-flat.shape[0] // 1024) * 1024
    return jnp.pad(flat, (0, n - flat.shape[0])).reshape(-1, 128)


def _rope_tables(positions):
    inv_freq = 10000.0 ** (-jnp.arange(0, 32, 2, dtype=F32) / 32)
    ang = positions.astype(F32).reshape(-1, 1) * inv_freq
    return jnp.cos(ang), jnp.sin(ang)


def _gather_big(shards, c):
    rb = sum(s.size for s in shards) // 256
    flat = jnp.concatenate([s.astype(BF16).reshape(-1, 128) for s in shards], axis=0).reshape(2, rb, 128)
    got = allgather8(lax.dynamic_index_in_dim(flat, c, 0, keepdims=False), "ag_weights", False).reshape(N_CHIP, 2 * rb, 128)
    fulls, off = [], 0
    for (_, axis), s in zip(BIG, shards):
        rows = s.size // 128
        fulls.append(jnp.concatenate([got[k, off:off + rows].reshape(s.shape) for k in range(N_CHIP)], axis=axis))
        off += rows
    return fulls


def _reduce_big(grads, x, y, c):
    chips = []
    for k in range(N_CHIP):
        parts = []
        for (_, axis), g in zip(BIG, grads):
            n = g.shape[axis] // N_CHIP
            parts.append(lax.slice_in_dim(g, k * n, (k + 1) * n, axis=axis).astype(BF16).reshape(-1, 128))
        chips.append(jnp.concatenate(parts, axis=0))
    rb = chips[0].shape[0] // 2
    halves = jnp.stack(chips).reshape(N_CHIP, 2, rb, 128).transpose(1, 0, 2, 3).reshape(2, N_CHIP * rb, 128)
    theirs = flip_exchange(halves, [((0, 0, 1), lambda x, y, c: 1 - c, lambda x, y, c: 0)], 1, "rs_sibling")
    mine = lax.dynamic_index_in_dim(halves, c, 0, keepdims=False)
    chip_sum = add_pairs(mine, theirs[0], BF16, "rs_add_sibling").reshape(N_CHIP, rb, 128)
    plan = [((fx, fy, 0), (lambda x, y, c, fx=fx, fy=fy: 2 * (x ^ fx) + (y ^ fy)), (lambda x, y, c, s=s: s))
            for s, (fx, fy) in enumerate(CHIP_FLIPS)]
    others = flip_exchange(chip_sum, plan, len(CHIP_FLIPS), "rs_chips")
    own = lax.dynamic_index_in_dim(chip_sum, 2 * x + y, 0, keepdims=False)
    half = add_slots(own, others, "rs_add_chips")
    both = flip_exchange(half[None], [(None, lambda x, y, c: 0, lambda x, y, c: c),
                                      ((0, 0, 1), lambda x, y, c: 0, lambda x, y, c: c)], 2, "rs_share")
    flat = both.reshape(2 * rb, 128)
    out, off = [], 0
    for (_, axis), g in zip(BIG, grads):
        shape = list(g.shape)
        shape[axis] //= N_CHIP
        rows = math.prod(shape) // 128
        out.append(flat[off:off + rows].reshape(shape))
        off += rows
    return out


def kernel(x, c, positions, ada_w, ada_b, norm1_g, norm2_g, w_in, ssd_conv_w, ssd_conv_b, ssd_dt_bias, ssd_a_log, ssd_d, ssd_norm_g, mla_q_norm_g, mla_w_uq, mla_kv_norm_g, mla_w_ukv, swa_sinks, w_out, ffn_w_up, ffn_conv_w, ffn_conv_b, ffn_w_down, final_norm_g, loss_target, m_ada_w, m_ada_b, m_norm1_g, m_norm2_g, m_w_in, m_ssd_conv_w, m_ssd_conv_b, m_ssd_dt_bias, m_ssd_a_log, m_ssd_d, m_ssd_norm_g, m_mla_q_norm_g, m_mla_w_uq, m_mla_kv_norm_g, m_mla_w_ukv, m_swa_sinks, m_w_out, m_ffn_w_up, m_ffn_conv_w, m_ffn_conv_b, m_ffn_w_down, m_final_norm_g, v_ada_w, v_ada_b, v_norm1_g, v_norm2_g, v_w_in, v_ssd_conv_w, v_ssd_conv_b, v_ssd_dt_bias, v_ssd_a_log, v_ssd_d, v_ssd_norm_g, v_mla_q_norm_g, v_mla_w_uq, v_mla_kv_norm_g, v_mla_w_ukv, v_swa_sinks, v_w_out, v_ffn_w_up, v_ffn_conv_w, v_ffn_conv_b, v_ffn_w_down, v_final_norm_g):
    args = locals()
    wt = {n: args[n] for n in WEIGHTS}
    mx, my, mc = _place()
    chip = 2 * mx + my
    dev = 2 * chip + mc
    n_ada = ada_w.shape[2]

    pack = _tile_rows(jnp.concatenate([c.reshape(-1), ssd_conv_w.reshape(-1), ffn_conv_w.reshape(-1)]))
    got = allgather8(pack, "ag_small_in", True).reshape(N_DEV, -1)
    c_all = got[:, :D]
    per_chip = got[0::2]
    n_scw = ssd_conv_w.size
    ssd_cw = jnp.concatenate([per_chip[k, D:D + n_scw].reshape(ssd_conv_w.shape) for k in range(N_CHIP)], axis=2)
    n_fcw = ffn_conv_w.size
    ffn_cw = jnp.concatenate([per_chip[k, D + n_scw:D + n_scw + n_fcw].reshape(ffn_conv_w.shape) for k in range(N_CHIP)], axis=2)

    ada_b_cols = lax.dynamic_slice_in_dim(ada_b, chip * n_ada, n_ada, axis=1)
    mod_cols = ada_mod(c_all, ada_w, ada_b_cols, "ada_mod")
    mod_all = allgather8(mod_cols.reshape(-1, 128), "ag_mod", True)[0::2].reshape(N_CHIP, DEPTH, N_DEV, n_ada)
    mods = lax.dynamic_index_in_dim(mod_all, dev, 2, keepdims=False).transpose(1, 0, 2).reshape(DEPTH, 6, D)

    full = dict(zip([n for n, _ in BIG], _gather_big([wt[n] for n, _ in BIG], mc)))
    full["w_in"] = pack_w_in(full["w_in"])
    full["mla_w_uq"] = pack_w_uq(full["mla_w_uq"])
    ws = []
    for l in range(DEPTH):
        w = {n: full[n][l] for n, _ in BIG}
        w.update(ssd_conv_w=ssd_cw[l], ffn_conv_w=ffn_cw[l])
        for n in ("norm1_g", "norm2_g", "ssd_conv_b", "ssd_dt_bias", "ssd_a_log", "ssd_d", "ssd_norm_g", "mla_q_norm_g",
                  "mla_kv_norm_g", "swa_sinks", "ffn_conv_b"):
            w[n] = wt[n][l]
        ws.append(w)

    cos, sin = _rope_tables(positions)
    t = x.shape[1]
    loss8, dx, dfg, dmods, lg = local_step(x.reshape(t, D), loss_target.reshape(t, D), mods, ws, final_norm_g, cos, sin)
    loss = lax.psum(loss8[0, 0], ("x", "y", "c"))

    rows = []
    for l in range(DEPTH):
        for name, n in SMALL:
            rows.append(_lanes(dmods[l] if name == "dmod" else lg[l][name], n))
    rows.append(dfg)
    small = allgather8(_tile_rows(jnp.concatenate(rows)), "ag_small_grads", True)
    total = sum8(small, "sum_small_grads").reshape(-1)
    grads = {}
    per_layer = {name: [] for name, _ in SMALL}
    for l in range(DEPTH):
        off = l * SMALL_LAYER
        for name, n in SMALL:
            per_layer[name].append(total[off:off + n])
            off += n
    grads["ada_b"] = jnp.stack(per_layer["dmod"])
    for name, shape in SMALL_SHAPES.items():
        grads[name] = jnp.stack([v[:math.prod(shape)].reshape(shape) for v in per_layer[name]])
    grads["final_norm_g"] = total[DEPTH * SMALL_LAYER:DEPTH * SMALL_LAYER + D]
    for name in ("ssd_conv_w", "ffn_conv_w"):
        n = grads[name].shape[2] // N_CHIP
        grads[name] = lax.dynamic_slice_in_dim(grads[name], chip * n, n, axis=2)
    dmod_all = small.reshape(N_DEV, -1)[:, :DEPTH * SMALL_LAYER].reshape(N_DEV, DEPTH, SMALL_LAYER)[:, :, :6 * D]
    dmod_cols = lax.dynamic_slice_in_dim(dmod_all, chip * n_ada, n_ada, axis=2).transpose(1, 0, 2)
    grads["ada_w"] = ada_grad(c_all, dmod_cols, "ada_grad")

    stacked = []
    for name, _ in BIG:
        g = jnp.stack([lg[l][name] for l in range(DEPTH)])
        if name == "w_in":
            g = unpack_w_in(g)
        if name == "mla_w_uq":
            g = unpack_w_uq(g)
        stacked.append(g)
    for (name, _), g in zip(BIG, _reduce_big(stacked, mx, my, mc)):
        grads[name] = g

    deltas, new_m, new_v = {}, {}, {}
    for n in WEIGHTS:
        deltas[n], new_m[n], new_v[n] = adamw(wt[n], grads[n], args["m_" + n], args["v_" + n], "adamw_" + n)
    return (loss, dx.reshape(x.shape), *[grads[n] for n in WEIGHTS], *[deltas[n] for n in WEIGHTS],
            *[new_m[n] for n in WEIGHTS], *[new_v[n] for n in WEIGHTS])
```

```python
import functools
import math

import jax
import jax.numpy as jnp
from jax import lax
from jax.experimental import pallas as pl
from jax.experimental.pallas import tpu as pltpu

F32 = jnp.float32
BF16 = jnp.bfloat16
MXU_DTYPE = BF16

D = 1024
DEPTH = 4
EPS = 1e-6
N_DEV = 8
N_CHIP = 4

SSD_HEADS = 8
SSD_INNER = 512
SSD_STATE = 128
SSD_XBC = 1024
Q = 128
MLA_HEADS = 4
MLA_QK = 96
D_FF = 2816
D_IN = 2472

P_XBC, P_Z, P_CQ, P_SQ, P_CKV, P_DT, P_KR, P_SK, P_SV = 0, 1024, 1536, 1792, 2048, 2176, 2304, 2432, 2560
NP = 2688
_PACK = ((P_Z, 0, 512), (P_XBC, 512, 1024), (P_DT, 1536, 8), (P_CQ, 1544, 256), (P_CKV, 1800, 128),
         (P_KR, 1928, 32), (P_SQ, 1960, 256), (P_SK, 2216, 128), (P_SV, 2344, 128))

ADAM_LR, ADAM_B1, ADAM_B2, ADAM_EPS, ADAM_WD, ADAM_STEP = 0.001, 0.9, 0.999, 1e-08, 0.01, 10

VMEM_LIMIT = 56 * 1024 * 1024
NEG = -1e30


def _cp(sem=None):
    return pltpu.CompilerParams(dimension_semantics=sem, vmem_limit_bytes=VMEM_LIMIT)


def _dot(a, b, dims):
    return lax.dot_general(a.astype(MXU_DTYPE), b.astype(MXU_DTYPE), (dims, ((), ())), preferred_element_type=F32)


_NN = ((1,), (0,))
_NT = ((1,), (1,))
_TN = ((0,), (0,))


@jax.custom_vjp
def mm(a, b):
    return _dot(a, b, _NN)


mm.defvjp(lambda a, b: (_dot(a, b, _NN), (a, b)),
          lambda r, g: (_dot(g, r[1], _NT), _dot(r[0], g, _TN)))


@jax.custom_vjp
def mm_nt(a, b):
    return _dot(a, b, _NT)


mm_nt.defvjp(lambda a, b: (_dot(a, b, _NT), (a, b)),
             lambda r, g: (_dot(g, r[1], _NN), _dot(g, r[0], _TN)))


@jax.custom_vjp
def mm_tn(a, b):
    return _dot(a, b, _TN)


mm_tn.defvjp(lambda a, b: (_dot(a, b, _TN), (a, b)),
             lambda r, g: (_dot(r[1], g, _NT), _dot(r[0], g, _NN)))


def _silu(x):
    return x * jax.nn.sigmoid(x)


def _rms(x, g):
    return x * lax.rsqrt(jnp.mean(x * x, axis=-1, keepdims=True) + EPS) * g


def _modnorm(x, g, sh, sc):
    return _rms(x, g) * (1.0 + sc) + sh


def _blk(dim, target, mult=128):
    best = None
    for b in range(mult, min(dim, target) + 1, mult):
        if dim % b == 0:
            best = b
    return best if best is not None else dim


def matmul(a, b, mode, out_dtype, name):
    if mode == "nn":
        (m, k), n = a.shape, b.shape[1]
    elif mode == "nt":
        (m, k), n = a.shape, b.shape[0]
    else:
        (k, m), n = a.shape, b.shape[1]
    bm, bn, bk = _blk(m, 512), _blk(n, 1408), _blk(k, 2816)
    nk = k // bk
    dims = {"nn": _NN, "nt": _NT, "tn": _TN}[mode]

    def body(a_ref, b_ref, o_ref, acc_ref):
        kk = pl.program_id(2)
        part = _dot(a_ref[...], b_ref[...], dims)
        if nk == 1:
            o_ref[...] = part.astype(o_ref.dtype)
            return

        @pl.when(kk == 0)
        def _():
            acc_ref[...] = part

        @pl.when((kk > 0) & (kk < nk - 1))
        def _():
            acc_ref[...] += part

        @pl.when(kk == nk - 1)
        def _():
            o_ref[...] = (acc_ref[...] + part).astype(o_ref.dtype)

    a_spec = pl.BlockSpec((bk, bm), lambda i, j, kk: (kk, i)) if mode == "tn" else pl.BlockSpec((bm, bk), lambda i, j, kk: (i, kk))
    b_spec = pl.BlockSpec((bn, bk), lambda i, j, kk: (j, kk)) if mode == "nt" else pl.BlockSpec((bk, bn), lambda i, j, kk: (kk, j))
    return pl.pallas_call(
        body, name=name, out_shape=jax.ShapeDtypeStruct((m, n), out_dtype), grid=(m // bm, n // bn, nk),
        in_specs=[a_spec, b_spec], out_specs=pl.BlockSpec((bm, bn), lambda i, j, kk: (i, j)),
        scratch_shapes=[pltpu.VMEM((bm, bn), F32)], compiler_params=_cp(("parallel", "parallel", "arbitrary")),
    )(a, b)


TM = 512


def _row(v):
    return v.reshape(1, -1)


def modnorm_fwd(x, g, sh, sc, name):
    t = x.shape[0]

    def body(x_ref, g_ref, sh_ref, sc_ref, o_ref):
        o_ref[...] = _modnorm(x_ref[...], g_ref[...], sh_ref[...], sc_ref[...]).astype(o_ref.dtype)

    vec = pl.BlockSpec((1, D), lambda i: (0, 0))
    return pl.pallas_call(
        body, name=name, out_shape=jax.ShapeDtypeStruct((t, D), BF16), grid=(t // TM,),
        in_specs=[pl.BlockSpec((TM, D), lambda i: (i, 0)), vec, vec, vec],
        out_specs=pl.BlockSpec((TM, D), lambda i: (i, 0)), compiler_params=_cp(("parallel",)),
    )(x, _row(g), _row(sh), _row(sc))


def modnorm_bwd(x, g, sh, sc, dh, dres, name):
    t = x.shape[0]

    def body(x_ref, g_ref, sh_ref, sc_ref, dh_ref, dres_ref, dx_ref, sums_ref):
        _, vjp = jax.vjp(_modnorm, x_ref[...], g_ref[...], sh_ref[...], sc_ref[...])
        dx, dg, dsh, dsc = vjp(dh_ref[...].astype(F32))
        dx_ref[...] = dx + dres_ref[...]

        @pl.when(pl.program_id(0) == 0)
        def _():
            sums_ref[...] = jnp.zeros_like(sums_ref)

        sums_ref[0:1, :] += dg
        sums_ref[1:2, :] += dsh
        sums_ref[2:3, :] += dsc

    vec = pl.BlockSpec((1, D), lambda i: (0, 0))
    tile = pl.BlockSpec((TM, D), lambda i: (i, 0))
    return pl.pallas_call(
        body, name=name, out_shape=(jax.ShapeDtypeStruct((t, D), F32), jax.ShapeDtypeStruct((8, D), F32)), grid=(t // TM,),
        in_specs=[tile, vec, vec, vec, tile, tile], out_specs=(tile, pl.BlockSpec((8, D), lambda i: (0, 0))),
        compiler_params=_cp(("arbitrary",)),
    )(x, _row(g), _row(sh), _row(sc), dh, dres)


def resid_fwd(x, y, gate, name):
    t = x.shape[0]

    def body(x_ref, y_ref, g_ref, o_ref):
        o_ref[...] = x_ref[...] + g_ref[...] * y_ref[...]

    tile = pl.BlockSpec((TM, D), lambda i: (i, 0))
    return pl.pallas_call(
        body, name=name, out_shape=jax.ShapeDtypeStruct((t, D), F32), grid=(t // TM,),
        in_specs=[tile, tile, pl.BlockSpec((1, D), lambda i: (0, 0))], out_specs=tile, compiler_params=_cp(("parallel",)),
    )(x, y, _row(gate))


def resid_bwd(dxo, y, gate, name):
    t = dxo.shape[0]

    def body(d_ref, y_ref, g_ref, dy_ref, dg_ref):
        d = d_ref[...]
        dy_ref[...] = (d * g_ref[...]).astype(BF16)

        @pl.when(pl.program_id(0) == 0)
        def _():
            dg_ref[...] = jnp.zeros_like(dg_ref)

        dg_ref[0:1, :] += jnp.sum(d * y_ref[...], axis=0, keepdims=True)

    tile = pl.BlockSpec((TM, D), lambda i: (i, 0))
    return pl.pallas_call(
        body, name=name, out_shape=(jax.ShapeDtypeStruct((t, D), BF16), jax.ShapeDtypeStruct((8, D), F32)), grid=(t // TM,),
        in_specs=[tile, tile, pl.BlockSpec((1, D), lambda i: (0, 0))], out_specs=(tile, pl.BlockSpec((8, D), lambda i: (0, 0))),
        compiler_params=_cp(("arbitrary",)),
    )(dxo, y, _row(gate))


CW = 256
NCW = D_FF // CW


def _conv3(u, halo, w, b):
    n = u.shape[0]
    win = jnp.concatenate([halo, u], axis=0)
    return b + w[0:1] * win[6:6 + n] + w[1:2] * win[7:7 + n] + w[2:3] * win[8:8 + n]


def _convglu(ua, ub, ha, hb, wa, wb, ba, bb):
    return _silu(_conv3(ua, ha, wa, ba)) * _conv3(ub, hb, wb, bb)


def _halo_specs(tm, cw, off):
    r = tm // 8
    return pl.BlockSpec((8, cw), lambda j, i, o=off: (jnp.maximum(i * r - 1, 0), j + o))


def convglu_fwd(u0, cw, cb, name):
    t = u0.shape[0]

    def body(ua_ref, ub_ref, ha_ref, hb_ref, wa_ref, wb_ref, ba_ref, bb_ref, o_ref):
        keep = (pl.program_id(1) > 0).astype(F32)
        o_ref[...] = _convglu(ua_ref[...], ub_ref[...], ha_ref[...] * keep, hb_ref[...] * keep,
                              wa_ref[...], wb_ref[...], ba_ref[...], bb_ref[...]).astype(o_ref.dtype)

    def col(rows, off):
        return pl.BlockSpec((rows, CW), lambda j, i, o=off: (0, j + o))

    return pl.pallas_call(
        body, name=name, out_shape=jax.ShapeDtypeStruct((t, D_FF), BF16), grid=(NCW, t // TM),
        in_specs=[pl.BlockSpec((TM, CW), lambda j, i: (i, j)), pl.BlockSpec((TM, CW), lambda j, i: (i, j + NCW)),
                  _halo_specs(TM, CW, 0), _halo_specs(TM, CW, NCW), col(3, 0), col(3, NCW), col(1, 0), col(1, NCW)],
        out_specs=pl.BlockSpec((TM, CW), lambda j, i: (i, j)), compiler_params=_cp(("parallel", "parallel")),
    )(u0, u0, u0, u0, cw, cw, _row(cb), _row(cb))


def convglu_bwd(u0, cw, cb, dgact, name):
    t = u0.shape[0]
    nt = t // TM

    def body(ua_ref, ub_ref, ha_ref, hb_ref, wa_ref, wb_ref, ba_ref, bb_ref, dg_ref,
             dua_ref, dub_ref, dwa_ref, dwb_ref, dba_ref, dbb_ref, ca_ref, cb_ref):
        step = pl.program_id(1)
        keep = (step < nt - 1).astype(F32)

        @pl.when(step == 0)
        def _():
            ca_ref[...] = jnp.zeros_like(ca_ref)
            cb_ref[...] = jnp.zeros_like(cb_ref)
            dwa_ref[...] = jnp.zeros_like(dwa_ref)
            dwb_ref[...] = jnp.zeros_like(dwb_ref)
            dba_ref[...] = jnp.zeros_like(dba_ref)
            dbb_ref[...] = jnp.zeros_like(dbb_ref)

        _, vjp = jax.vjp(_convglu, ua_ref[...], ub_ref[...], ha_ref[...] * keep, hb_ref[...] * keep,
                         wa_ref[...], wb_ref[...], ba_ref[...], bb_ref[...])
        dua, dub, dha, dhb, dwa, dwb, dba, dbb = vjp(dg_ref[...].astype(F32))
        zeros = jnp.zeros((TM - 8, CW), F32)
        dua_ref[...] = (dua + jnp.concatenate([zeros, ca_ref[...]], axis=0)).astype(BF16)
        dub_ref[...] = (dub + jnp.concatenate([zeros, cb_ref[...]], axis=0)).astype(BF16)
        ca_ref[...] = dha * keep
        cb_ref[...] = dhb * keep
        dwa_ref[...] += dwa
        dwb_ref[...] += dwb
        dba_ref[...] += dba
        dbb_ref[...] += dbb

    def rev(i):
        return nt - 1 - i

    def tile(off):
        return pl.BlockSpec((TM, CW), lambda j, i, o=off: (rev(i), j + o))

    def halo(off):
        r = TM // 8
        return pl.BlockSpec((8, CW), lambda j, i, o=off: (jnp.maximum(rev(i) * r - 1, 0), j + o))

    def col(rows, off):
        return pl.BlockSpec((rows, CW), lambda j, i, o=off: (0, j + o))

    outs = pl.pallas_call(
        body, name=name,
        out_shape=(jax.ShapeDtypeStruct((t, D_FF), BF16), jax.ShapeDtypeStruct((t, D_FF), BF16),
                   jax.ShapeDtypeStruct((3, D_FF), F32), jax.ShapeDtypeStruct((3, D_FF), F32),
                   jax.ShapeDtypeStruct((1, D_FF), F32), jax.ShapeDtypeStruct((1, D_FF), F32)),
        grid=(NCW, nt),
        in_specs=[tile(0), tile(NCW), halo(0), halo(NCW), col(3, 0), col(3, NCW), col(1, 0), col(1, NCW), tile(0)],
        out_specs=(tile(0), tile(0), col(3, 0), col(3, 0), col(1, 0), col(1, 0)),
        scratch_shapes=[pltpu.VMEM((8, CW), F32), pltpu.VMEM((8, CW), F32)],
        compiler_params=_cp(("parallel", "arbitrary")),
    )(u0, u0, u0, u0, cw, cw, _row(cb), _row(cb), dgact)
    dua, dub, dwa, dwb, dba, dbb = outs
    return (jnp.concatenate([dua, dub], axis=1), jnp.concatenate([dwa, dwb], axis=1), jnp.concatenate([dba, dbb], axis=1))


def _pick(v, h, axis):
    return v[:, h:h + 1] if axis == 1 else v[h:h + 1, :]


def _ssd_chunk(z, xh, xc, dtp, hin, cw, cb, dtb, alog, dsk, ng):
    lane_hi = lax.broadcasted_iota(jnp.int32, (Q, Q), 1) >= 64
    row_hi = lax.broadcasted_iota(jnp.int32, (Q, Q), 0) >= 64
    causal = lax.broadcasted_iota(jnp.int32, (Q, Q), 0) >= lax.broadcasted_iota(jnp.int32, (Q, Q), 1)
    win = jnp.concatenate([xh, xc], axis=0)
    xbc = cb
    for k in range(4):
        xbc = xbc + cw[k:k + 1] * win[5 + k:5 + k + Q]
    xbc = _silu(xbc)
    xs, bm, cm = xbc[:, 0:512], xbc[:, 512:768], xbc[:, 768:1024]
    dt = jax.nn.softplus(dtp + dtb)
    da = dt * (-jnp.exp(alog))
    ah = jnp.dot(causal.astype(F32), da, precision=lax.Precision.HIGHEST, preferred_element_type=F32)
    aht = ah.T
    alast = ah[Q - 1:Q, :]
    eah = jnp.exp(ah)
    dte = jnp.exp(alast - ah)
    elast = jnp.exp(alast)
    ys, houts = [], []
    for g in range(2):
        bg, cg = bm[:, 128 * g:128 * g + 128], cm[:, 128 * g:128 * g + 128]
        cbm = mm_nt(cg, bg)
        for jp in range(2):
            j = 2 * g + jp
            h0, h1 = 2 * j, 2 * j + 1
            xp = xs[:, 128 * j:128 * j + 128]
            xdt = xp * jnp.where(lane_hi, _pick(dt, h1, 1), _pick(dt, h0, 1))
            yd, st = [], []
            for h in (h0, h1):
                seg = _pick(ah, h, 1) - _pick(aht, h, 0)
                decay = jnp.exp(jnp.where(causal, seg, NEG))
                yd.append(mm(cbm * decay, xdt))
                st.append(mm_tn(xdt * _pick(dte, h, 1), bg))
            hj = hin[j]
            hout = hj * jnp.where(row_hi, _pick(elast, h1, 1), _pick(elast, h0, 1)) + jnp.where(row_hi, st[1], st[0])
            yoff = mm_nt(cg, hj) * jnp.where(lane_hi, _pick(eah, h1, 1), _pick(eah, h0, 1))
            skip = xp * jnp.where(lane_hi[0:1], _pick(dsk, h1, 1), _pick(dsk, h0, 1))
            ys.append(jnp.where(lane_hi, yd[1], yd[0]) + yoff + skip)
            houts.append(hout)
    y = jnp.concatenate(ys, axis=1) * _silu(z)
    yn = []
    for g in range(2):
        yg = y[:, 256 * g:256 * g + 256]
        yn.append(yg * lax.rsqrt(jnp.mean(yg * yg, axis=-1, keepdims=True) + EPS))
    return jnp.concatenate(yn, axis=1) * ng, jnp.stack(houts)


def _pad_lanes(v, n=128):
    v = v.reshape(1, -1)
    return jnp.pad(v, ((0, 0), (0, n - v.shape[1])))


def _ssd_in_specs(chunk_of):
    return [pl.BlockSpec((Q, 512), lambda i: (chunk_of(i), P_Z // 512)),
            pl.BlockSpec((8, 1024), lambda i: (jnp.maximum(chunk_of(i) * (Q // 8) - 1, 0), P_XBC // 1024)),
            pl.BlockSpec((Q, 1024), lambda i: (chunk_of(i), P_XBC // 1024)),
            pl.BlockSpec((Q, 128), lambda i: (chunk_of(i), P_DT // 128))]


def _full(shape):
    nd = len(shape)
    return pl.BlockSpec(shape, lambda i: (0,) * nd)


def ssd_fwd(proj, cw, cb, dtb, alog, dsk, ng, name):
    t = proj.shape[0]
    nc = t // Q

    def body(z_ref, xh_ref, xc_ref, dt_ref, cw_ref, cb_ref, dtb_ref, al_ref, dsk_ref, ng_ref, y_ref, hs_ref, h_ref):
        i = pl.program_id(0)

        @pl.when(i == 0)
        def _():
            h_ref[...] = jnp.zeros_like(h_ref)

        hin = h_ref[...]
        hs_ref[0] = hin
        y, hout = _ssd_chunk(z_ref[...], xh_ref[...] * (i > 0).astype(F32), xc_ref[...], dt_ref[...], hin,
                             cw_ref[...], cb_ref[...], dtb_ref[...], al_ref[...], dsk_ref[...], ng_ref[...])
        y_ref[...] = y
        h_ref[...] = hout

    return pl.pallas_call(
        body, name=name,
        out_shape=(jax.ShapeDtypeStruct((t, 512), F32), jax.ShapeDtypeStruct((nc, 4, 128, 128), F32)), grid=(nc,),
        in_specs=_ssd_in_specs(lambda i: i) + [_full((4, 1024)), _full((1, 1024)), _full((1, 128)), _full((1, 128)),
                                               _full((1, 128)), _full((1, 512))],
        out_specs=(pl.BlockSpec((Q, 512), lambda i: (i, 0)), pl.BlockSpec((1, 4, 128, 128), lambda i: (i, 0, 0, 0))),
        scratch_shapes=[pltpu.VMEM((4, 128, 128), F32)], compiler_params=_cp(("arbitrary",)),
    )(proj, proj, proj, proj, cw, _row(cb), _pad_lanes(dtb), _pad_lanes(alog), _pad_lanes(dsk), _row(ng))


def ssd_bwd(proj, hs, dy, cw, cb, dtb, alog, dsk, ng, name):
    t = proj.shape[0]
    nc = t // Q

    def body(z_ref, xh_ref, xc_ref, dt_ref, hs_ref, dy_ref, cw_ref, cb_ref, dtb_ref, al_ref, dsk_ref, ng_ref,
             dz_ref, dx_ref, ddt_ref, dcw_ref, vec_ref, dh_ref, carry_ref):
        step = pl.program_id(0)
        keep = (step < nc - 1).astype(F32)

        @pl.when(step == 0)
        def _():
            dh_ref[...] = jnp.zeros_like(dh_ref)
            carry_ref[...] = jnp.zeros_like(carry_ref)
            dcw_ref[...] = jnp.zeros_like(dcw_ref)
            vec_ref[...] = jnp.zeros_like(vec_ref)

        _, vjp = jax.vjp(_ssd_chunk, z_ref[...], xh_ref[...] * keep, xc_ref[...], dt_ref[...], hs_ref[0],
                         cw_ref[...], cb_ref[...], dtb_ref[...], al_ref[...], dsk_ref[...], ng_ref[...])
        dz, dxh, dxc, ddt, dhin, dcw, dcb, ddtb, dal, ddsk, dng = vjp((dy_ref[...], dh_ref[...]))
        dz_ref[...] = dz
        dx_ref[...] = dxc + jnp.concatenate([jnp.zeros((Q - 8, 1024), F32), carry_ref[...]], axis=0)
        ddt_ref[...] = ddt
        carry_ref[...] = dxh * keep
        dh_ref[...] = dhin
        dcw_ref[...] += dcw
        vec_ref[0:1, :] += dcb
        vec_ref[1:2, 0:128] += ddtb
        vec_ref[2:3, 0:128] += dal
        vec_ref[3:4, 0:128] += ddsk
        vec_ref[4:5, 0:512] += dng

    def rev(i):
        return nc - 1 - i

    return pl.pallas_call(
        body, name=name,
        out_shape=(jax.ShapeDtypeStruct((t, 512), F32), jax.ShapeDtypeStruct((t, 1024), F32), jax.ShapeDtypeStruct((t, 128), F32),
                   jax.ShapeDtypeStruct((4, 1024), F32), jax.ShapeDtypeStruct((8, 1024), F32)),
        grid=(nc,),
        in_specs=_ssd_in_specs(rev) + [pl.BlockSpec((1, 4, 128, 128), lambda i: (rev(i), 0, 0, 0)),
                                       pl.BlockSpec((Q, 512), lambda i: (rev(i), 0)),
                                       _full((4, 1024)), _full((1, 1024)), _full((1, 128)), _full((1, 128)), _full((1, 128)),
                                       _full((1, 512))],
        out_specs=(pl.BlockSpec((Q, 512), lambda i: (rev(i), 0)), pl.BlockSpec((Q, 1024), lambda i: (rev(i), 0)),
                   pl.BlockSpec((Q, 128), lambda i: (rev(i), 0)), _full((4, 1024)), _full((8, 1024))),
        scratch_shapes=[pltpu.VMEM((4, 128, 128), F32), pltpu.VMEM((8, 1024), F32)], compiler_params=_cp(("arbitrary",)),
    )(proj, proj, proj, proj, hs, dy, cw, _row(cb), _pad_lanes(dtb), _pad_lanes(alog), _pad_lanes(dsk), _row(ng))


TA = 256
MLA_SCALE = 1.0 / math.sqrt(MLA_QK)


def _rope(x1, x2, cos, sin):
    return x1 * cos - x2 * sin, x1 * sin + x2 * cos


def _mla_pre(cq, ckv, kr, cos, sin, qg, kvg, wuq, wukv):
    n = cq.shape[0]
    qh = mm(_rms(cq, qg), wuq)
    kv = mm(_rms(ckv, kvg), wukv)
    kr1, kr2 = _rope(kr[:, 0:16], kr[:, 16:32], cos, sin)
    pad = jnp.zeros((n, 32), F32)
    qs, ks, vs = [], [], []
    for h in range(MLA_HEADS):
        b = qh[:, 128 * h:128 * h + 128]
        q1, q2 = _rope(b[:, 64:80], b[:, 80:96], cos, sin)
        qs.append(jnp.concatenate([b[:, 0:64], q1, q2, pad], axis=1))
        ks.append(jnp.concatenate([kv[:, 128 * h:128 * h + 64], kr1, kr2, pad], axis=1))
        vs.append(kv[:, 128 * h + 64:128 * h + 128])
    return jnp.stack(qs), jnp.stack(ks), jnp.stack(vs)


def _mla_pre_specs():
    return [pl.BlockSpec((TM, 256), lambda i: (i, P_CQ // 256)), pl.BlockSpec((TM, 128), lambda i: (i, P_CKV // 128)),
            pl.BlockSpec((TM, 128), lambda i: (i, P_KR // 128)), pl.BlockSpec((TM, 16), lambda i: (i, 0)),
            pl.BlockSpec((TM, 16), lambda i: (i, 0)), _full((1, 256)), _full((1, 128)), _full((256, 512)), _full((128, 512))]


def _head_tile(w):
    return pl.BlockSpec((MLA_HEADS, TM, w), lambda i: (0, i, 0))


def mla_pre_fwd(proj, cos, sin, qg, kvg, wuq, wukv, name):
    t = proj.shape[0]

    def body(cq_ref, ckv_ref, kr_ref, cos_ref, sin_ref, qg_ref, kvg_ref, wuq_ref, wukv_ref, q_ref, k_ref, v_ref):
        q, k, v = _mla_pre(cq_ref[...], ckv_ref[...], kr_ref[...], cos_ref[...], sin_ref[...], qg_ref[...], kvg_ref[...],
                           wuq_ref[...], wukv_ref[...])
        q_ref[...] = q.astype(BF16)
        k_ref[...] = k.astype(BF16)
        v_ref[...] = v.astype(BF16)

    return pl.pallas_call(
        body, name=name,
        out_shape=(jax.ShapeDtypeStruct((MLA_HEADS, t, 128), BF16), jax.ShapeDtypeStruct((MLA_HEADS, t, 128), BF16),
                   jax.ShapeDtypeStruct((MLA_HEADS, t, 64), BF16)),
        grid=(t // TM,), in_specs=_mla_pre_specs(), out_specs=(_head_tile(128), _head_tile(128), _head_tile(64)),
        compiler_params=_cp(("parallel",)),
    )(proj, proj, proj, cos, sin, _row(qg), _row(kvg), wuq, wukv)


def mla_pre_bwd(proj, cos, sin, qg, kvg, wuq, wukv, dq, dk, dv, name):
    t = proj.shape[0]

    def body(cq_ref, ckv_ref, kr_ref, cos_ref, sin_ref, qg_ref, kvg_ref, wuq_ref, wukv_ref, dq_ref, dk_ref, dv_ref,
             dcq_ref, dckv_ref, dkr_ref, dwuq_ref, dwukv_ref, vec_ref):
        @pl.when(pl.program_id(0) == 0)
        def _():
            dwuq_ref[...] = jnp.zeros_like(dwuq_ref)
            dwukv_ref[...] = jnp.zeros_like(dwukv_ref)
            vec_ref[...] = jnp.zeros_like(vec_ref)

        cos, sin = cos_ref[...], sin_ref[...]
        f = lambda cq, ckv, kr, qg, kvg, wuq, wukv: _mla_pre(cq, ckv, kr, cos, sin, qg, kvg, wuq, wukv)
        _, vjp = jax.vjp(f, cq_ref[...], ckv_ref[...], kr_ref[...], qg_ref[...], kvg_ref[...], wuq_ref[...], wukv_ref[...])
        dcq, dckv, dkr, dqg, dkvg, dwuq, dwukv = vjp((dq_ref[...], dk_ref[...], dv_ref[...]))
        dcq_ref[...] = dcq
        dckv_ref[...] = dckv
        dkr_ref[...] = dkr
        dwuq_ref[...] += dwuq
        dwukv_ref[...] += dwukv
        vec_ref[0:1, :] += dqg
        vec_ref[1:2, 0:128] += dkvg

    return pl.pallas_call(
        body, name=name,
        out_shape=(jax.ShapeDtypeStruct((t, 256), F32), jax.ShapeDtypeStruct((t, 128), F32), jax.ShapeDtypeStruct((t, 128), F32),
                   jax.ShapeDtypeStruct((256, 512), F32), jax.ShapeDtypeStruct((128, 512), F32), jax.ShapeDtypeStruct((8, 256), F32)),
        grid=(t // TM,), in_specs=_mla_pre_specs() + [_head_tile(128), _head_tile(128), _head_tile(64)],
        out_specs=(pl.BlockSpec((TM, 256), lambda i: (i, 0)), pl.BlockSpec((TM, 128), lambda i: (i, 0)),
                   pl.BlockSpec((TM, 128), lambda i: (i, 0)), _full((256, 512)), _full((128, 512)), _full((8, 256))),
        compiler_params=_cp(("arbitrary",)),
    )(proj, proj, proj, cos, sin, _row(qg), _row(kvg), wuq, wukv, dq, dk, dv)


def _causal_mask(i, j):
    qpos = i * TA + lax.broadcasted_iota(jnp.int32, (TA, TA), 0)
    kpos = j * TA + lax.broadcasted_iota(jnp.int32, (TA, TA), 1)
    return kpos <= qpos


def mla_flash_fwd(q, k, v, name):
    h, t, _ = q.shape

    def body(q_ref, k_ref, v_ref, o_ref, lse_ref, m_ref, l_ref, acc_ref):
        i = pl.program_id(1)
        m_ref[...] = jnp.full_like(m_ref, NEG)
        l_ref[...] = jnp.zeros_like(l_ref)
        acc_ref[...] = jnp.zeros_like(acc_ref)
        qb = q_ref[0]

        def step(j, carry, diagonal=False):
            rows = pl.ds(pl.multiple_of(j * TA, TA), TA)
            s = _dot(qb, k_ref[0, rows, :], _NT) * MLA_SCALE
            if diagonal:
                s = jnp.where(_causal_mask(i, j), s, NEG)
            m_new = jnp.maximum(m_ref[...], jnp.max(s, axis=-1, keepdims=True))
            p = jnp.exp(s - m_new)
            alpha = jnp.exp(m_ref[...] - m_new)
            l_ref[...] = alpha * l_ref[...] + jnp.sum(p, axis=-1, keepdims=True)
            acc_ref[...] = alpha * acc_ref[...] + _dot(p, v_ref[0, rows, :], _NN)
            m_ref[...] = m_new
            return carry

        lax.fori_loop(0, i, step, 0)
        step(i, 0, diagonal=True)
        o_ref[0] = acc_ref[...] / l_ref[...]
        lse_ref[0] = m_ref[...] + jnp.log(l_ref[...])

    return pl.pallas_call(
        body, name=name,
        out_shape=(jax.ShapeDtypeStruct((h, t, 64), F32), jax.ShapeDtypeStruct((h, t, 1), F32)), grid=(h, t // TA),
        in_specs=[pl.BlockSpec((1, TA, 128), lambda hh, i: (hh, i, 0)), pl.BlockSpec((1, t, 128), lambda hh, i: (hh, 0, 0)),
                  pl.BlockSpec((1, t, 64), lambda hh, i: (hh, 0, 0))],
        out_specs=(pl.BlockSpec((1, TA, 64), lambda hh, i: (hh, i, 0)), pl.BlockSpec((1, TA, 1), lambda hh, i: (hh, i, 0))),
        scratch_shapes=[pltpu.VMEM((TA, 1), F32), pltpu.VMEM((TA, 1), F32), pltpu.VMEM((TA, 64), F32)],
        compiler_params=_cp(("parallel", "parallel")),
    )(q, k, v)


def mla_flash_bwd(q, k, v, o, lse, do, name):
    h, t, _ = q.shape
    nb = t // TA

    def body(q_ref, k_ref, v_ref, o_ref, lse_ref, do_ref, dq_ref, dk_ref, dv_ref):
        j = pl.program_id(1)

        @pl.when(j == 0)
        def _():
            dq_ref[...] = jnp.zeros_like(dq_ref)

        dk_ref[...] = jnp.zeros_like(dk_ref)
        dv_ref[...] = jnp.zeros_like(dv_ref)
        kb, vb = k_ref[0], v_ref[0]

        def step(i, carry, diagonal=False):
            rows = pl.ds(pl.multiple_of(i * TA, TA), TA)
            qb, dob = q_ref[0, rows, :], do_ref[0, rows, :]
            s = _dot(qb, kb, _NT) * MLA_SCALE
            p = jnp.exp(s - lse_ref[0, rows, :])
            if diagonal:
                p = jnp.where(_causal_mask(i, j), p, 0.0)
            delta = jnp.sum(dob * o_ref[0, rows, :], axis=-1, keepdims=True)
            dv_ref[0] += _dot(p, dob, _TN)
            ds = p * (_dot(dob, vb, _NT) - delta) * MLA_SCALE
            dk_ref[0] += _dot(ds, qb, _TN)
            dq_ref[0, rows, :] += _dot(ds, kb, _NN)
            return carry

        step(j, 0, diagonal=True)
        lax.fori_loop(j + 1, nb, step, 0)

    def whole(w):
        return pl.BlockSpec((1, t, w), lambda hh, j: (hh, 0, 0))

    def blk(w):
        return pl.BlockSpec((1, TA, w), lambda hh, j: (hh, j, 0))

    return pl.pallas_call(
        body, name=name,
        out_shape=(jax.ShapeDtypeStruct((h, t, 128), F32), jax.ShapeDtypeStruct((h, t, 128), F32), jax.ShapeDtypeStruct((h, t, 64), F32)),
        grid=(h, nb), in_specs=[whole(128), blk(128), blk(64), whole(64), whole(1), whole(64)],
        out_specs=(whole(128), blk(128), blk(64)), compiler_params=_cp(("parallel", "arbitrary")),
    )(q, k, v, o, lse, do)


SWA_SCALE = 1.0 / 8.0


def _swa_block(q, kp, kc, vp, vc, sinks, has_prev):
    k2 = jnp.concatenate([kp, kc], axis=0)
    v2 = jnp.concatenate([vp, vc], axis=0)
    rel = Q + lax.broadcasted_iota(jnp.int32, (Q, 2 * Q), 0) - lax.broadcasted_iota(jnp.int32, (Q, 2 * Q), 1)
    valid = (rel >= 0) & (rel < Q) & ((lax.broadcasted_iota(jnp.int32, (Q, 2 * Q), 1) >= Q) | has_prev)
    outs = []
    for h in range(4):
        g = h // 2
        s = mm_nt(q[:, 64 * h:64 * h + 64], k2[:, 64 * g:64 * g + 64]) * SWA_SCALE
        s = jnp.where(valid, s, NEG)
        sink = sinks[:, h:h + 1]
        m = jnp.maximum(jnp.max(s, axis=-1, keepdims=True), sink)
        e = jnp.exp(s - m)
        p = e / (jnp.sum(e, axis=-1, keepdims=True) + jnp.exp(sink - m))
        outs.append(mm(p, v2[:, 64 * g:64 * g + 64]))
    return jnp.concatenate(outs, axis=1)


def _swa_specs(blk_of):
    def prev(i):
        return jnp.maximum(blk_of(i) - 1, 0)

    return [pl.BlockSpec((Q, 256), lambda i: (blk_of(i), P_SQ // 256)),
            pl.BlockSpec((Q, 128), lambda i: (prev(i), P_SK // 128)), pl.BlockSpec((Q, 128), lambda i: (blk_of(i), P_SK // 128)),
            pl.BlockSpec((Q, 128), lambda i: (prev(i), P_SV // 128)), pl.BlockSpec((Q, 128), lambda i: (blk_of(i), P_SV // 128)),
            _full((1, 128))]


def swa_fwd(proj, sinks, name):
    t = proj.shape[0]

    def body(q_ref, kp_ref, kc_ref, vp_ref, vc_ref, s_ref, o_ref):
        o_ref[...] = _swa_block(q_ref[...], kp_ref[...], kc_ref[...], vp_ref[...], vc_ref[...], s_ref[...], pl.program_id(0) > 0)

    return pl.pallas_call(
        body, name=name, out_shape=jax.ShapeDtypeStruct((t, 256), F32), grid=(t // Q,), in_specs=_swa_specs(lambda i: i),
        out_specs=pl.BlockSpec((Q, 256), lambda i: (i, 0)), compiler_params=_cp(("parallel",)),
    )(proj, proj, proj, proj, proj, _pad_lanes(sinks))


def swa_bwd(proj, sinks, do, name):
    t = proj.shape[0]
    nb = t // Q

    def body(q_ref, kp_ref, kc_ref, vp_ref, vc_ref, s_ref, do_ref, dq_ref, dk_ref, dv_ref, ds_ref, ck_ref, cv_ref):
        step = pl.program_id(0)

        @pl.when(step == 0)
        def _():
            ck_ref[...] = jnp.zeros_like(ck_ref)
            cv_ref[...] = jnp.zeros_like(cv_ref)
            ds_ref[...] = jnp.zeros_like(ds_ref)

        has_prev = step < nb - 1
        f = lambda q, kp, kc, vp, vc, s: _swa_block(q, kp, kc, vp, vc, s, has_prev)
        _, vjp = jax.vjp(f, q_ref[...], kp_ref[...], kc_ref[...], vp_ref[...], vc_ref[...], s_ref[...])
        dq, dkp, dkc, dvp, dvc, dsk = vjp(do_ref[...])
        dq_ref[...] = dq
        dk_ref[...] = dkc + ck_ref[...]
        dv_ref[...] = dvc + cv_ref[...]
        ck_ref[...] = dkp
        cv_ref[...] = dvp
        ds_ref[0:1, :] += dsk

    def rev(i):
        return nb - 1 - i

    return pl.pallas_call(
        body, name=name,
        out_shape=(jax.ShapeDtypeStruct((t, 256), F32), jax.ShapeDtypeStruct((t, 128), F32), jax.ShapeDtypeStruct((t, 128), F32),
                   jax.ShapeDtypeStruct((8, 128), F32)),
        grid=(nb,), in_specs=_swa_specs(rev) + [pl.BlockSpec((Q, 256), lambda i: (rev(i), 0))],
        out_specs=(pl.BlockSpec((Q, 256), lambda i: (rev(i), 0)), pl.BlockSpec((Q, 128), lambda i: (rev(i), 0)),
                   pl.BlockSpec((Q, 128), lambda i: (rev(i), 0)), _full((8, 128))),
        scratch_shapes=[pltpu.VMEM((Q, 128), F32), pltpu.VMEM((Q, 128), F32)], compiler_params=_cp(("arbitrary",)),
    )(proj, proj, proj, proj, proj, _pad_lanes(sinks), do)


def _loss_tile(x, g, tgt):
    err = jnp.square(_rms(x, g) - tgt)
    return 0.5 * jnp.sum(jnp.mean(err, axis=-1, keepdims=True), axis=0, keepdims=True)


def loss_fwd_bwd(x, g, tgt, name):
    t = x.shape[0]

    def body(x_ref, g_ref, t_ref, loss_ref, dx_ref, dg_ref):
        @pl.when(pl.program_id(0) == 0)
        def _():
            loss_ref[...] = jnp.zeros_like(loss_ref)
            dg_ref[...] = jnp.zeros_like(dg_ref)

        tgt = t_ref[...]
        val, vjp = jax.vjp(lambda x, g: _loss_tile(x, g, tgt), x_ref[...], g_ref[...])
        dx, dg = vjp(jnp.ones((1, 1), F32))
        dx_ref[...] = dx
        dg_ref[0:1, :] += dg
        loss_ref[...] += val

    tile = pl.BlockSpec((TM, D), lambda i: (i, 0))
    return pl.pallas_call(
        body, name=name,
        out_shape=(jax.ShapeDtypeStruct((8, 128), F32), jax.ShapeDtypeStruct((t, D), F32), jax.ShapeDtypeStruct((8, D), F32)),
        grid=(t // TM,), in_specs=[tile, _full((1, D)), tile], out_specs=(_full((8, 128)), tile, _full((8, D))),
        compiler_params=_cp(("arbitrary",)),
    )(x, _row(g), tgt)


def adamw(w, g, m, v, name):
    shape = w.shape
    cols = shape[-1] if w.ndim > 1 else shape[0]
    w2, g2, m2, v2 = (a.reshape(-1, cols) for a in (w, g, m, v))
    rows = w2.shape[0]
    br = _blk(rows, max(8, (1 << 19) // cols), 8)

    def body(w_ref, g_ref, m_ref, v_ref, d_ref, nm_ref, nv_ref):
        gg = g_ref[...]
        nm = ADAM_B1 * m_ref[...] + (1.0 - ADAM_B1) * gg
        nv = ADAM_B2 * v_ref[...] + (1.0 - ADAM_B2) * jnp.square(gg)
        m_hat = nm / (1.0 - ADAM_B1 ** ADAM_STEP)
        v_hat = nv / (1.0 - ADAM_B2 ** ADAM_STEP)
        d_ref[...] = -ADAM_LR * (m_hat / (jnp.sqrt(v_hat) + ADAM_EPS) + ADAM_WD * w_ref[...])
        nm_ref[...] = nm
        nv_ref[...] = nv

    spec = pl.BlockSpec((br, cols), lambda i: (i, 0))
    out = jax.ShapeDtypeStruct((rows, cols), F32)
    res = pl.pallas_call(body, name=name, out_shape=(out, out, out), grid=(rows // br,), in_specs=[spec] * 4,
                         out_specs=(spec, spec, spec), compiler_params=_cp(("parallel",)))(w2, g2, m2, v2)
    return tuple(r.reshape(shape) for r in res)


MESH = pl.DeviceIdType.MESH
CHIP_FLIPS = ((1, 0), (0, 1), (1, 1))


def _place():
    return lax.axis_index("x"), lax.axis_index("y"), lax.axis_index("c")


def allgather8(blk, name, in_vmem):
    space = pltpu.VMEM if in_vmem else pl.ANY

    def body(x_ref, out_ref, send_sems, recv_sems, local_sem):
        x, y, c = _place()
        me, sibling = (x, y, c), (x, y, 1 - c)
        chips = [(x ^ fx, y ^ fy) for fx, fy in CHIP_FLIPS]

        def slot(px, py, pc):
            return out_ref.at[4 * px + 2 * py + pc]

        def copy(k, block, to, src=None):
            return pltpu.make_async_remote_copy(
                src_ref=slot(*block) if src is None else src, dst_ref=slot(*block),
                send_sem=send_sems.at[k], recv_sem=recv_sems.at[k], device_id=to, device_id_type=MESH)

        mine = pltpu.make_async_copy(x_ref, slot(*me), local_sem)
        mine.start()
        first = [copy(0, me, sibling, src=x_ref)]
        first += [copy(1 + j, me, (*chip, c), src=x_ref) for j, chip in enumerate(chips)]
        for cp in first:
            cp.start()
        passed = [copy(4 + j, (*chip, c), sibling) for j, chip in enumerate(chips)]
        for j, chip in enumerate(chips):
            copy(1 + j, (*chip, c), me).wait_recv()
            passed[j].start()
        copy(0, sibling, me).wait_recv()
        for j, chip in enumerate(chips):
            copy(4 + j, (*chip, 1 - c), me).wait_recv()
        for cp in first + passed:
            cp.wait_send()
        mine.wait()

    return pl.pallas_call(
        body, name=name, out_shape=jax.ShapeDtypeStruct((N_DEV,) + blk.shape, blk.dtype),
        in_specs=[pl.BlockSpec(memory_space=space)], out_specs=pl.BlockSpec(memory_space=space),
        scratch_shapes=[pltpu.SemaphoreType.DMA((7,)), pltpu.SemaphoreType.DMA((7,)), pltpu.SemaphoreType.DMA],
        compiler_params=pltpu.CompilerParams(vmem_limit_bytes=VMEM_LIMIT),
    )(blk)


def flip_exchange(src, plan, n_out, name):
    def body(x_ref, out_ref, send_sems, recv_sems):
        x, y, c = _place()
        copies = []
        for k, (flip, src_index, dst_slot) in enumerate(plan):
            s, d = x_ref.at[src_index(x, y, c)], out_ref.at[dst_slot(x, y, c)]
            if flip is None:
                copies.append(pltpu.make_async_copy(s, d, send_sems.at[k]))
            else:
                copies.append(pltpu.make_async_remote_copy(
                    src_ref=s, dst_ref=d, send_sem=send_sems.at[k], recv_sem=recv_sems.at[k],
                    device_id=(x ^ flip[0], y ^ flip[1], c ^ flip[2]), device_id_type=MESH))
        for cp in copies:
            cp.start()
        for (flip, _, _), cp in zip(plan, copies):
            if flip is None:
                cp.wait()
            else:
                cp.wait_recv()
                cp.wait_send()

    n = len(plan)
    return pl.pallas_call(
        body, name=name, out_shape=jax.ShapeDtypeStruct((n_out,) + src.shape[1:], src.dtype),
        in_specs=[pl.BlockSpec(memory_space=pl.ANY)], out_specs=pl.BlockSpec(memory_space=pl.ANY),
        scratch_shapes=[pltpu.SemaphoreType.DMA((n,)), pltpu.SemaphoreType.DMA((n,))],
    )(src)


ROWS_ADD = 2048


def add_pairs(a, b, out_dtype, name):
    n = a.shape[0]
    br = _blk(n, ROWS_ADD, 16)

    def body(a_ref, b_ref, o_ref):
        o_ref[...] = (a_ref[...].astype(F32) + b_ref[...].astype(F32)).astype(o_ref.dtype)

    spec = pl.BlockSpec((br, 128), lambda i: (i, 0))
    return pl.pallas_call(body, name=name, out_shape=jax.ShapeDtypeStruct((n, 128), out_dtype), grid=(n // br,),
                          in_specs=[spec, spec], out_specs=spec, compiler_params=_cp(("parallel",)))(a, b)


def add_slots(own, others, name):
    n = own.shape[0]
    ns = others.shape[0]
    br = _blk(n, ROWS_ADD, 16)

    def body(a_ref, b_ref, o_ref):
        acc = a_ref[...].astype(F32)
        for s in range(ns):
            acc = acc + b_ref[s].astype(F32)
        o_ref[...] = acc

    return pl.pallas_call(body, name=name, out_shape=jax.ShapeDtypeStruct((n, 128), F32), grid=(n // br,),
                          in_specs=[pl.BlockSpec((br, 128), lambda i: (i, 0)), pl.BlockSpec((ns, br, 128), lambda i: (0, i, 0))],
                          out_specs=pl.BlockSpec((br, 128), lambda i: (i, 0)), compiler_params=_cp(("parallel",)))(own, others)


def sum8(g, name):
    r = g.shape[1]

    def body(g_ref, o_ref):
        acc = g_ref[0]
        for s in range(1, N_DEV):
            acc = acc + g_ref[s]
        o_ref[...] = acc

    return pl.pallas_call(body, name=name, out_shape=jax.ShapeDtypeStruct((r, 128), F32))(g)


def ada_mod(c_all, ada_w, ada_b_cols, name):
    def body(c_ref, w_ref, b_ref, o_ref):
        o_ref[0] = mm(_silu(c_ref[...]), w_ref[0]) + b_ref[0]

    n = ada_w.shape[2]
    return pl.pallas_call(
        body, name=name, out_shape=jax.ShapeDtypeStruct((DEPTH, N_DEV, n), F32), grid=(DEPTH,),
        in_specs=[pl.BlockSpec((N_DEV, D), lambda l: (0, 0)), pl.BlockSpec((1, D, n), lambda l: (l, 0, 0)),
                  pl.BlockSpec((1, 1, n), lambda l: (l, 0, 0))],
        out_specs=pl.BlockSpec((1, N_DEV, n), lambda l: (l, 0, 0)), compiler_params=_cp(("parallel",)),
    )(c_all, ada_w, ada_b_cols.reshape(DEPTH, 1, n))


def ada_grad(c_all, dmod_cols, name):
    def body(c_ref, d_ref, o_ref):
        o_ref[0] = mm_tn(_silu(c_ref[...]), d_ref[0])

    n = dmod_cols.shape[2]
    return pl.pallas_call(
        body, name=name, out_shape=jax.ShapeDtypeStruct((DEPTH, D, n), F32), grid=(DEPTH,),
        in_specs=[pl.BlockSpec((N_DEV, D), lambda l: (0, 0)), pl.BlockSpec((1, N_DEV, n), lambda l: (l, 0, 0))],
        out_specs=pl.BlockSpec((1, D, n), lambda l: (l, 0, 0)), compiler_params=_cp(("parallel",)),
    )(c_all, dmod_cols)


def pack_w_in(w):
    out = jnp.zeros(w.shape[:-1] + (NP,), w.dtype)
    for p_off, o_off, width in _PACK:
        out = out.at[..., p_off:p_off + width].set(w[..., o_off:o_off + width])
    return out


def unpack_w_in(w):
    return jnp.concatenate([w[..., p_off:p_off + width] for p_off, _, width in _PACK], axis=-1)


def pack_w_uq(w):
    return jnp.pad(w.reshape(w.shape[:-1] + (MLA_HEADS, MLA_QK)), [(0, 0)] * (w.ndim - 1) + [(0, 0), (0, 32)]).reshape(w.shape[:-1] + (512,))


def unpack_w_uq(w):
    return w.reshape(w.shape[:-1] + (MLA_HEADS, 128))[..., :MLA_QK].reshape(w.shape[:-1] + (MLA_HEADS * MLA_QK,))


def layer_fwd(x, mod, w, cos, sin, tag):
    h1 = modnorm_fwd(x, w["norm1_g"], mod[0], mod[1], tag + "norm1")
    proj = matmul(h1, w["w_in"], "nn", F32, tag + "w_in")
    y_ssd, hs = ssd_fwd(proj, w["ssd_conv_w"], w["ssd_conv_b"], w["ssd_dt_bias"], w["ssd_a_log"], w["ssd_d"], w["ssd_norm_g"], tag + "ssd")
    q, k, v = mla_pre_fwd(proj, cos, sin, w["mla_q_norm_g"], w["mla_kv_norm_g"], w["mla_w_uq"], w["mla_w_ukv"], tag + "mla_pre")
    o, lse = mla_flash_fwd(q, k, v, tag + "mla_attn")
    y_swa = swa_fwd(proj, w["swa_sinks"], tag + "swa")
    t = x.shape[0]
    ycat = jnp.concatenate([y_ssd, jnp.transpose(o, (1, 0, 2)).reshape(t, 256), y_swa], axis=1).astype(BF16)
    y = matmul(ycat, w["w_out"], "nn", F32, tag + "w_out")
    xm = resid_fwd(x, y, mod[2], tag + "res1")
    h2 = modnorm_fwd(xm, w["norm2_g"], mod[3], mod[4], tag + "norm2")
    u0 = matmul(h2, w["ffn_w_up"], "nn", F32, tag + "w_up")
    gact = convglu_fwd(u0, w["ffn_conv_w"], w["ffn_conv_b"], tag + "glu")
    yd = matmul(gact, w["ffn_w_down"], "nn", F32, tag + "w_down")
    xo = resid_fwd(xm, yd, mod[5], tag + "res2")
    return xo, dict(x=x, h1=h1, proj=proj, hs=hs, q=q, k=k, v=v, o=o, lse=lse, ycat=ycat, y=y, xm=xm, h2=h2, u0=u0, gact=gact, yd=yd)


def layer_bwd(dxo, s, mod, w, cos, sin, tag):
    t = dxo.shape[0]
    g = {}
    dyd, dg2 = resid_bwd(dxo, s["yd"], mod[5], tag + "res2_b")
    dgact = matmul(dyd, w["ffn_w_down"], "nt", BF16, tag + "w_down_dx")
    g["ffn_w_down"] = matmul(s["gact"], dyd, "tn", F32, tag + "w_down_dw")
    du0, g["ffn_conv_w"], dcb = convglu_bwd(s["u0"], w["ffn_conv_w"], w["ffn_conv_b"], dgact, tag + "glu_b")
    g["ffn_conv_b"] = dcb[0]
    dh2 = matmul(du0, w["ffn_w_up"], "nt", F32, tag + "w_up_dx")
    g["ffn_w_up"] = matmul(s["h2"], du0, "tn", F32, tag + "w_up_dw")
    dxm, sums2 = modnorm_bwd(s["xm"], w["norm2_g"], mod[3], mod[4], dh2, dxo, tag + "norm2_b")
    g["norm2_g"] = sums2[0]
    dy, dg1 = resid_bwd(dxm, s["y"], mod[2], tag + "res1_b")
    dycat = matmul(dy, w["w_out"], "nt", F32, tag + "w_out_dx")
    g["w_out"] = matmul(s["ycat"], dy, "tn", F32, tag + "w_out_dw")
    proj = s["proj"]
    dz, dxbc, ddt, g["ssd_conv_w"], vec = ssd_bwd(proj, s["hs"], dycat[:, 0:512], w["ssd_conv_w"], w["ssd_conv_b"], w["ssd_dt_bias"],
                                                 w["ssd_a_log"], w["ssd_d"], w["ssd_norm_g"], tag + "ssd_b")
    g["ssd_conv_b"], g["ssd_dt_bias"], g["ssd_a_log"], g["ssd_d"], g["ssd_norm_g"] = vec[0], vec[1, :8], vec[2, :8], vec[3, :8], vec[4, :512]
    do = jnp.transpose(dycat[:, 512:768].reshape(t, MLA_HEADS, 64), (1, 0, 2))
    dq, dk, dv = mla_flash_bwd(s["q"], s["k"], s["v"], s["o"], s["lse"], do, tag + "mla_attn_b")
    dcq, dckv, dkr, g["mla_w_uq"], g["mla_w_ukv"], mvec = mla_pre_bwd(proj, cos, sin, w["mla_q_norm_g"], w["mla_kv_norm_g"],
                                                                    w["mla_w_uq"], w["mla_w_ukv"], dq, dk, dv, tag + "mla_pre_b")
    g["mla_q_norm_g"], g["mla_kv_norm_g"] = mvec[0], mvec[1, :128]
    dsq, dsk, dsv, dsink = swa_bwd(proj, w["swa_sinks"], dycat[:, 768:1024], tag + "swa_b")
    g["swa_sinks"] = dsink[0, :4]
    dproj = jnp.concatenate([dxbc, dz, dcq, dsq, dckv, ddt, dkr, dsk, dsv], axis=1).astype(BF16)
    dh1 = matmul(dproj, w["w_in"], "nt", F32, tag + "w_in_dx")
    g["w_in"] = matmul(s["h1"], dproj, "tn", F32, tag + "w_in_dw")
    dx, sums1 = modnorm_bwd(s["x"], w["norm1_g"], mod[0], mod[1], dh1, dxm, tag + "norm1_b")
    g["norm1_g"] = sums1[0]
    dmod = jnp.stack([sums1[1], sums1[2], dg1[0], sums2[1], sums2[2], dg2[0]])
    return dx, dmod, g


def local_step(x, tgt, mods, ws, final_norm_g, cos, sin):
    saved = []
    for l in range(len(ws)):
        x, s = layer_fwd(x, mods[l], ws[l], cos, sin, f"l{l}_")
        saved.append(s)
    loss, dx, dfg = loss_fwd_bwd(x, final_norm_g, tgt, "loss")
    dmods, grads = [None] * len(ws), [None] * len(ws)
    for l in reversed(range(len(ws))):
        dx, dmods[l], grads[l] = layer_bwd(dx, saved[l], mods[l], ws[l], cos, sin, f"l{l}_")
    return loss, dx, dfg[0], jnp.stack(dmods), grads


WEIGHTS = ("ada_w", "ada_b", "norm1_g", "norm2_g", "w_in", "ssd_conv_w", "ssd_conv_b", "ssd_dt_bias", "ssd_a_log", "ssd_d",
           "ssd_norm_g", "mla_q_norm_g", "mla_w_uq", "mla_kv_norm_g", "mla_w_ukv", "swa_sinks", "w_out", "ffn_w_up",
           "ffn_conv_w", "ffn_conv_b", "ffn_w_down", "final_norm_g")
BIG = (("w_in", 2), ("w_out", 1), ("ffn_w_up", 2), ("ffn_w_down", 1), ("mla_w_uq", 2), ("mla_w_ukv", 2))
SMALL = (("dmod", 6 * D), ("norm1_g", D), ("norm2_g", D), ("ssd_conv_w", 4 * SSD_XBC), ("ssd_conv_b", SSD_XBC), ("ssd_dt_bias", 128),
         ("ssd_a_log", 128), ("ssd_d", 128), ("ssd_norm_g", SSD_INNER), ("mla_q_norm_g", 256), ("mla_kv_norm_g", 128),
         ("swa_sinks", 128), ("ffn_conv_w", 3 * 2 * D_FF), ("ffn_conv_b", 2 * D_FF))
SMALL_LAYER = sum(n for _, n in SMALL)
SMALL_SHAPES = {"norm1_g": (D,), "norm2_g": (D,), "ssd_conv_w": (4, SSD_XBC), "ssd_conv_b": (SSD_XBC,), "ssd_dt_bias": (8,),
                "ssd_a_log": (8,), "ssd_d": (8,), "ssd_norm_g": (SSD_INNER,), "mla_q_norm_g": (256,), "mla_kv_norm_g": (128,),
                "swa_sinks": (4,), "ffn_conv_w": (3, 2 * D_FF), "ffn_conv_b": (2 * D_FF,)}


def _lanes(v, n):
    v = v.reshape(-1)
    return jnp.pad(v, (0, n - v.shape[0]))


def _tile_rows(flat):
    n = -(-flat.shape[0] // 1024) * 1024
    return jnp.pad(flat, (0, n - flat.shape[0])).reshape(-1, 128)


def _rope_tables(positions):
    inv_freq = 10000.0 ** (-jnp.arange(0, 32, 2, dtype=F32) / 32)
    ang = positions.astype(F32).reshape(-1, 1) * inv_freq
    return jnp.cos(ang), jnp.sin(ang)


def _gather_big(shards, c):
    rb = sum(s.size for s in shards) // 256
    flat = jnp.concatenate([s.astype(BF16).reshape(-1, 128) for s in shards], axis=0).reshape(2, rb, 128)
    got = allgather8(lax.dynamic_index_in_dim(flat, c, 0, keepdims=False), "ag_weights", False).reshape(N_CHIP, 2 * rb, 128)
    fulls, off = [], 0
    for (_, axis), s in zip(BIG, shards):
        rows = s.size // 128
        fulls.append(jnp.concatenate([got[k, off:off + rows].reshape(s.shape) for k in range(N_CHIP)], axis=axis))
        off += rows
    return fulls


def _reduce_big(grads, x, y, c):
    chips = []
    for k in range(N_CHIP):
        parts = []
        for (_, axis), g in zip(BIG, grads):
            n = g.shape[axis] // N_CHIP
            parts.append(lax.slice_in_dim(g, k * n, (k + 1) * n, axis=axis).astype(BF16).reshape(-1, 128))
        chips.append(jnp.concatenate(parts, axis=0))
    rb = chips[0].shape[0] // 2
    halves = jnp.stack(chips).reshape(N_CHIP, 2, rb, 128).transpose(1, 0, 2, 3).reshape(2, N_CHIP * rb, 128)
    theirs = flip_exchange(halves, [((0, 0, 1), lambda x, y, c: 1 - c, lambda x, y, c: 0)], 1, "rs_sibling")
    mine = lax.dynamic_index_in_dim(halves, c, 0, keepdims=False)
    chip_sum = add_pairs(mine, theirs[0], BF16, "rs_add_sibling").reshape(N_CHIP, rb, 128)
    plan = [((fx, fy, 0), (lambda x, y, c, fx=fx, fy=fy: 2 * (x ^ fx) + (y ^ fy)), (lambda x, y, c, s=s: s))
            for s, (fx, fy) in enumerate(CHIP_FLIPS)]
    others = flip_exchange(chip_sum, plan, len(CHIP_FLIPS), "rs_chips")
    own = lax.dynamic_index_in_dim(chip_sum, 2 * x + y, 0, keepdims=False)
    half = add_slots(own, others, "rs_add_chips")
    both = flip_exchange(half[None], [(None, lambda x, y, c: 0, lambda x, y, c: c),
                                      ((0, 0, 1), lambda x, y, c: 0, lambda x, y, c: c)], 2, "rs_share")
    flat = both.reshape(2 * rb, 128)
    out, off = [], 0
    for (_, axis), g in zip(BIG, grads):
        shape = list(g.shape)
        shape[axis] //= N_CHIP
        rows = math.prod(shape) // 128
        out.append(flat[off:off + rows].reshape(shape))
        off += rows
    return out


def kernel(x, c, positions, ada_w, ada_b, norm1_g, norm2_g, w_in, ssd_conv_w, ssd_conv_b, ssd_dt_bias, ssd_a_log, ssd_d, ssd_norm_g, mla_q_norm_g, mla_w_uq, mla_kv_norm_g, mla_w_ukv, swa_sinks, w_out, ffn_w_up, ffn_conv_w, ffn_conv_b, ffn_w_down, final_norm_g, loss_target, m_ada_w, m_ada_b, m_norm1_g, m_norm2_g, m_w_in, m_ssd_conv_w, m_ssd_conv_b, m_ssd_dt_bias, m_ssd_a_log, m_ssd_d, m_ssd_norm_g, m_mla_q_norm_g, m_mla_w_uq, m_mla_kv_norm_g, m_mla_w_ukv, m_swa_sinks, m_w_out, m_ffn_w_up, m_ffn_conv_w, m_ffn_conv_b, m_ffn_w_down, m_final_norm_g, v_ada_w, v_ada_b, v_norm1_g, v_norm2_g, v_w_in, v_ssd_conv_w, v_ssd_conv_b, v_ssd_dt_bias, v_ssd_a_log, v_ssd_d, v_ssd_norm_g, v_mla_q_norm_g, v_mla_w_uq, v_mla_kv_norm_g, v_mla_w_ukv, v_swa_sinks, v_w_out, v_ffn_w_up, v_ffn_conv_w, v_ffn_conv_b, v_ffn_w_down, v_final_norm_g):
    args = locals()
    wt = {n: args[n] for n in WEIGHTS}
    mx, my, mc = _place()
    chip = 2 * mx + my
    dev = 2 * chip + mc
    n_ada = ada_w.shape[2]

    pack = _tile_rows(jnp.concatenate([c.reshape(-1), ssd_conv_w.reshape(-1), ffn_conv_w.reshape(-1)]))
    got = allgather8(pack, "ag_small_in", True).reshape(N_DEV, -1)
    c_all = got[:, :D]
    per_chip = got[0::2]
    n_scw = ssd_conv_w.size
    ssd_cw = jnp.concatenate([per_chip[k, D:D + n_scw].reshape(ssd_conv_w.shape) for k in range(N_CHIP)], axis=2)
    n_fcw = ffn_conv_w.size
    ffn_cw = jnp.concatenate([per_chip[k, D + n_scw:D + n_scw + n_fcw].reshape(ffn_conv_w.shape) for k in range(N_CHIP)], axis=2)

    ada_b_cols = lax.dynamic_slice_in_dim(ada_b, chip * n_ada, n_ada, axis=1)
    mod_cols = ada_mod(c_all, ada_w, ada_b_cols, "ada_mod")
    mod_all = allgather8(mod_cols.reshape(-1, 128), "ag_mod", True)[0::2].reshape(N_CHIP, DEPTH, N_DEV, n_ada)
    mods = lax.dynamic_index_in_dim(mod_all, dev, 2, keepdims=False).transpose(1, 0, 2).reshape(DEPTH, 6, D)

    full = dict(zip([n for n, _ in BIG], _gather_big([wt[n] for n, _ in BIG], mc)))
    full["w_in"] = pack_w_in(full["w_in"])
    full["mla_w_uq"] = pack_w_uq(full["mla_w_uq"])
    ws = []
    for l in range(DEPTH):
        w = {n: full[n][l] for n, _ in BIG}
        w.update(ssd_conv_w=ssd_cw[l], ffn_conv_w=ffn_cw[l])
        for n in ("norm1_g", "norm2_g", "ssd_conv_b", "ssd_dt_bias", "ssd_a_log", "ssd_d", "ssd_norm_g", "mla_q_norm_g",
                  "mla_kv_norm_g", "swa_sinks", "ffn_conv_b"):
            w[n] = wt[n][l]
        ws.append(w)

    cos, sin = _rope_tables(positions)
    t = x.shape[1]
    loss8, dx, dfg, dmods, lg = local_step(x.reshape(t, D), loss_target.reshape(t, D), mods, ws, final_norm_g, cos, sin)
    loss = lax.psum(loss8[0, 0], ("x", "y", "c"))

    rows = []
    for l in range(DEPTH):
        for name, n in SMALL:
            rows.append(_lanes(dmods[l] if name == "dmod" else lg[l][name], n))
    rows.append(dfg)
    small = allgather8(_tile_rows(jnp.concatenate(rows)), "ag_small_grads", True)
    total = sum8(small, "sum_small_grads").reshape(-1)
    grads = {}
    per_layer = {name: [] for name, _ in SMALL}
    for l in range(DEPTH):
        off = l * SMALL_LAYER
        for name, n in SMALL:
            per_layer[name].append(total[off:off + n])
            off += n
    grads["ada_b"] = jnp.stack(per_layer["dmod"])
    for name, shape in SMALL_SHAPES.items():
        grads[name] = jnp.stack([v[:math.prod(shape)].reshape(shape) for v in per_layer[name]])
    grads["final_norm_g"] = total[DEPTH * SMALL_LAYER:DEPTH * SMALL_LAYER + D]
    for name in ("ssd_conv_w", "ffn_conv_w"):
        n = grads[name].shape[2] // N_CHIP
        grads[name] = lax.dynamic_slice_in_dim(grads[name], chip * n, n, axis=2)
    dmod_all = small.reshape(N_DEV, -1)[:, :DEPTH * SMALL_LAYER].reshape(N_DEV, DEPTH, SMALL_LAYER)[:, :, :6 * D]
    dmod_cols = lax.dynamic_slice_in_dim(dmod_all, chip * n_ada, n_ada, axis=2).transpose(1, 0, 2)
    grads["ada_w"] = ada_grad(c_all, dmod_cols, "ada_grad")

    stacked = []
    for name, _ in BIG:
        g = jnp.stack([lg[l][name] for l in range(DEPTH)])
        if name == "w_in":
            g = unpack_w_in(g)
        if name == "mla_w_uq":
            g = unpack_w_uq(g)
        stacked.append(g)
    for (name, _), g in zip(BIG, _reduce_big(stacked, mx, my, mc)):
        grads[name] = g

    deltas, new_m, new_v = {}, {}, {}
    for n in WEIGHTS:
        deltas[n], new_m[n], new_v[n] = adamw(wt[n], grads[n], args["m_" + n], args["v_" + n], "adamw_" + n)
    return (loss, dx.reshape(x.shape), *[grads[n] for n in WEIGHTS], *[deltas[n] for n in WEIGHTS],
            *[new_m[n] for n in WEIGHTS], *[new_v[n] for n in WEIGHTS])
```

```python
import functools
import math

import jax
import jax.numpy as jnp
from jax import lax
from jax.experimental import pallas as pl
from jax.experimental.pallas import tpu as pltpu

F32 = jnp.float32
BF16 = jnp.bfloat16
MXU_DTYPE = BF16

D = 1024
DEPTH = 4
EPS = 1e-6
N_DEV = 8
N_CHIP = 4

SSD_HEADS = 8
SSD_INNER = 512
SSD_STATE = 128
SSD_XBC = 1024
Q = 128
MLA_HEADS = 4
MLA_QK = 96
D_FF = 2816
D_IN = 2472

P_XBC, P_Z, P_CQ, P_SQ, P_CKV, P_DT, P_KR, P_SK, P_SV = 0, 1024, 1536, 1792, 2048, 2176, 2304, 2432, 2560
NP = 2688
_PACK = ((P_Z, 0, 512), (P_XBC, 512, 1024), (P_DT, 1536, 8), (P_CQ, 1544, 256), (P_CKV, 1800, 128),
         (P_KR, 1928, 32), (P_SQ, 1960, 256), (P_SK, 2216, 128), (P_SV, 2344, 128))

ADAM_LR, ADAM_B1, ADAM_B2, ADAM_EPS, ADAM_WD, ADAM_STEP = 0.001, 0.9, 0.999, 1e-08, 0.01, 10

VMEM_LIMIT = 56 * 1024 * 1024
NEG = -1e30


def _cp(sem=None):
    return pltpu.CompilerParams(dimension_semantics=sem, vmem_limit_bytes=VMEM_LIMIT)


def _dot(a, b, dims):
    return lax.dot_general(a.astype(MXU_DTYPE), b.astype(MXU_DTYPE), (dims, ((), ())), preferred_element_type=F32)


_NN = ((1,), (0,))
_NT = ((1,), (1,))
_TN = ((0,), (0,))


@jax.custom_vjp
def mm(a, b):
    return _dot(a, b, _NN)


mm.defvjp(lambda a, b: (_dot(a, b, _NN), (a, b)),
          lambda r, g: (_dot(g, r[1], _NT), _dot(r[0], g, _TN)))


@jax.custom_vjp
def mm_nt(a, b):
    return _dot(a, b, _NT)


mm_nt.defvjp(lambda a, b: (_dot(a, b, _NT), (a, b)),
             lambda r, g: (_dot(g, r[1], _NN), _dot(g, r[0], _TN)))


@jax.custom_vjp
def mm_tn(a, b):
    return _dot(a, b, _TN)


mm_tn.defvjp(lambda a, b: (_dot(a, b, _TN), (a, b)),
             lambda r, g: (_dot(r[1], g, _NT), _dot(r[0], g, _NN)))


def _silu(x):
    return x * jax.nn.sigmoid(x)


def _rms(x, g):
    return x * lax.rsqrt(jnp.mean(x * x, axis=-1, keepdims=True) + EPS) * g


def _modnorm(x, g, sh, sc):
    return _rms(x, g) * (1.0 + sc) + sh


def _blk(dim, target, mult=128):
    best = None
    for b in range(mult, min(dim, target) + 1, mult):
        if dim % b == 0:
            best = b
    return best if best is not None else dim


def matmul(a, b, mode, out_dtype, name):
    if mode == "nn":
        (m, k), n = a.shape, b.shape[1]
    elif mode == "nt":
        (m, k), n = a.shape, b.shape[0]
    else:
        (k, m), n = a.shape, b.shape[1]
    bm, bn, bk = _blk(m, 512), _blk(n, 1408), _blk(k, 2816)
    nk = k // bk
    dims = {"nn": _NN, "nt": _NT, "tn": _TN}[mode]

    def body(a_ref, b_ref, o_ref, acc_ref):
        kk = pl.program_id(2)
        part = _dot(a_ref[...], b_ref[...], dims)
        if nk == 1:
            o_ref[...] = part.astype(o_ref.dtype)
            return

        @pl.when(kk == 0)
        def _():
            acc_ref[...] = part

        @pl.when((kk > 0) & (kk < nk - 1))
        def _():
            acc_ref[...] += part

        @pl.when(kk == nk - 1)
        def _():
            o_ref[...] = (acc_ref[...] + part).astype(o_ref.dtype)

    a_spec = pl.BlockSpec((bk, bm), lambda i, j, kk: (kk, i)) if mode == "tn" else pl.BlockSpec((bm, bk), lambda i, j, kk: (i, kk))
    b_spec = pl.BlockSpec((bn, bk), lambda i, j, kk: (j, kk)) if mode == "nt" else pl.BlockSpec((bk, bn), lambda i, j, kk: (kk, j))
    return pl.pallas_call(
        body, name=name, out_shape=jax.ShapeDtypeStruct((m, n), out_dtype), grid=(m // bm, n // bn, nk),
        in_specs=[a_spec, b_spec], out_specs=pl.BlockSpec((bm, bn), lambda i, j, kk: (i, j)),
        scratch_shapes=[pltpu.VMEM((bm, bn), F32)], compiler_params=_cp(("parallel", "parallel", "arbitrary")),
    )(a, b)


TM = 512


def _row(v):
    return v.reshape(1, -1)


def modnorm_fwd(x, g, sh, sc, name):
    t = x.shape[0]

    def body(x_ref, g_ref, sh_ref, sc_ref, o_ref):
        o_ref[...] = _modnorm(x_ref[...], g_ref[...], sh_ref[...], sc_ref[...]).astype(o_ref.dtype)

    vec = pl.BlockSpec((1, D), lambda i: (0, 0))
    return pl.pallas_call(
        body, name=name, out_shape=jax.ShapeDtypeStruct((t, D), BF16), grid=(t // TM,),
        in_specs=[pl.BlockSpec((TM, D), lambda i: (i, 0)), vec, vec, vec],
        out_specs=pl.BlockSpec((TM, D), lambda i: (i, 0)), compiler_params=_cp(("parallel",)),
    )(x, _row(g), _row(sh), _row(sc))


def modnorm_bwd(x, g, sh, sc, dh, dres, name):
    t = x.shape[0]

    def body(x_ref, g_ref, sh_ref, sc_ref, dh_ref, dres_ref, dx_ref, sums_ref):
        _, vjp = jax.vjp(_modnorm, x_ref[...], g_ref[...], sh_ref[...], sc_ref[...])
        dx, dg, dsh, dsc = vjp(dh_ref[...].astype(F32))
        dx_ref[...] = dx + dres_ref[...]

        @pl.when(pl.program_id(0) == 0)
        def _():
            sums_ref[...] = jnp.zeros_like(sums_ref)

        sums_ref[0:1, :] += dg
        sums_ref[1:2, :] += dsh
        sums_ref[2:3, :] += dsc

    vec = pl.BlockSpec((1, D), lambda i: (0, 0))
    tile = pl.BlockSpec((TM, D), lambda i: (i, 0))
    return pl.pallas_call(
        body, name=name, out_shape=(jax.ShapeDtypeStruct((t, D), F32), jax.ShapeDtypeStruct((8, D), F32)), grid=(t // TM,),
        in_specs=[tile, vec, vec, vec, tile, tile], out_specs=(tile, pl.BlockSpec((8, D), lambda i: (0, 0))),
        compiler_params=_cp(("arbitrary",)),
    )(x, _row(g), _row(sh), _row(sc), dh, dres)


def resid_fwd(x, y, gate, name):
    t = x.shape[0]

    def body(x_ref, y_ref, g_ref, o_ref):
        o_ref[...] = x_ref[...] + g_ref[...] * y_ref[...]

    tile = pl.BlockSpec((TM, D), lambda i: (i, 0))
    return pl.pallas_call(
        body, name=name, out_shape=jax.ShapeDtypeStruct((t, D), F32), grid=(t // TM,),
        in_specs=[tile, tile, pl.BlockSpec((1, D), lambda i: (0, 0))], out_specs=tile, compiler_params=_cp(("parallel",)),
    )(x, y, _row(gate))


def resid_bwd(dxo, y, gate, name):
    t = dxo.shape[0]

    def body(d_ref, y_ref, g_ref, dy_ref, dg_ref):
        d = d_ref[...]
        dy_ref[...] = (d * g_ref[...]).astype(BF16)

        @pl.when(pl.program_id(0) == 0)
        def _():
            dg_ref[...] = jnp.zeros_like(dg_ref)

        dg_ref[0:1, :] += jnp.sum(d * y_ref[...], axis=0, keepdims=True)

    tile = pl.BlockSpec((TM, D), lambda i: (i, 0))
    return pl.pallas_call(
        body, name=name, out_shape=(jax.ShapeDtypeStruct((t, D), BF16), jax.ShapeDtypeStruct((8, D), F32)), grid=(t // TM,),
        in_specs=[tile, tile, pl.BlockSpec((1, D), lambda i: (0, 0))], out_specs=(tile, pl.BlockSpec((8, D), lambda i: (0, 0))),
        compiler_params=_cp(("arbitrary",)),
    )(dxo, y, _row(gate))


CW = 256
NCW = D_FF // CW


def _conv3(u, halo, w, b):
    n = u.shape[0]
    win = jnp.concatenate([halo, u], axis=0)
    return b + w[0:1] * win[6:6 + n] + w[1:2] * win[7:7 + n] + w[2:3] * win[8:8 + n]


def _convglu(ua, ub, ha, hb, wa, wb, ba, bb):
    return _silu(_conv3(ua, ha, wa, ba)) * _conv3(ub, hb, wb, bb)


def _halo_specs(tm, cw, off):
    r = tm // 8
    return pl.BlockSpec((8, cw), lambda j, i, o=off: (jnp.maximum(i * r - 1, 0), j + o))


def convglu_fwd(u0, cw, cb, name):
    t = u0.shape[0]

    def body(ua_ref, ub_ref, ha_ref, hb_ref, wa_ref, wb_ref, ba_ref, bb_ref, o_ref):
        keep = (pl.program_id(1) > 0).astype(F32)
        o_ref[...] = _convglu(ua_ref[...], ub_ref[...], ha_ref[...] * keep, hb_ref[...] * keep,
                              wa_ref[...], wb_ref[...], ba_ref[...], bb_ref[...]).astype(o_ref.dtype)

    def col(rows, off):
        return pl.BlockSpec((rows, CW), lambda j, i, o=off: (0, j + o))

    return pl.pallas_call(
        body, name=name, out_shape=jax.ShapeDtypeStruct((t, D_FF), BF16), grid=(NCW, t // TM),
        in_specs=[pl.BlockSpec((TM, CW), lambda j, i: (i, j)), pl.BlockSpec((TM, CW), lambda j, i: (i, j + NCW)),
                  _halo_specs(TM, CW, 0), _halo_specs(TM, CW, NCW), col(3, 0), col(3, NCW), col(1, 0), col(1, NCW)],
        out_specs=pl.BlockSpec((TM, CW), lambda j, i: (i, j)), compiler_params=_cp(("parallel", "parallel")),
    )(u0, u0, u0, u0, cw, cw, _row(cb), _row(cb))


def convglu_bwd(u0, cw, cb, dgact, name):
    t = u0.shape[0]
    nt = t // TM

    def body(ua_ref, ub_ref, ha_ref, hb_ref, wa_ref, wb_ref, ba_ref, bb_ref, dg_ref,
             dua_ref, dub_ref, dwa_ref, dwb_ref, dba_ref, dbb_ref, ca_ref, cb_ref):
        step = pl.program_id(1)
        keep = (step < nt - 1).astype(F32)

        @pl.when(step == 0)
        def _():
            ca_ref[...] = jnp.zeros_like(ca_ref)
            cb_ref[...] = jnp.zeros_like(cb_ref)
            dwa_ref[...] = jnp.zeros_like(dwa_ref)
            dwb_ref[...] = jnp.zeros_like(dwb_ref)
            dba_ref[...] = jnp.zeros_like(dba_ref)
            dbb_ref[...] = jnp.zeros_like(dbb_ref)

        _, vjp = jax.vjp(_convglu, ua_ref[...], ub_ref[...], ha_ref[...] * keep, hb_ref[...] * keep,
                         wa_ref[...], wb_ref[...], ba_ref[...], bb_ref[...])
        dua, dub, dha, dhb, dwa, dwb, dba, dbb = vjp(dg_ref[...].astype(F32))
        zeros = jnp.zeros((TM - 8, CW), F32)
        dua_ref[...] = (dua + jnp.concatenate([zeros, ca_ref[...]], axis=0)).astype(BF16)
        dub_ref[...] = (dub + jnp.concatenate([zeros, cb_ref[...]], axis=0)).astype(BF16)
        ca_ref[...] = dha * keep
        cb_ref[...] = dhb * keep
        dwa_ref[...] += dwa
        dwb_ref[...] += dwb
        dba_ref[...] += dba
        dbb_ref[...] += dbb

    def rev(i):
        return nt - 1 - i

    def tile(off):
        return pl.BlockSpec((TM, CW), lambda j, i, o=off: (rev(i), j + o))

    def halo(off):
        r = TM // 8
        return pl.BlockSpec((8, CW), lambda j, i, o=off: (jnp.maximum(rev(i) * r - 1, 0), j + o))

    def col(rows, off):
        return pl.BlockSpec((rows, CW), lambda j, i, o=off: (0, j + o))

    outs = pl.pallas_call(
        body, name=name,
        out_shape=(jax.ShapeDtypeStruct((t, D_FF), BF16), jax.ShapeDtypeStruct((t, D_FF), BF16),
                   jax.ShapeDtypeStruct((3, D_FF), F32), jax.ShapeDtypeStruct((3, D_FF), F32),
                   jax.ShapeDtypeStruct((1, D_FF), F32), jax.ShapeDtypeStruct((1, D_FF), F32)),
        grid=(NCW, nt),
        in_specs=[tile(0), tile(NCW), halo(0), halo(NCW), col(3, 0), col(3, NCW), col(1, 0), col(1, NCW), tile(0)],
        out_specs=(tile(0), tile(0), col(3, 0), col(3, 0), col(1, 0), col(1, 0)),
        scratch_shapes=[pltpu.VMEM((8, CW), F32), pltpu.VMEM((8, CW), F32)],
        compiler_params=_cp(("parallel", "arbitrary")),
    )(u0, u0, u0, u0, cw, cw, _row(cb), _row(cb), dgact)
    dua, dub, dwa, dwb, dba, dbb = outs
    return (jnp.concatenate([dua, dub], axis=1), jnp.concatenate([dwa, dwb], axis=1), jnp.concatenate([dba, dbb], axis=1))


def _pick(v, h, axis):
    return v[:, h:h + 1] if axis == 1 else v[h:h + 1, :]


def _ssd_chunk(z, xh, xc, dtp, hin, cw, cb, dtb, alog, dsk, ng):
    lane_hi = lax.broadcasted_iota(jnp.int32, (Q, Q), 1) >= 64
    row_hi = lax.broadcasted_iota(jnp.int32, (Q, Q), 0) >= 64
    causal = lax.broadcasted_iota(jnp.int32, (Q, Q), 0) >= lax.broadcasted_iota(jnp.int32, (Q, Q), 1)
    win = jnp.concatenate([xh, xc], axis=0)
    xbc = cb
    for k in range(4):
        xbc = xbc + cw[k:k + 1] * win[5 + k:5 + k + Q]
    xbc = _silu(xbc)
    xs, bm, cm = xbc[:, 0:512], xbc[:, 512:768], xbc[:, 768:1024]
    dt = jax.nn.softplus(dtp + dtb)
    da = dt * (-jnp.exp(alog))
    ah = jnp.dot(causal.astype(F32), da, precision=lax.Precision.HIGHEST, preferred_element_type=F32)
    aht = ah.T
    alast = ah[Q - 1:Q, :]
    eah = jnp.exp(ah)
    dte = jnp.exp(alast - ah)
    elast = jnp.exp(alast)
    ys, houts = [], []
    for g in range(2):
        bg, cg = bm[:, 128 * g:128 * g + 128], cm[:, 128 * g:128 * g + 128]
        cbm = mm_nt(cg, bg)
        for jp in range(2):
            j = 2 * g + jp
            h0, h1 = 2 * j, 2 * j + 1
            xp = xs[:, 128 * j:128 * j + 128]
            xdt = xp * jnp.where(lane_hi, _pick(dt, h1, 1), _pick(dt, h0, 1))
            yd, st = [], []
            for h in (h0, h1):
                seg = _pick(ah, h, 1) - _pick(aht, h, 0)
                decay = jnp.exp(jnp.where(causal, seg, NEG))
                yd.append(mm(cbm * decay, xdt))
                st.append(mm_tn(xdt * _pick(dte, h, 1), bg))
            hj = hin[j]
            hout = hj * jnp.where(row_hi, _pick(elast, h1, 1), _pick(elast, h0, 1)) + jnp.where(row_hi, st[1], st[0])
            yoff = mm_nt(cg, hj) * jnp.where(lane_hi, _pick(eah, h1, 1), _pick(eah, h0, 1))
            skip = xp * jnp.where(lane_hi[0:1], _pick(dsk, h1, 1), _pick(dsk, h0, 1))
            ys.append(jnp.where(lane_hi, yd[1], yd[0]) + yoff + skip)
            houts.append(hout)
    y = jnp.concatenate(ys, axis=1) * _silu(z)
    yn = []
    for g in range(2):
        yg = y[:, 256 * g:256 * g + 256]
        yn.append(yg * lax.rsqrt(jnp.mean(yg * yg, axis=-1, keepdims=True) + EPS))
    return jnp.concatenate(yn, axis=1) * ng, jnp.stack(houts)


def _pad_lanes(v, n=128):
    v = v.reshape(1, -1)
    return jnp.pad(v, ((0, 0), (0, n - v.shape[1])))


def _ssd_in_specs(chunk_of):
    return [pl.BlockSpec((Q, 512), lambda i: (chunk_of(i), P_Z // 512)),
            pl.BlockSpec((8, 1024), lambda i: (jnp.maximum(chunk_of(i) * (Q // 8) - 1, 0), P_XBC // 1024)),
            pl.BlockSpec((Q, 1024), lambda i: (chunk_of(i), P_XBC // 1024)),
            pl.BlockSpec((Q, 128), lambda i: (chunk_of(i), P_DT // 128))]


def _full(shape):
    nd = len(shape)
    return pl.BlockSpec(shape, lambda i: (0,) * nd)


def ssd_fwd(proj, cw, cb, dtb, alog, dsk, ng, name):
    t = proj.shape[0]
    nc = t // Q

    def body(z_ref, xh_ref, xc_ref, dt_ref, cw_ref, cb_ref, dtb_ref, al_ref, dsk_ref, ng_ref, y_ref, hs_ref, h_ref):
        i = pl.program_id(0)

        @pl.when(i == 0)
        def _():
            h_ref[...] = jnp.zeros_like(h_ref)

        hin = h_ref[...]
        hs_ref[0] = hin
        y, hout = _ssd_chunk(z_ref[...], xh_ref[...] * (i > 0).astype(F32), xc_ref[...], dt_ref[...], hin,
                             cw_ref[...], cb_ref[...], dtb_ref[...], al_ref[...], dsk_ref[...], ng_ref[...])
        y_ref[...] = y
        h_ref[...] = hout

    return pl.pallas_call(
        body, name=name,
        out_shape=(jax.ShapeDtypeStruct((t, 512), F32), jax.ShapeDtypeStruct((nc, 4, 128, 128), F32)), grid=(nc,),
        in_specs=_ssd_in_specs(lambda i: i) + [_full((4, 1024)), _full((1, 1024)), _full((1, 128)), _full((1, 128)),
                                               _full((1, 128)), _full((1, 512))],
        out_specs=(pl.BlockSpec((Q, 512), lambda i: (i, 0)), pl.BlockSpec((1, 4, 128, 128), lambda i: (i, 0, 0, 0))),
        scratch_shapes=[pltpu.VMEM((4, 128, 128), F32)], compiler_params=_cp(("arbitrary",)),
    )(proj, proj, proj, proj, cw, _row(cb), _pad_lanes(dtb), _pad_lanes(alog), _pad_lanes(dsk), _row(ng))


def ssd_bwd(proj, hs, dy, cw, cb, dtb, alog, dsk, ng, name):
    t = proj.shape[0]
    nc = t // Q

    def body(z_ref, xh_ref, xc_ref, dt_ref, hs_ref, dy_ref, cw_ref, cb_ref, dtb_ref, al_ref, dsk_ref, ng_ref,
             dz_ref, dx_ref, ddt_ref, dcw_ref, vec_ref, dh_ref, carry_ref):
        step = pl.program_id(0)
        keep = (step < nc - 1).astype(F32)

        @pl.when(step == 0)
        def _():
            dh_ref[...] = jnp.zeros_like(dh_ref)
            carry_ref[...] = jnp.zeros_like(carry_ref)
            dcw_ref[...] = jnp.zeros_like(dcw_ref)
            vec_ref[...] = jnp.zeros_like(vec_ref)

        _, vjp = jax.vjp(_ssd_chunk, z_ref[...], xh_ref[...] * keep, xc_ref[...], dt_ref[...], hs_ref[0],
                         cw_ref[...], cb_ref[...], dtb_ref[...], al_ref[...], dsk_ref[...], ng_ref[...])
        dz, dxh, dxc, ddt, dhin, dcw, dcb, ddtb, dal, ddsk, dng = vjp((dy_ref[...], dh_ref[...]))
        dz_ref[...] = dz
        dx_ref[...] = dxc + jnp.concatenate([jnp.zeros((Q - 8, 1024), F32), carry_ref[...]], axis=0)
        ddt_ref[...] = ddt
        carry_ref[...] = dxh * keep
        dh_ref[...] = dhin
        dcw_ref[...] += dcw
        vec_ref[0:1, :] += dcb
        vec_ref[1:2, 0:128] += ddtb
        vec_ref[2:3, 0:128] += dal
        vec_ref[3:4, 0:128] += ddsk
        vec_ref[4:5, 0:512] += dng

    def rev(i):
        return nc - 1 - i

    return pl.pallas_call(
        body, name=name,
        out_shape=(jax.ShapeDtypeStruct((t, 512), F32), jax.ShapeDtypeStruct((t, 1024), F32), jax.ShapeDtypeStruct((t, 128), F32),
                   jax.ShapeDtypeStruct((4, 1024), F32), jax.ShapeDtypeStruct((8, 1024), F32)),
        grid=(nc,),
        in_specs=_ssd_in_specs(rev) + [pl.BlockSpec((1, 4, 128, 128), lambda i: (rev(i), 0, 0, 0)),
                                       pl.BlockSpec((Q, 512), lambda i: (rev(i), 0)),
                                       _full((4, 1024)), _full((1, 1024)), _full((1, 128)), _full((1, 128)), _full((1, 128)),
                                       _full((1, 512))],
        out_specs=(pl.BlockSpec((Q, 512), lambda i: (rev(i), 0)), pl.BlockSpec((Q, 1024), lambda i: (rev(i), 0)),
                   pl.BlockSpec((Q, 128), lambda i: (rev(i), 0)), _full((4, 1024)), _full((8, 1024))),
        scratch_shapes=[pltpu.VMEM((4, 128, 128), F32), pltpu.VMEM((8, 1024), F32)], compiler_params=_cp(("arbitrary",)),
    )(proj, proj, proj, proj, hs, dy, cw, _row(cb), _pad_lanes(dtb), _pad_lanes(alog), _pad_lanes(dsk), _row(ng))


TA = 512
MLA_SCALE = 1.0 / math.sqrt(MLA_QK)


def _rope(x1, x2, cos, sin):
    return x1 * cos - x2 * sin, x1 * sin + x2 * cos


def _mla_pre(cq, ckv, kr, cos, sin, qg, kvg, wuq, wukv):
    n = cq.shape[0]
    qh = mm(_rms(cq, qg), wuq)
    kv = mm(_rms(ckv, kvg), wukv)
    kr1, kr2 = _rope(kr[:, 0:16], kr[:, 16:32], cos, sin)
    pad = jnp.zeros((n, 32), F32)
    qs, ks, vs = [], [], []
    for h in range(MLA_HEADS):
        b = qh[:, 128 * h:128 * h + 128]
        q1, q2 = _rope(b[:, 64:80], b[:, 80:96], cos, sin)
        qs.append(jnp.concatenate([b[:, 0:64], q1, q2, pad], axis=1))
        ks.append(jnp.concatenate([kv[:, 128 * h:128 * h + 64], kr1, kr2, pad], axis=1))
        vs.append(kv[:, 128 * h + 64:128 * h + 128])
    return jnp.stack(qs), jnp.stack(ks), jnp.stack(vs)


def _mla_pre_specs():
    return [pl.BlockSpec((TM, 256), lambda i: (i, P_CQ // 256)), pl.BlockSpec((TM, 128), lambda i: (i, P_CKV // 128)),
            pl.BlockSpec((TM, 128), lambda i: (i, P_KR // 128)), pl.BlockSpec((TM, 16), lambda i: (i, 0)),
            pl.BlockSpec((TM, 16), lambda i: (i, 0)), _full((1, 256)), _full((1, 128)), _full((256, 512)), _full((128, 512))]


def _head_tile(w):
    return pl.BlockSpec((MLA_HEADS, TM, w), lambda i: (0, i, 0))


def mla_pre_fwd(proj, cos, sin, qg, kvg, wuq, wukv, name):
    t = proj.shape[0]

    def body(cq_ref, ckv_ref, kr_ref, cos_ref, sin_ref, qg_ref, kvg_ref, wuq_ref, wukv_ref, q_ref, k_ref, v_ref):
        q, k, v = _mla_pre(cq_ref[...], ckv_ref[...], kr_ref[...], cos_ref[...], sin_ref[...], qg_ref[...], kvg_ref[...],
                           wuq_ref[...], wukv_ref[...])
        q_ref[...] = q.astype(BF16)
        k_ref[...] = k.astype(BF16)
        v_ref[...] = v.astype(BF16)

    return pl.pallas_call(
        body, name=name,
        out_shape=(jax.ShapeDtypeStruct((MLA_HEADS, t, 128), BF16), jax.ShapeDtypeStruct((MLA_HEADS, t, 128), BF16),
                   jax.ShapeDtypeStruct((MLA_HEADS, t, 64), BF16)),
        grid=(t // TM,), in_specs=_mla_pre_specs(), out_specs=(_head_tile(128), _head_tile(128), _head_tile(64)),
        compiler_params=_cp(("parallel",)),
    )(proj, proj, proj, cos, sin, _row(qg), _row(kvg), wuq, wukv)


def mla_pre_bwd(proj, cos, sin, qg, kvg, wuq, wukv, dq, dk, dv, name):
    t = proj.shape[0]

    def body(cq_ref, ckv_ref, kr_ref, cos_ref, sin_ref, qg_ref, kvg_ref, wuq_ref, wukv_ref, dq_ref, dk_ref, dv_ref,
             dcq_ref, dckv_ref, dkr_ref, dwuq_ref, dwukv_ref, vec_ref):
        @pl.when(pl.program_id(0) == 0)
        def _():
            dwuq_ref[...] = jnp.zeros_like(dwuq_ref)
            dwukv_ref[...] = jnp.zeros_like(dwukv_ref)
            vec_ref[...] = jnp.zeros_like(vec_ref)

        cos, sin = cos_ref[...], sin_ref[...]
        f = lambda cq, ckv, kr, qg, kvg, wuq, wukv: _mla_pre(cq, ckv, kr, cos, sin, qg, kvg, wuq, wukv)
        _, vjp = jax.vjp(f, cq_ref[...], ckv_ref[...], kr_ref[...], qg_ref[...], kvg_ref[...], wuq_ref[...], wukv_ref[...])
        dcq, dckv, dkr, dqg, dkvg, dwuq, dwukv = vjp((dq_ref[...], dk_ref[...], dv_ref[...]))
        dcq_ref[...] = dcq
        dckv_ref[...] = dckv
        dkr_ref[...] = dkr
        dwuq_ref[...] += dwuq
        dwukv_ref[...] += dwukv
        vec_ref[0:1, :] += dqg
        vec_ref[1:2, 0:128] += dkvg

    return pl.pallas_call(
        body, name=name,
        out_shape=(jax.ShapeDtypeStruct((t, 256), F32), jax.ShapeDtypeStruct((t, 128), F32), jax.ShapeDtypeStruct((t, 128), F32),
                   jax.ShapeDtypeStruct((256, 512), F32), jax.ShapeDtypeStruct((128, 512), F32), jax.ShapeDtypeStruct((8, 256), F32)),
        grid=(t // TM,), in_specs=_mla_pre_specs() + [_head_tile(128), _head_tile(128), _head_tile(64)],
        out_specs=(pl.BlockSpec((TM, 256), lambda i: (i, 0)), pl.BlockSpec((TM, 128), lambda i: (i, 0)),
                   pl.BlockSpec((TM, 128), lambda i: (i, 0)), _full((256, 512)), _full((128, 512)), _full((8, 256))),
        compiler_params=_cp(("arbitrary",)),
    )(proj, proj, proj, cos, sin, _row(qg), _row(kvg), wuq, wukv, dq, dk, dv)


def _causal_mask(i, j):
    qpos = i * TA + lax.broadcasted_iota(jnp.int32, (TA, TA), 0)
    kpos = j * TA + lax.broadcasted_iota(jnp.int32, (TA, TA), 1)
    return kpos <= qpos


def mla_flash_fwd(q, k, v, name):
    h, t, _ = q.shape

    def body(q_ref, k_ref, v_ref, o_ref, lse_ref, m_ref, l_ref, acc_ref):
        i = pl.program_id(1)
        m_ref[...] = jnp.full_like(m_ref, NEG)
        l_ref[...] = jnp.zeros_like(l_ref)
        acc_ref[...] = jnp.zeros_like(acc_ref)
        qb = q_ref[0]

        def step(j, carry, diagonal=False):
            rows = pl.ds(pl.multiple_of(j * TA, TA), TA)
            s = _dot(qb, k_ref[0, rows, :], _NT) * MLA_SCALE
            if diagonal:
                s = jnp.where(_causal_mask(i, j), s, NEG)
            m_new = jnp.maximum(m_ref[...], jnp.max(s, axis=-1, keepdims=True))
            p = jnp.exp(s - m_new)
            alpha = jnp.exp(m_ref[...] - m_new)
            l_ref[...] = alpha * l_ref[...] + jnp.sum(p, axis=-1, keepdims=True)
            acc_ref[...] = alpha * acc_ref[...] + _dot(p, v_ref[0, rows, :], _NN)
            m_ref[...] = m_new
            return carry

        lax.fori_loop(0, i, step, 0)
        step(i, 0, diagonal=True)
        o_ref[0] = acc_ref[...] / l_ref[...]
        lse_ref[0] = m_ref[...] + jnp.log(l_ref[...])

    return pl.pallas_call(
        body, name=name,
        out_shape=(jax.ShapeDtypeStruct((h, t, 64), F32), jax.ShapeDtypeStruct((h, t, 1), F32)), grid=(h, t // TA),
        in_specs=[pl.BlockSpec((1, TA, 128), lambda hh, i: (hh, i, 0)), pl.BlockSpec((1, t, 128), lambda hh, i: (hh, 0, 0)),
                  pl.BlockSpec((1, t, 64), lambda hh, i: (hh, 0, 0))],
        out_specs=(pl.BlockSpec((1, TA, 64), lambda hh, i: (hh, i, 0)), pl.BlockSpec((1, TA, 1), lambda hh, i: (hh, i, 0))),
        scratch_shapes=[pltpu.VMEM((TA, 1), F32), pltpu.VMEM((TA, 1), F32), pltpu.VMEM((TA, 64), F32)],
        compiler_params=_cp(("parallel", "parallel")),
    )(q, k, v)


def mla_flash_bwd(q, k, v, o, lse, do, name):
    h, t, _ = q.shape
    nb = t // TA

    def body(q_ref, k_ref, v_ref, o_ref, lse_ref, do_ref, dq_ref, dk_ref, dv_ref):
        j = pl.program_id(1)

        @pl.when(j == 0)
        def _():
            dq_ref[...] = jnp.zeros_like(dq_ref)

        dk_ref[...] = jnp.zeros_like(dk_ref)
        dv_ref[...] = jnp.zeros_like(dv_ref)
        kb, vb = k_ref[0], v_ref[0]

        def step(i, carry, diagonal=False):
            rows = pl.ds(pl.multiple_of(i * TA, TA), TA)
            qb, dob = q_ref[0, rows, :], do_ref[0, rows, :]
            s = _dot(qb, kb, _NT) * MLA_SCALE
            p = jnp.exp(s - lse_ref[0, rows, :])
            if diagonal:
                p = jnp.where(_causal_mask(i, j), p, 0.0)
            delta = jnp.sum(dob * o_ref[0, rows, :], axis=-1, keepdims=True)
            dv_ref[0] += _dot(p, dob, _TN)
            ds = p * (_dot(dob, vb, _NT) - delta) * MLA_SCALE
            dk_ref[0] += _dot(ds, qb, _TN)
            dq_ref[0, rows, :] += _dot(ds, kb, _NN)
            return carry

        step(j, 0, diagonal=True)
        lax.fori_loop(j + 1, nb, step, 0)

    def whole(w):
        return pl.BlockSpec((1, t, w), lambda hh, j: (hh, 0, 0))

    def blk(w):
        return pl.BlockSpec((1, TA, w), lambda hh, j: (hh, j, 0))

    return pl.pallas_call(
        body, name=name,
        out_shape=(jax.ShapeDtypeStruct((h, t, 128), F32), jax.ShapeDtypeStruct((h, t, 128), F32), jax.ShapeDtypeStruct((h, t, 64), F32)),
        grid=(h, nb), in_specs=[whole(128), blk(128), blk(64), whole(64), whole(1), whole(64)],
        out_specs=(whole(128), blk(128), blk(64)), compiler_params=_cp(("parallel", "arbitrary")),
    )(q, k, v, o, lse, do)


SWA_SCALE = 1.0 / 8.0


def _swa_block(q, kp, kc, vp, vc, sinks, has_prev):
    k2 = jnp.concatenate([kp, kc], axis=0)
    v2 = jnp.concatenate([vp, vc], axis=0)
    rel = Q + lax.broadcasted_iota(jnp.int32, (Q, 2 * Q), 0) - lax.broadcasted_iota(jnp.int32, (Q, 2 * Q), 1)
    valid = (rel >= 0) & (rel < Q) & ((lax.broadcasted_iota(jnp.int32, (Q, 2 * Q), 1) >= Q) | has_prev)
    outs = []
    for h in range(4):
        g = h // 2
        s = mm_nt(q[:, 64 * h:64 * h + 64], k2[:, 64 * g:64 * g + 64]) * SWA_SCALE
        s = jnp.where(valid, s, NEG)
        sink = sinks[:, h:h + 1]
        m = jnp.maximum(jnp.max(s, axis=-1, keepdims=True), sink)
        e = jnp.exp(s - m)
        p = e / (jnp.sum(e, axis=-1, keepdims=True) + jnp.exp(sink - m))
        outs.append(mm(p, v2[:, 64 * g:64 * g + 64]))
    return jnp.concatenate(outs, axis=1)


def _swa_specs(blk_of):
    def prev(i):
        return jnp.maximum(blk_of(i) - 1, 0)

    return [pl.BlockSpec((Q, 256), lambda i: (blk_of(i), P_SQ // 256)),
            pl.BlockSpec((Q, 128), lambda i: (prev(i), P_SK // 128)), pl.BlockSpec((Q, 128), lambda i: (blk_of(i), P_SK // 128)),
            pl.BlockSpec((Q, 128), lambda i: (prev(i), P_SV // 128)), pl.BlockSpec((Q, 128), lambda i: (blk_of(i), P_SV // 128)),
            _full((1, 128))]


def swa_fwd(proj, sinks, name):
    t = proj.shape[0]

    def body(q_ref, kp_ref, kc_ref, vp_ref, vc_ref, s_ref, o_ref):
        o_ref[...] = _swa_block(q_ref[...], kp_ref[...], kc_ref[...], vp_ref[...], vc_ref[...], s_ref[...], pl.program_id(0) > 0)

    return pl.pallas_call(
        body, name=name, out_shape=jax.ShapeDtypeStruct((t, 256), F32), grid=(t // Q,), in_specs=_swa_specs(lambda i: i),
        out_specs=pl.BlockSpec((Q, 256), lambda i: (i, 0)), compiler_params=_cp(("parallel",)),
    )(proj, proj, proj, proj, proj, _pad_lanes(sinks))


def swa_bwd(proj, sinks, do, name):
    t = proj.shape[0]
    nb = t // Q

    def body(q_ref, kp_ref, kc_ref, vp_ref, vc_ref, s_ref, do_ref, dq_ref, dk_ref, dv_ref, ds_ref, ck_ref, cv_ref):
        step = pl.program_id(0)

        @pl.when(step == 0)
        def _():
            ck_ref[...] = jnp.zeros_like(ck_ref)
            cv_ref[...] = jnp.zeros_like(cv_ref)
            ds_ref[...] = jnp.zeros_like(ds_ref)

        has_prev = step < nb - 1
        f = lambda q, kp, kc, vp, vc, s: _swa_block(q, kp, kc, vp, vc, s, has_prev)
        _, vjp = jax.vjp(f, q_ref[...], kp_ref[...], kc_ref[...], vp_ref[...], vc_ref[...], s_ref[...])
        dq, dkp, dkc, dvp, dvc, dsk = vjp(do_ref[...])
        dq_ref[...] = dq
        dk_ref[...] = dkc + ck_ref[...]
        dv_ref[...] = dvc + cv_ref[...]
        ck_ref[...] = dkp
        cv_ref[...] = dvp
        ds_ref[0:1, :] += dsk

    def rev(i):
        return nb - 1 - i

    return pl.pallas_call(
        body, name=name,
        out_shape=(jax.ShapeDtypeStruct((t, 256), F32), jax.ShapeDtypeStruct((t, 128), F32), jax.ShapeDtypeStruct((t, 128), F32),
                   jax.ShapeDtypeStruct((8, 128), F32)),
        grid=(nb,), in_specs=_swa_specs(rev) + [pl.BlockSpec((Q, 256), lambda i: (rev(i), 0))],
        out_specs=(pl.BlockSpec((Q, 256), lambda i: (rev(i), 0)), pl.BlockSpec((Q, 128), lambda i: (rev(i), 0)),
                   pl.BlockSpec((Q, 128), lambda i: (rev(i), 0)), _full((8, 128))),
        scratch_shapes=[pltpu.VMEM((Q, 128), F32), pltpu.VMEM((Q, 128), F32)], compiler_params=_cp(("arbitrary",)),
    )(proj, proj, proj, proj, proj, _pad_lanes(sinks), do)


def _loss_tile(x, g, tgt):
    err = jnp.square(_rms(x, g) - tgt)
    return 0.5 * jnp.sum(jnp.mean(err, axis=-1, keepdims=True), axis=0, keepdims=True)


def loss_fwd_bwd(x, g, tgt, name):
    t = x.shape[0]

    def body(x_ref, g_ref, t_ref, loss_ref, dx_ref, dg_ref):
        @pl.when(pl.program_id(0) == 0)
        def _():
            loss_ref[...] = jnp.zeros_like(loss_ref)
            dg_ref[...] = jnp.zeros_like(dg_ref)

        tgt = t_ref[...]
        val, vjp = jax.vjp(lambda x, g: _loss_tile(x, g, tgt), x_ref[...], g_ref[...])
        dx, dg = vjp(jnp.ones((1, 1), F32))
        dx_ref[...] = dx
        dg_ref[0:1, :] += dg
        loss_ref[...] += val

    tile = pl.BlockSpec((TM, D), lambda i: (i, 0))
    return pl.pallas_call(
        body, name=name,
        out_shape=(jax.ShapeDtypeStruct((8, 128), F32), jax.ShapeDtypeStruct((t, D), F32), jax.ShapeDtypeStruct((8, D), F32)),
        grid=(t // TM,), in_specs=[tile, _full((1, D)), tile], out_specs=(_full((8, 128)), tile, _full((8, D))),
        compiler_params=_cp(("arbitrary",)),
    )(x, _row(g), tgt)


def adamw(w, g, m, v, name):
    shape = w.shape
    cols = shape[-1] if w.ndim > 1 else shape[0]
    w2, g2, m2, v2 = (a.reshape(-1, cols) for a in (w, g, m, v))
    rows = w2.shape[0]
    br = _blk(rows, max(8, (1 << 19) // cols), 8)

    def body(w_ref, g_ref, m_ref, v_ref, d_ref, nm_ref, nv_ref):
        gg = g_ref[...]
        nm = ADAM_B1 * m_ref[...] + (1.0 - ADAM_B1) * gg
        nv = ADAM_B2 * v_ref[...] + (1.0 - ADAM_B2) * jnp.square(gg)
        m_hat = nm / (1.0 - ADAM_B1 ** ADAM_STEP)
        v_hat = nv / (1.0 - ADAM_B2 ** ADAM_STEP)
        d_ref[...] = -ADAM_LR * (m_hat / (jnp.sqrt(v_hat) + ADAM_EPS) + ADAM_WD * w_ref[...])
        nm_ref[...] = nm
        nv_ref[...] = nv

    spec = pl.BlockSpec((br, cols), lambda i: (i, 0))
    out = jax.ShapeDtypeStruct((rows, cols), F32)
    res = pl.pallas_call(body, name=name, out_shape=(out, out, out), grid=(rows // br,), in_specs=[spec] * 4,
                         out_specs=(spec, spec, spec), compiler_params=_cp(("parallel",)))(w2, g2, m2, v2)
    return tuple(r.reshape(shape) for r in res)


MESH = pl.DeviceIdType.MESH
CHIP_FLIPS = ((1, 0), (0, 1), (1, 1))


def _place():
    return lax.axis_index("x"), lax.axis_index("y"), lax.axis_index("c")


def allgather8(blk, name, in_vmem):
    space = pltpu.VMEM if in_vmem else pl.ANY

    def body(x_ref, out_ref, send_sems, recv_sems, local_sem):
        x, y, c = _place()
        me, sibling = (x, y, c), (x, y, 1 - c)
        chips = [(x ^ fx, y ^ fy) for fx, fy in CHIP_FLIPS]

        def slot(px, py, pc):
            return out_ref.at[4 * px + 2 * py + pc]

        def copy(k, block, to, src=None):
            return pltpu.make_async_remote_copy(
                src_ref=slot(*block) if src is None else src, dst_ref=slot(*block),
                send_sem=send_sems.at[k], recv_sem=recv_sems.at[k], device_id=to, device_id_type=MESH)

        mine = pltpu.make_async_copy(x_ref, slot(*me), local_sem)
        mine.start()
        first = [copy(0, me, sibling, src=x_ref)]
        first += [copy(1 + j, me, (*chip, c), src=x_ref) for j, chip in enumerate(chips)]
        for cp in first:
            cp.start()
        passed = [copy(4 + j, (*chip, c), sibling) for j, chip in enumerate(chips)]
        for j, chip in enumerate(chips):
            copy(1 + j, (*chip, c), me).wait_recv()
            passed[j].start()
        copy(0, sibling, me).wait_recv()
        for j, chip in enumerate(chips):
            copy(4 + j, (*chip, 1 - c), me).wait_recv()
        for cp in first + passed:
            cp.wait_send()
        mine.wait()

    return pl.pallas_call(
        body, name=name, out_shape=jax.ShapeDtypeStruct((N_DEV,) + blk.shape, blk.dtype),
        in_specs=[pl.BlockSpec(memory_space=space)], out_specs=pl.BlockSpec(memory_space=space),
        scratch_shapes=[pltpu.SemaphoreType.DMA((7,)), pltpu.SemaphoreType.DMA((7,)), pltpu.SemaphoreType.DMA],
        compiler_params=pltpu.CompilerParams(vmem_limit_bytes=VMEM_LIMIT),
    )(blk)


def flip_exchange(src, plan, n_out, name):
    def body(x_ref, out_ref, send_sems, recv_sems):
        x, y, c = _place()
        copies = []
        for k, (flip, src_index, dst_slot) in enumerate(plan):
            s, d = x_ref.at[src_index(x, y, c)], out_ref.at[dst_slot(x, y, c)]
            if flip is None:
                copies.append(pltpu.make_async_copy(s, d, send_sems.at[k]))
            else:
                copies.append(pltpu.make_async_remote_copy(
                    src_ref=s, dst_ref=d, send_sem=send_sems.at[k], recv_sem=recv_sems.at[k],
                    device_id=(x ^ flip[0], y ^ flip[1], c ^ flip[2]), device_id_type=MESH))
        for cp in copies:
            cp.start()
        for (flip, _, _), cp in zip(plan, copies):
            if flip is None:
                cp.wait()
            else:
                cp.wait_recv()
                cp.wait_send()

    n = len(plan)
    return pl.pallas_call(
        body, name=name, out_shape=jax.ShapeDtypeStruct((n_out,) + src.shape[1:], src.dtype),
        in_specs=[pl.BlockSpec(memory_space=pl.ANY)], out_specs=pl.BlockSpec(memory_space=pl.ANY),
        scratch_shapes=[pltpu.SemaphoreType.DMA((n,)), pltpu.SemaphoreType.DMA((n,))],
    )(src)


ROWS_ADD = 2048


def add_pairs(a, b, out_dtype, name):
    n = a.shape[0]
    br = _blk(n, ROWS_ADD, 16)

    def body(a_ref, b_ref, o_ref):
        o_ref[...] = (a_ref[...].astype(F32) + b_ref[...].astype(F32)).astype(o_ref.dtype)

    spec = pl.BlockSpec((br, 128), lambda i: (i, 0))
    return pl.pallas_call(body, name=name, out_shape=jax.ShapeDtypeStruct((n, 128), out_dtype), grid=(n // br,),
                          in_specs=[spec, spec], out_specs=spec, compiler_params=_cp(("parallel",)))(a, b)


def add_slots(own, others, name):
    n = own.shape[0]
    ns = others.shape[0]
    br = _blk(n, ROWS_ADD, 16)

    def body(a_ref, b_ref, o_ref):
        acc = a_ref[...].astype(F32)
        for s in range(ns):
            acc = acc + b_ref[s].astype(F32)
        o_ref[...] = acc

    return pl.pallas_call(body, name=name, out_shape=jax.ShapeDtypeStruct((n, 128), F32), grid=(n // br,),
                          in_specs=[pl.BlockSpec((br, 128), lambda i: (i, 0)), pl.BlockSpec((ns, br, 128), lambda i: (0, i, 0))],
                          out_specs=pl.BlockSpec((br, 128), lambda i: (i, 0)), compiler_params=_cp(("parallel",)))(own, others)


def sum8(g, name):
    r = g.shape[1]

    def body(g_ref, o_ref):
        acc = g_ref[0]
        for s in range(1, N_DEV):
            acc = acc + g_ref[s]
        o_ref[...] = acc

    return pl.pallas_call(body, name=name, out_shape=jax.ShapeDtypeStruct((r, 128), F32))(g)


def ada_mod(c_all, ada_w, ada_b_cols, name):
    def body(c_ref, w_ref, b_ref, o_ref):
        o_ref[0] = mm(_silu(c_ref[...]), w_ref[0]) + b_ref[0]

    n = ada_w.shape[2]
    return pl.pallas_call(
        body, name=name, out_shape=jax.ShapeDtypeStruct((DEPTH, N_DEV, n), F32), grid=(DEPTH,),
        in_specs=[pl.BlockSpec((N_DEV, D), lambda l: (0, 0)), pl.BlockSpec((1, D, n), lambda l: (l, 0, 0)),
                  pl.BlockSpec((1, 1, n), lambda l: (l, 0, 0))],
        out_specs=pl.BlockSpec((1, N_DEV, n), lambda l: (l, 0, 0)), compiler_params=_cp(("parallel",)),
    )(c_all, ada_w, ada_b_cols.reshape(DEPTH, 1, n))


def ada_grad(c_all, dmod_cols, name):
    def body(c_ref, d_ref, o_ref):
        o_ref[0] = mm_tn(_silu(c_ref[...]), d_ref[0])

    n = dmod_cols.shape[2]
    return pl.pallas_call(
        body, name=name, out_shape=jax.ShapeDtypeStruct((DEPTH, D, n), F32), grid=(DEPTH,),
        in_specs=[pl.BlockSpec((N_DEV, D), lambda l: (0, 0)), pl.BlockSpec((1, N_DEV, n), lambda l: (l, 0, 0))],
        out_specs=pl.BlockSpec((1, D, n), lambda l: (l, 0, 0)), compiler_params=_cp(("parallel",)),
    )(c_all, dmod_cols)


def pack_w_in(w):
    out = jnp.zeros(w.shape[:-1] + (NP,), w.dtype)
    for p_off, o_off, width in _PACK:
        out = out.at[..., p_off:p_off + width].set(w[..., o_off:o_off + width])
    return out


def unpack_w_in(w):
    return jnp.concatenate([w[..., p_off:p_off + width] for p_off, _, width in _PACK], axis=-1)


def pack_w_uq(w):
    return jnp.pad(w.reshape(w.shape[:-1] + (MLA_HEADS, MLA_QK)), [(0, 0)] * (w.ndim - 1) + [(0, 0), (0, 32)]).reshape(w.shape[:-1] + (512,))


def unpack_w_uq(w):
    return w.reshape(w.shape[:-1] + (MLA_HEADS, 128))[..., :MLA_QK].reshape(w.shape[:-1] + (MLA_HEADS * MLA_QK,))


def layer_fwd(x, mod, w, cos, sin, tag):
    h1 = modnorm_fwd(x, w["norm1_g"], mod[0], mod[1], tag + "norm1")
    proj = matmul(h1, w["w_in"], "nn", F32, tag + "w_in")
    y_ssd, hs = ssd_fwd(proj, w["ssd_conv_w"], w["ssd_conv_b"], w["ssd_dt_bias"], w["ssd_a_log"], w["ssd_d"], w["ssd_norm_g"], tag + "ssd")
    q, k, v = mla_pre_fwd(proj, cos, sin, w["mla_q_norm_g"], w["mla_kv_norm_g"], w["mla_w_uq"], w["mla_w_ukv"], tag + "mla_pre")
    o, lse = mla_flash_fwd(q, k, v, tag + "mla_attn")
    y_swa = swa_fwd(proj, w["swa_sinks"], tag + "swa")
    t = x.shape[0]
    ycat = jnp.concatenate([y_ssd, jnp.transpose(o, (1, 0, 2)).reshape(t, 256), y_swa], axis=1).astype(BF16)
    y = matmul(ycat, w["w_out"], "nn", F32, tag + "w_out")
    xm = resid_fwd(x, y, mod[2], tag + "res1")
    h2 = modnorm_fwd(xm, w["norm2_g"], mod[3], mod[4], tag + "norm2")
    u0 = matmul(h2, w["ffn_w_up"], "nn", F32, tag + "w_up")
    gact = convglu_fwd(u0, w["ffn_conv_w"], w["ffn_conv_b"], tag + "glu")
    yd = matmul(gact, w["ffn_w_down"], "nn", F32, tag + "w_down")
    xo = resid_fwd(xm, yd, mod[5], tag + "res2")
    return xo, dict(x=x, h1=h1, proj=proj, hs=hs, q=q, k=k, v=v, o=o, lse=lse, ycat=ycat, y=y, xm=xm, h2=h2, u0=u0, gact=gact, yd=yd)


def layer_bwd(dxo, s, mod, w, cos, sin, tag):
    t = dxo.shape[0]
    g = {}
    dyd, dg2 = resid_bwd(dxo, s["yd"], mod[5], tag + "res2_b")
    dgact = matmul(dyd, w["ffn_w_down"], "nt", BF16, tag + "w_down_dx")
    g["ffn_w_down"] = matmul(s["gact"], dyd, "tn", F32, tag + "w_down_dw")
    du0, g["ffn_conv_w"], dcb = convglu_bwd(s["u0"], w["ffn_conv_w"], w["ffn_conv_b"], dgact, tag + "glu_b")
    g["ffn_conv_b"] = dcb[0]
    dh2 = matmul(du0, w["ffn_w_up"], "nt", F32, tag + "w_up_dx")
    g["ffn_w_up"] = matmul(s["h2"], du0, "tn", F32, tag + "w_up_dw")
    dxm, sums2 = modnorm_bwd(s["xm"], w["norm2_g"], mod[3], mod[4], dh2, dxo, tag + "norm2_b")
    g["norm2_g"] = sums2[0]
    dy, dg1 = resid_bwd(dxm, s["y"], mod[2], tag + "res1_b")
    dycat = matmul(dy, w["w_out"], "nt", F32, tag + "w_out_dx")
    g["w_out"] = matmul(s["ycat"], dy, "tn", F32, tag + "w_out_dw")
    proj = s["proj"]
    dz, dxbc, ddt, g["ssd_conv_w"], vec = ssd_bwd(proj, s["hs"], dycat[:, 0:512], w["ssd_conv_w"], w["ssd_conv_b"], w["ssd_dt_bias"],
                                                 w["ssd_a_log"], w["ssd_d"], w["ssd_norm_g"], tag + "ssd_b")
    g["ssd_conv_b"], g["ssd_dt_bias"], g["ssd_a_log"], g["ssd_d"], g["ssd_norm_g"] = vec[0], vec[1, :8], vec[2, :8], vec[3, :8], vec[4, :512]
    do = jnp.transpose(dycat[:, 512:768].reshape(t, MLA_HEADS, 64), (1, 0, 2))
    dq, dk, dv = mla_flash_bwd(s["q"], s["k"], s["v"], s["o"], s["lse"], do, tag + "mla_attn_b")
    dcq, dckv, dkr, g["mla_w_uq"], g["mla_w_ukv"], mvec = mla_pre_bwd(proj, cos, sin, w["mla_q_norm_g"], w["mla_kv_norm_g"],
                                                                    w["mla_w_uq"], w["mla_w_ukv"], dq, dk, dv, tag + "mla_pre_b")
    g["mla_q_norm_g"], g["mla_kv_norm_g"] = mvec[0], mvec[1, :128]
    dsq, dsk, dsv, dsink = swa_bwd(proj, w["swa_sinks"], dycat[:, 768:1024], tag + "swa_b")
    g["swa_sinks"] = dsink[0, :4]
    dproj = jnp.concatenate([dxbc, dz, dcq, dsq, dckv, ddt, dkr, dsk, dsv], axis=1).astype(BF16)
    dh1 = matmul(dproj, w["w_in"], "nt", F32, tag + "w_in_dx")
    g["w_in"] = matmul(s["h1"], dproj, "tn", F32, tag + "w_in_dw")
    dx, sums1 = modnorm_bwd(s["x"], w["norm1_g"], mod[0], mod[1], dh1, dxm, tag + "norm1_b")
    g["norm1_g"] = sums1[0]
    dmod = jnp.stack([sums1[1], sums1[2], dg1[0], sums2[1], sums2[2], dg2[0]])
    return dx, dmod, g


def local_step(x, tgt, mods, ws, final_norm_g, cos, sin):
    saved = []
    for l in range(len(ws)):
        x, s = layer_fwd(x, mods[l], ws[l], cos, sin, f"l{l}_")
        saved.append(s)
    loss, dx, dfg = loss_fwd_bwd(x, final_norm_g, tgt, "loss")
    dmods, grads = [None] * len(ws), [None] * len(ws)
    for l in reversed(range(len(ws))):
        dx, dmods[l], grads[l] = layer_bwd(dx, saved[l], mods[l], ws[l], cos, sin, f"l{l}_")
    return loss, dx, dfg[0], jnp.stack(dmods), grads


WEIGHTS = ("ada_w", "ada_b", "norm1_g", "norm2_g", "w_in", "ssd_conv_w", "ssd_conv_b", "ssd_dt_bias", "ssd_a_log", "ssd_d",
           "ssd_norm_g", "mla_q_norm_g", "mla_w_uq", "mla_kv_norm_g", "mla_w_ukv", "swa_sinks", "w_out", "ffn_w_up",
           "ffn_conv_w", "ffn_conv_b", "ffn_w_down", "final_norm_g")
BIG = (("w_in", 2), ("w_out", 1), ("ffn_w_up", 2), ("ffn_w_down", 1), ("mla_w_uq", 2), ("mla_w_ukv", 2))
SMALL = (("dmod", 6 * D), ("norm1_g", D), ("norm2_g", D), ("ssd_conv_w", 4 * SSD_XBC), ("ssd_conv_b", SSD_XBC), ("ssd_dt_bias", 128),
         ("ssd_a_log", 128), ("ssd_d", 128), ("ssd_norm_g", SSD_INNER), ("mla_q_norm_g", 256), ("mla_kv_norm_g", 128),
         ("swa_sinks", 128), ("ffn_conv_w", 3 * 2 * D_FF), ("ffn_conv_b", 2 * D_FF))
SMALL_LAYER = sum(n for _, n in SMALL)
SMALL_SHAPES = {"norm1_g": (D,), "norm2_g": (D,), "ssd_conv_w": (4, SSD_XBC), "ssd_conv_b": (SSD_XBC,), "ssd_dt_bias": (8,),
                "ssd_a_log": (8,), "ssd_d": (8,), "ssd_norm_g": (SSD_INNER,), "mla_q_norm_g": (256,), "mla_kv_norm_g": (128,),
                "swa_sinks": (4,), "ffn_conv_w": (3, 2 * D_FF), "ffn_conv_b": (2 * D_FF,)}


def _lanes(v, n):
    v = v.reshape(-1)
    return jnp.pad(v, (0, n - v.shape[0]))


def _tile_rows(flat):
    n = -(-flat.shape[0] // 1024) * 1024
    return jnp.pad(flat, (0, n - flat.shape[0])).reshape(-1, 128)


def _rope_tables(positions):
    inv_freq = 10000.0 ** (-jnp.arange(0, 32, 2, dtype=F32) / 32)
    ang = positions.astype(F32).reshape(-1, 1) * inv_freq
    return jnp.cos(ang), jnp.sin(ang)


def _gather_big(shards, c):
    rb = sum(s.size for s in shards) // 256
    flat = jnp.concatenate([s.astype(BF16).reshape(-1, 128) for s in shards], axis=0).reshape(2, rb, 128)
    got = allgather8(lax.dynamic_index_in_dim(flat, c, 0, keepdims=False), "ag_weights", False).reshape(N_CHIP, 2 * rb, 128)
    fulls, off = [], 0
    for (_, axis), s in zip(BIG, shards):
        rows = s.size // 128
        fulls.append(jnp.concatenate([got[k, off:off + rows].reshape(s.shape) for k in range(N_CHIP)], axis=axis))
        off += rows
    return fulls


def _reduce_big(grads, x, y, c):
    chips = []
    for k in range(N_CHIP):
        parts = []
        for (_, axis), g in zip(BIG, grads):
            n = g.shape[axis] // N_CHIP
            parts.append(lax.slice_in_dim(g, k * n, (k + 1) * n, axis=axis).astype(BF16).reshape(-1, 128))
        chips.append(jnp.concatenate(parts, axis=0))
    rb = chips[0].shape[0] // 2
    halves = jnp.stack(chips).reshape(N_CHIP, 2, rb, 128).transpose(1, 0, 2, 3).reshape(2, N_CHIP * rb, 128)
    theirs = flip_exchange(halves, [((0, 0, 1), lambda x, y, c: 1 - c, lambda x, y, c: 0)], 1, "rs_sibling")
    mine = lax.dynamic_index_in_dim(halves, c, 0, keepdims=False)
    chip_sum = add_pairs(mine, theirs[0], BF16, "rs_add_sibling").reshape(N_CHIP, rb, 128)
    plan = [((fx, fy, 0), (lambda x, y, c, fx=fx, fy=fy: 2 * (x ^ fx) + (y ^ fy)), (lambda x, y, c, s=s: s))
            for s, (fx, fy) in enumerate(CHIP_FLIPS)]
    others = flip_exchange(chip_sum, plan, len(CHIP_FLIPS), "rs_chips")
    own = lax.dynamic_index_in_dim(chip_sum, 2 * x + y, 0, keepdims=False)
    half = add_slots(own, others, "rs_add_chips")
    both = flip_exchange(half[None], [(None, lambda x, y, c: 0, lambda x, y, c: c),
                                      ((0, 0, 1), lambda x, y, c: 0, lambda x, y, c: c)], 2, "rs_share")
    flat = both.reshape(2 * rb, 128)
    out, off = [], 0
    for (_, axis), g in zip(BIG, grads):
        shape = list(g.shape)
        shape[axis] //= N_CHIP
        rows = math.prod(shape) // 128
        out.append(flat[off:off + rows].reshape(shape))
        off += rows
    return out


def kernel(x, c, positions, ada_w, ada_b, norm1_g, norm2_g, w_in, ssd_conv_w, ssd_conv_b, ssd_dt_bias, ssd_a_log, ssd_d, ssd_norm_g, mla_q_norm_g, mla_w_uq, mla_kv_norm_g, mla_w_ukv, swa_sinks, w_out, ffn_w_up, ffn_conv_w, ffn_conv_b, ffn_w_down, final_norm_g, loss_target, m_ada_w, m_ada_b, m_norm1_g, m_norm2_g, m_w_in, m_ssd_conv_w, m_ssd_conv_b, m_ssd_dt_bias, m_ssd_a_log, m_ssd_d, m_ssd_norm_g, m_mla_q_norm_g, m_mla_w_uq, m_mla_kv_norm_g, m_mla_w_ukv, m_swa_sinks, m_w_out, m_ffn_w_up, m_ffn_conv_w, m_ffn_conv_b, m_ffn_w_down, m_final_norm_g, v_ada_w, v_ada_b, v_norm1_g, v_norm2_g, v_w_in, v_ssd_conv_w, v_ssd_conv_b, v_ssd_dt_bias, v_ssd_a_log, v_ssd_d, v_ssd_norm_g, v_mla_q_norm_g, v_mla_w_uq, v_mla_kv_norm_g, v_mla_w_ukv, v_swa_sinks, v_w_out, v_ffn_w_up, v_ffn_conv_w, v_ffn_conv_b, v_ffn_w_down, v_final_norm_g):
    args = locals()
    wt = {n: args[n] for n in WEIGHTS}
    mx, my, mc = _place()
    chip = 2 * mx + my
    dev = 2 * chip + mc
    n_ada = ada_w.shape[2]

    pack = _tile_rows(jnp.concatenate([c.reshape(-1), ssd_conv_w.reshape(-1), ffn_conv_w.reshape(-1)]))
    got = allgather8(pack, "ag_small_in", True).reshape(N_DEV, -1)
    c_all = got[:, :D]
    per_chip = got[0::2]
    n_scw = ssd_conv_w.size
    ssd_cw = jnp.concatenate([per_chip[k, D:D + n_scw].reshape(ssd_conv_w.shape) for k in range(N_CHIP)], axis=2)
    n_fcw = ffn_conv_w.size
    ffn_cw = jnp.concatenate([per_chip[k, D + n_scw:D + n_scw + n_fcw].reshape(ffn_conv_w.shape) for k in range(N_CHIP)], axis=2)

    ada_b_cols = lax.dynamic_slice_in_dim(ada_b, chip * n_ada, n_ada, axis=1)
    mod_cols = ada_mod(c_all, ada_w, ada_b_cols, "ada_mod")
    mod_all = allgather8(mod_cols.reshape(-1, 128), "ag_mod", True)[0::2].reshape(N_CHIP, DEPTH, N_DEV, n_ada)
    mods = lax.dynamic_index_in_dim(mod_all, dev, 2, keepdims=False).transpose(1, 0, 2).reshape(DEPTH, 6, D)

    full = dict(zip([n for n, _ in BIG], _gather_big([wt[n] for n, _ in BIG], mc)))
    full["w_in"] = pack_w_in(full["w_in"])
    full["mla_w_uq"] = pack_w_uq(full["mla_w_uq"])
    ws = []
    for l in range(DEPTH):
        w = {n: full[n][l] for n, _ in BIG}
        w.update(ssd_conv_w=ssd_cw[l], ffn_conv_w=ffn_cw[l])
        for n in ("norm1_g", "norm2_g", "ssd_conv_b", "ssd_dt_bias", "ssd_a_log", "ssd_d", "ssd_norm_g", "mla_q_norm_g",
                  "mla_kv_norm_g", "swa_sinks", "ffn_conv_b"):
            w[n] = wt[n][l]
        ws.append(w)

    cos, sin = _rope_tables(positions)
    t = x.shape[1]
    loss8, dx, dfg, dmods, lg = local_step(x.reshape(t, D), loss_target.reshape(t, D), mods, ws, final_norm_g, cos, sin)
    loss = lax.psum(loss8[0, 0], ("x", "y", "c"))

    rows = []
    for l in range(DEPTH):
        for name, n in SMALL:
            rows.append(_lanes(dmods[l] if name == "dmod" else lg[l][name], n))
    rows.append(dfg)
    small = allgather8(_tile_rows(jnp.concatenate(rows)), "ag_small_grads", True)
    total = sum8(small, "sum_small_grads").reshape(-1)
    grads = {}
    per_layer = {name: [] for name, _ in SMALL}
    for l in range(DEPTH):
        off = l * SMALL_LAYER
        for name, n in SMALL:
            per_layer[name].append(total[off:off + n])
            off += n
    grads["ada_b"] = jnp.stack(per_layer["dmod"])
    for name, shape in SMALL_SHAPES.items():
        grads[name] = jnp.stack([v[:math.prod(shape)].reshape(shape) for v in per_layer[name]])
    grads["final_norm_g"] = total[DEPTH * SMALL_LAYER:DEPTH * SMALL_LAYER + D]
    for name in ("ssd_conv_w", "ffn_conv_w"):
        n = grads[name].shape[2] // N_CHIP
        grads[name] = lax.dynamic_slice_in_dim(grads[name], chip * n, n, axis=2)
    dmod_all = small.reshape(N_DEV, -1)[:, :DEPTH * SMALL_LAYER].reshape(N_DEV, DEPTH, SMALL_LAYER)[:, :, :6 * D]
    dmod_cols = lax.dynamic_slice_in_dim(dmod_all, chip * n_ada, n_ada, axis=2).transpose(1, 0, 2)
    grads["ada_w"] = ada_grad(c_all, dmod_cols, "ada_grad")

    stacked = []
    for name, _ in BIG:
        g = jnp.stack([lg[l][name] for l in range(DEPTH)])
        if name == "w_in":
            g = unpack_w_in(g)
        if name == "mla_w_uq":
            g = unpack_w_uq(g)
        stacked.append(g)
    for (name, _), g in zip(BIG, _reduce_big(stacked, mx, my, mc)):
        grads[name] = g

    deltas, new_m, new_v = {}, {}, {}
    for n in WEIGHTS:
        deltas[n], new_m[n], new_v[n] = adamw(wt[n], grads[n], args["m_" + n], args["v_" + n], "adamw_" + n)
    return (loss, dx.reshape(x.shape), *[grads[n] for n in WEIGHTS], *[deltas[n] for n in WEIGHTS],
            *[new_m[n] for n in WEIGHTS], *[new_v[n] for n in WEIGHTS])
```

```python
import functools
import math

import jax
import jax.numpy as jnp
from jax import lax
from jax.experimental import pallas as pl
from jax.experimental.pallas import tpu as pltpu

F32 = jnp.float32
BF16 = jnp.bfloat16
MXU_DTYPE = BF16

D = 1024
DEPTH = 4
EPS = 1e-6
N_DEV = 8
N_CHIP = 4

SSD_HEADS = 8
SSD_INNER = 512
SSD_STATE = 128
SSD_XBC = 1024
Q = 128
MLA_HEADS = 4
MLA_QK = 96
D_FF = 2816
D_IN = 2472

P_XBC, P_Z, P_CQ, P_SQ, P_CKV, P_DT, P_KR, P_SK, P_SV = 0, 1024, 1536, 1792, 2048, 2176, 2304, 2432, 2560
NP = 2688
_PACK = ((P_Z, 0, 512), (P_XBC, 512, 1024), (P_DT, 1536, 8), (P_CQ, 1544, 256), (P_CKV, 1800, 128),
         (P_KR, 1928, 32), (P_SQ, 1960, 256), (P_SK, 2216, 128), (P_SV, 2344, 128))

ADAM_LR, ADAM_B1, ADAM_B2, ADAM_EPS, ADAM_WD, ADAM_STEP = 0.001, 0.9, 0.999, 1e-08, 0.01, 10

VMEM_LIMIT = 56 * 1024 * 1024
NEG = -1e30


def _cp(sem=None):
    return pltpu.CompilerParams(dimension_semantics=sem, vmem_limit_bytes=VMEM_LIMIT)


def _dot(a, b, dims):
    return lax.dot_general(a.astype(MXU_DTYPE), b.astype(MXU_DTYPE), (dims, ((), ())), preferred_element_type=F32)


_NN = ((1,), (0,))
_NT = ((1,), (1,))
_TN = ((0,), (0,))


@jax.custom_vjp
def mm(a, b):
    return _dot(a, b, _NN)


mm.defvjp(lambda a, b: (_dot(a, b, _NN), (a, b)),
          lambda r, g: (_dot(g, r[1], _NT), _dot(r[0], g, _TN)))


@jax.custom_vjp
def mm_nt(a, b):
    return _dot(a, b, _NT)


mm_nt.defvjp(lambda a, b: (_dot(a, b, _NT), (a, b)),
             lambda r, g: (_dot(g, r[1], _NN), _dot(g, r[0], _TN)))


@jax.custom_vjp
def mm_tn(a, b):
    return _dot(a, b, _TN)


mm_tn.defvjp(lambda a, b: (_dot(a, b, _TN), (a, b)),
             lambda r, g: (_dot(r[1], g, _NT), _dot(r[0], g, _NN)))


def _silu(x):
    return x * jax.nn.sigmoid(x)


def _rms(x, g):
    return x * lax.rsqrt(jnp.mean(x * x, axis=-1, keepdims=True) + EPS) * g


def _modnorm(x, g, sh, sc):
    return _rms(x, g) * (1.0 + sc) + sh


def _blk(dim, target, mult=128):
    best = None
    for b in range(mult, min(dim, target) + 1, mult):
        if dim % b == 0:
            best = b
    return best if best is not None else dim


def matmul(a, b, mode, out_dtype, name):
    if mode == "nn":
        (m, k), n = a.shape, b.shape[1]
    elif mode == "nt":
        (m, k), n = a.shape, b.shape[0]
    else:
        (k, m), n = a.shape, b.shape[1]
    bm, bn, bk = _blk(m, 512), _blk(n, 1408), _blk(k, 2816)
    nk = k // bk
    dims = {"nn": _NN, "nt": _NT, "tn": _TN}[mode]

    def body(a_ref, b_ref, o_ref, acc_ref):
        kk = pl.program_id(2)
        part = _dot(a_ref[...], b_ref[...], dims)
        if nk == 1:
            o_ref[...] = part.astype(o_ref.dtype)
            return

        @pl.when(kk == 0)
        def _():
            acc_ref[...] = part

        @pl.when((kk > 0) & (kk < nk - 1))
        def _():
            acc_ref[...] += part

        @pl.when(kk == nk - 1)
        def _():
            o_ref[...] = (acc_ref[...] + part).astype(o_ref.dtype)

    a_spec = pl.BlockSpec((bk, bm), lambda i, j, kk: (kk, i)) if mode == "tn" else pl.BlockSpec((bm, bk), lambda i, j, kk: (i, kk))
    b_spec = pl.BlockSpec((bn, bk), lambda i, j, kk: (j, kk)) if mode == "nt" else pl.BlockSpec((bk, bn), lambda i, j, kk: (kk, j))
    return pl.pallas_call(
        body, name=name, out_shape=jax.ShapeDtypeStruct((m, n), out_dtype), grid=(m // bm, n // bn, nk),
        in_specs=[a_spec, b_spec], out_specs=pl.BlockSpec((bm, bn), lambda i, j, kk: (i, j)),
        scratch_shapes=[pltpu.VMEM((bm, bn), F32)], compiler_params=_cp(("parallel", "parallel", "arbitrary")),
    )(a, b)


TM = 512


def _row(v):
    return v.reshape(1, -1)


def modnorm_fwd(x, g, sh, sc, name):
    t = x.shape[0]

    def body(x_ref, g_ref, sh_ref, sc_ref, o_ref):
        o_ref[...] = _modnorm(x_ref[...], g_ref[...], sh_ref[...], sc_ref[...]).astype(o_ref.dtype)

    vec = pl.BlockSpec((1, D), lambda i: (0, 0))
    return pl.pallas_call(
        body, name=name, out_shape=jax.ShapeDtypeStruct((t, D), BF16), grid=(t // TM,),
        in_specs=[pl.BlockSpec((TM, D), lambda i: (i, 0)), vec, vec, vec],
        out_specs=pl.BlockSpec((TM, D), lambda i: (i, 0)), compiler_params=_cp(("parallel",)),
    )(x, _row(g), _row(sh), _row(sc))


def modnorm_bwd(x, g, sh, sc, dh, dres, name):
    t = x.shape[0]

    def body(x_ref, g_ref, sh_ref, sc_ref, dh_ref, dres_ref, dx_ref, sums_ref):
        _, vjp = jax.vjp(_modnorm, x_ref[...], g_ref[...], sh_ref[...], sc_ref[...])
        dx, dg, dsh, dsc = vjp(dh_ref[...].astype(F32))
        dx_ref[...] = dx + dres_ref[...]

        @pl.when(pl.program_id(0) == 0)
        def _():
            sums_ref[...] = jnp.zeros_like(sums_ref)

        sums_ref[0:1, :] += dg
        sums_ref[1:2, :] += dsh
        sums_ref[2:3, :] += dsc

    vec = pl.BlockSpec((1, D), lambda i: (0, 0))
    tile = pl.BlockSpec((TM, D), lambda i: (i, 0))
    return pl.pallas_call(
        body, name=name, out_shape=(jax.ShapeDtypeStruct((t, D), F32), jax.ShapeDtypeStruct((8, D), F32)), grid=(t // TM,),
        in_specs=[tile, vec, vec, vec, tile, tile], out_specs=(tile, pl.BlockSpec((8, D), lambda i: (0, 0))),
        compiler_params=_cp(("arbitrary",)),
    )(x, _row(g), _row(sh), _row(sc), dh, dres)


def resid_fwd(x, y, gate, name):
    t = x.shape[0]

    def body(x_ref, y_ref, g_ref, o_ref):
        o_ref[...] = x_ref[...] + g_ref[...] * y_ref[...]

    tile = pl.BlockSpec((TM, D), lambda i: (i, 0))
    return pl.pallas_call(
        body, name=name, out_shape=jax.ShapeDtypeStruct((t, D), F32), grid=(t // TM,),
        in_specs=[tile, tile, pl.BlockSpec((1, D), lambda i: (0, 0))], out_specs=tile, compiler_params=_cp(("parallel",)),
    )(x, y, _row(gate))


def resid_bwd(dxo, y, gate, name):
    t = dxo.shape[0]

    def body(d_ref, y_ref, g_ref, dy_ref, dg_ref):
        d = d_ref[...]
        dy_ref[...] = (d * g_ref[...]).astype(BF16)

        @pl.when(pl.program_id(0) == 0)
        def _():
            dg_ref[...] = jnp.zeros_like(dg_ref)

        dg_ref[0:1, :] += jnp.sum(d * y_ref[...], axis=0, keepdims=True)

    tile = pl.BlockSpec((TM, D), lambda i: (i, 0))
    return pl.pallas_call(
        body, name=name, out_shape=(jax.ShapeDtypeStruct((t, D), BF16), jax.ShapeDtypeStruct((8, D), F32)), grid=(t // TM,),
        in_specs=[tile, tile, pl.BlockSpec((1, D), lambda i: (0, 0))], out_specs=(tile, pl.BlockSpec((8, D), lambda i: (0, 0))),
        compiler_params=_cp(("arbitrary",)),
    )(dxo, y, _row(gate))


CW = 256
NCW = D_FF // CW


def _conv3(u, halo, w, b):
    n = u.shape[0]
    win = jnp.concatenate([halo, u], axis=0)
    return b + w[0:1] * win[6:6 + n] + w[1:2] * win[7:7 + n] + w[2:3] * win[8:8 + n]


def _convglu(ua, ub, ha, hb, wa, wb, ba, bb):
    return _silu(_conv3(ua, ha, wa, ba)) * _conv3(ub, hb, wb, bb)


def _halo_specs(tm, cw, off):
    r = tm // 8
    return pl.BlockSpec((8, cw), lambda j, i, o=off: (jnp.maximum(i * r - 1, 0), j + o))


def convglu_fwd(u0, cw, cb, name):
    t = u0.shape[0]

    def body(ua_ref, ub_ref, ha_ref, hb_ref, wa_ref, wb_ref, ba_ref, bb_ref, o_ref):
        keep = (pl.program_id(1) > 0).astype(F32)
        o_ref[...] = _convglu(ua_ref[...], ub_ref[...], ha_ref[...] * keep, hb_ref[...] * keep,
                              wa_ref[...], wb_ref[...], ba_ref[...], bb_ref[...]).astype(o_ref.dtype)

    def col(rows, off):
        return pl.BlockSpec((rows, CW), lambda j, i, o=off: (0, j + o))

    return pl.pallas_call(
        body, name=name, out_shape=jax.ShapeDtypeStruct((t, D_FF), BF16), grid=(NCW, t // TM),
        in_specs=[pl.BlockSpec((TM, CW), lambda j, i: (i, j)), pl.BlockSpec((TM, CW), lambda j, i: (i, j + NCW)),
                  _halo_specs(TM, CW, 0), _halo_specs(TM, CW, NCW), col(3, 0), col(3, NCW), col(1, 0), col(1, NCW)],
        out_specs=pl.BlockSpec((TM, CW), lambda j, i: (i, j)), compiler_params=_cp(("parallel", "parallel")),
    )(u0, u0, u0, u0, cw, cw, _row(cb), _row(cb))


def convglu_bwd(u0, cw, cb, dgact, name):
    t = u0.shape[0]
    nt = t // TM

    def body(ua_ref, ub_ref, ha_ref, hb_ref, wa_ref, wb_ref, ba_ref, bb_ref, dg_ref,
             dua_ref, dub_ref, dwa_ref, dwb_ref, dba_ref, dbb_ref, ca_ref, cb_ref):
        step = pl.program_id(1)
        keep = (step < nt - 1).astype(F32)

        @pl.when(step == 0)
        def _():
            ca_ref[...] = jnp.zeros_like(ca_ref)
            cb_ref[...] = jnp.zeros_like(cb_ref)
            dwa_ref[...] = jnp.zeros_like(dwa_ref)
            dwb_ref[...] = jnp.zeros_like(dwb_ref)
            dba_ref[...] = jnp.zeros_like(dba_ref)
            dbb_ref[...] = jnp.zeros_like(dbb_ref)

        _, vjp = jax.vjp(_convglu, ua_ref[...], ub_ref[...], ha_ref[...] * keep, hb_ref[...] * keep,
                         wa_ref[...], wb_ref[...], ba_ref[...], bb_ref[...])
        dua, dub, dha, dhb, dwa, dwb, dba, dbb = vjp(dg_ref[...].astype(F32))
        zeros = jnp.zeros((TM - 8, CW), F32)
        dua_ref[...] = (dua + jnp.concatenate([zeros, ca_ref[...]], axis=0)).astype(BF16)
        dub_ref[...] = (dub + jnp.concatenate([zeros, cb_ref[...]], axis=0)).astype(BF16)
        ca_ref[...] = dha * keep
        cb_ref[...] = dhb * keep
        dwa_ref[...] += dwa
        dwb_ref[...] += dwb
        dba_ref[...] += dba
        dbb_ref[...] += dbb

    def rev(i):
        return nt - 1 - i

    def tile(off):
        return pl.BlockSpec((TM, CW), lambda j, i, o=off: (rev(i), j + o))

    def halo(off):
        r = TM // 8
        return pl.BlockSpec((8, CW), lambda j, i, o=off: (jnp.maximum(rev(i) * r - 1, 0), j + o))

    def col(rows, off):
        return pl.BlockSpec((rows, CW), lambda j, i, o=off: (0, j + o))

    outs = pl.pallas_call(
        body, name=name,
        out_shape=(jax.ShapeDtypeStruct((t, D_FF), BF16), jax.ShapeDtypeStruct((t, D_FF), BF16),
                   jax.ShapeDtypeStruct((3, D_FF), F32), jax.ShapeDtypeStruct((3, D_FF), F32),
                   jax.ShapeDtypeStruct((1, D_FF), F32), jax.ShapeDtypeStruct((1, D_FF), F32)),
        grid=(NCW, nt),
        in_specs=[tile(0), tile(NCW), halo(0), halo(NCW), col(3, 0), col(3, NCW), col(1, 0), col(1, NCW), tile(0)],
        out_specs=(tile(0), tile(0), col(3, 0), col(3, 0), col(1, 0), col(1, 0)),
        scratch_shapes=[pltpu.VMEM((8, CW), F32), pltpu.VMEM((8, CW), F32)],
        compiler_params=_cp(("parallel", "arbitrary")),
    )(u0, u0, u0, u0, cw, cw, _row(cb), _row(cb), dgact)
    dua, dub, dwa, dwb, dba, dbb = outs
    return (jnp.concatenate([dua, dub], axis=1), jnp.concatenate([dwa, dwb], axis=1), jnp.concatenate([dba, dbb], axis=1))


def _pick(v, h, axis):
    return v[:, h:h + 1] if axis == 1 else v[h:h + 1, :]


def _ssd_chunk(z, xh, xc, dtp, hin, cw, cb, dtb, alog, dsk, ng):
    lane_hi = lax.broadcasted_iota(jnp.int32, (Q, Q), 1) >= 64
    row_hi = lax.broadcasted_iota(jnp.int32, (Q, Q), 0) >= 64
    causal = lax.broadcasted_iota(jnp.int32, (Q, Q), 0) >= lax.broadcasted_iota(jnp.int32, (Q, Q), 1)
    win = jnp.concatenate([xh, xc], axis=0)
    xbc = cb
    for k in range(4):
        xbc = xbc + cw[k:k + 1] * win[5 + k:5 + k + Q]
    xbc = _silu(xbc)
    xs, bm, cm = xbc[:, 0:512], xbc[:, 512:768], xbc[:, 768:1024]
    dt = jax.nn.softplus(dtp + dtb)
    da = dt * (-jnp.exp(alog))
    ah = jnp.dot(causal.astype(F32), da, precision=lax.Precision.HIGHEST, preferred_element_type=F32)
    aht = ah.T
    alast = ah[Q - 1:Q, :]
    eah = jnp.exp(ah)
    dte = jnp.exp(alast - ah)
    elast = jnp.exp(alast)
    ys, houts = [], []
    for g in range(2):
        bg, cg = bm[:, 128 * g:128 * g + 128], cm[:, 128 * g:128 * g + 128]
        cbm = mm_nt(cg, bg)
        for jp in range(2):
            j = 2 * g + jp
            h0, h1 = 2 * j, 2 * j + 1
            xp = xs[:, 128 * j:128 * j + 128]
            xdt = xp * jnp.where(lane_hi, _pick(dt, h1, 1), _pick(dt, h0, 1))
            yd, st = [], []
            for h in (h0, h1):
                seg = _pick(ah, h, 1) - _pick(aht, h, 0)
                decay = jnp.exp(jnp.where(causal, seg, NEG))
                yd.append(mm(cbm * decay, xdt))
                st.append(mm_tn(xdt * _pick(dte, h, 1), bg))
            hj = hin[j]
            hout = hj * jnp.where(row_hi, _pick(elast, h1, 1), _pick(elast, h0, 1)) + jnp.where(row_hi, st[1], st[0])
            yoff = mm_nt(cg, hj) * jnp.where(lane_hi, _pick(eah, h1, 1), _pick(eah, h0, 1))
            skip = xp * jnp.where(lane_hi[0:1], _pick(dsk, h1, 1), _pick(dsk, h0, 1))
            ys.append(jnp.where(lane_hi, yd[1], yd[0]) + yoff + skip)
            houts.append(hout)
    y = jnp.concatenate(ys, axis=1) * _silu(z)
    yn = []
    for g in range(2):
        yg = y[:, 256 * g:256 * g + 256]
        yn.append(yg * lax.rsqrt(jnp.mean(yg * yg, axis=-1, keepdims=True) + EPS))
    return jnp.concatenate(yn, axis=1) * ng, jnp.stack(houts)


def _pad_lanes(v, n=128):
    v = v.reshape(1, -1)
    return jnp.pad(v, ((0, 0), (0, n - v.shape[1])))


def _ssd_in_specs(chunk_of):
    return [pl.BlockSpec((Q, 512), lambda i: (chunk_of(i), P_Z // 512)),
            pl.BlockSpec((8, 1024), lambda i: (jnp.maximum(chunk_of(i) * (Q // 8) - 1, 0), P_XBC // 1024)),
            pl.BlockSpec((Q, 1024), lambda i: (chunk_of(i), P_XBC // 1024)),
            pl.BlockSpec((Q, 128), lambda i: (chunk_of(i), P_DT // 128))]


def _full(shape):
    nd = len(shape)
    return pl.BlockSpec(shape, lambda i: (0,) * nd)


def ssd_fwd(proj, cw, cb, dtb, alog, dsk, ng, name):
    t = proj.shape[0]
    nc = t // Q

    def body(z_ref, xh_ref, xc_ref, dt_ref, cw_ref, cb_ref, dtb_ref, al_ref, dsk_ref, ng_ref, y_ref, hs_ref, h_ref):
        i = pl.program_id(0)

        @pl.when(i == 0)
        def _():
            h_ref[...] = jnp.zeros_like(h_ref)

        hin = h_ref[...]
        hs_ref[0] = hin
        y, hout = _ssd_chunk(z_ref[...], xh_ref[...] * (i > 0).astype(F32), xc_ref[...], dt_ref[...], hin,
                             cw_ref[...], cb_ref[...], dtb_ref[...], al_ref[...], dsk_ref[...], ng_ref[...])
        y_ref[...] = y
        h_ref[...] = hout

    return pl.pallas_call(
        body, name=name,
        out_shape=(jax.ShapeDtypeStruct((t, 512), F32), jax.ShapeDtypeStruct((nc, 4, 128, 128), F32)), grid=(nc,),
        in_specs=_ssd_in_specs(lambda i: i) + [_full((4, 1024)), _full((1, 1024)), _full((1, 128)), _full((1, 128)),
                                               _full((1, 128)), _full((1, 512))],
        out_specs=(pl.BlockSpec((Q, 512), lambda i: (i, 0)), pl.BlockSpec((1, 4, 128, 128), lambda i: (i, 0, 0, 0))),
        scratch_shapes=[pltpu.VMEM((4, 128, 128), F32)], compiler_params=_cp(("arbitrary",)),
    )(proj, proj, proj, proj, cw, _row(cb), _pad_lanes(dtb), _pad_lanes(alog), _pad_lanes(dsk), _row(ng))


def ssd_bwd(proj, hs, dy, cw, cb, dtb, alog, dsk, ng, name):
    t = proj.shape[0]
    nc = t // Q

    def body(z_ref, xh_ref, xc_ref, dt_ref, hs_ref, dy_ref, cw_ref, cb_ref, dtb_ref, al_ref, dsk_ref, ng_ref,
             dz_ref, dx_ref, ddt_ref, dcw_ref, vec_ref, dh_ref, carry_ref):
        step = pl.program_id(0)
        keep = (step < nc - 1).astype(F32)

        @pl.when(step == 0)
        def _():
            dh_ref[...] = jnp.zeros_like(dh_ref)
            carry_ref[...] = jnp.zeros_like(carry_ref)
            dcw_ref[...] = jnp.zeros_like(dcw_ref)
            vec_ref[...] = jnp.zeros_like(vec_ref)

        _, vjp = jax.vjp(_ssd_chunk, z_ref[...], xh_ref[...] * keep, xc_ref[...], dt_ref[...], hs_ref[0],
                         cw_ref[...], cb_ref[...], dtb_ref[...], al_ref[...], dsk_ref[...], ng_ref[...])
        dz, dxh, dxc, ddt, dhin, dcw, dcb, ddtb, dal, ddsk, dng = vjp((dy_ref[...], dh_ref[...]))
        dz_ref[...] = dz
        dx_ref[...] = dxc + jnp.concatenate([jnp.zeros((Q - 8, 1024), F32), carry_ref[...]], axis=0)
        ddt_ref[...] = ddt
        carry_ref[...] = dxh * keep
        dh_ref[...] = dhin
        dcw_ref[...] += dcw
        vec_ref[0:1, :] += dcb
        vec_ref[1:2, 0:128] += ddtb
        vec_ref[2:3, 0:128] += dal
        vec_ref[3:4, 0:128] += ddsk
        vec_ref[4:5, 0:512] += dng

    def rev(i):
        return nc - 1 - i

    return pl.pallas_call(
        body, name=name,
        out_shape=(jax.ShapeDtypeStruct((t, 512), F32), jax.ShapeDtypeStruct((t, 1024), F32), jax.ShapeDtypeStruct((t, 128), F32),
                   jax.ShapeDtypeStruct((4, 1024), F32), jax.ShapeDtypeStruct((8, 1024), F32)),
        grid=(nc,),
        in_specs=_ssd_in_specs(rev) + [pl.BlockSpec((1, 4, 128, 128), lambda i: (rev(i), 0, 0, 0)),
                                       pl.BlockSpec((Q, 512), lambda i: (rev(i), 0)),
                                       _full((4, 1024)), _full((1, 1024)), _full((1, 128)), _full((1, 128)), _full((1, 128)),
                                       _full((1, 512))],
        out_specs=(pl.BlockSpec((Q, 512), lambda i: (rev(i), 0)), pl.BlockSpec((Q, 1024), lambda i: (rev(i), 0)),
                   pl.BlockSpec((Q, 128), lambda i: (rev(i), 0)), _full((4, 1024)), _full((8, 1024))),
        scratch_shapes=[pltpu.VMEM((4, 128, 128), F32), pltpu.VMEM((8, 1024), F32)], compiler_params=_cp(("arbitrary",)),
    )(proj, proj, proj, proj, hs, dy, cw, _row(cb), _pad_lanes(dtb), _pad_lanes(alog), _pad_lanes(dsk), _row(ng))


TA = 512
MLA_SCALE = 1.0 / math.sqrt(MLA_QK)


def _rope(x1, x2, cos, sin):
    return x1 * cos - x2 * sin, x1 * sin + x2 * cos


def _mla_pre(cq, ckv, kr, cos, sin, qg, kvg, wuq, wukv):
    n = cq.shape[0]
    qh = mm(_rms(cq, qg), wuq)
    kv = mm(_rms(ckv, kvg), wukv)
    kr1, kr2 = _rope(kr[:, 0:16], kr[:, 16:32], cos, sin)
    pad = jnp.zeros((n, 32), F32)
    qs, ks, vs = [], [], []
    for h in range(MLA_HEADS):
        b = qh[:, 128 * h:128 * h + 128]
        q1, q2 = _rope(b[:, 64:80], b[:, 80:96], cos, sin)
        qs.append(jnp.concatenate([b[:, 0:64], q1, q2, pad], axis=1))
        ks.append(jnp.concatenate([kv[:, 128 * h:128 * h + 64], kr1, kr2, pad], axis=1))
        vs.append(kv[:, 128 * h + 64:128 * h + 128])
    return jnp.stack(qs), jnp.stack(ks), jnp.stack(vs)


def _mla_pre_specs():
    return [pl.BlockSpec((TM, 256), lambda i: (i, P_CQ // 256)), pl.BlockSpec((TM, 128), lambda i: (i, P_CKV // 128)),
            pl.BlockSpec((TM, 128), lambda i: (i, P_KR // 128)), pl.BlockSpec((TM, 16), lambda i: (i, 0)),
            pl.BlockSpec((TM, 16), lambda i: (i, 0)), _full((1, 256)), _full((1, 128)), _full((256, 512)), _full((128, 512))]


def _head_tile(w):
    return pl.BlockSpec((MLA_HEADS, TM, w), lambda i: (0, i, 0))


def mla_pre_fwd(proj, cos, sin, qg, kvg, wuq, wukv, name):
    t = proj.shape[0]

    def body(cq_ref, ckv_ref, kr_ref, cos_ref, sin_ref, qg_ref, kvg_ref, wuq_ref, wukv_ref, q_ref, k_ref, v_ref):
        q, k, v = _mla_pre(cq_ref[...], ckv_ref[...], kr_ref[...], cos_ref[...], sin_ref[...], qg_ref[...], kvg_ref[...],
                           wuq_ref[...], wukv_ref[...])
        q_ref[...] = q.astype(BF16)
        k_ref[...] = k.astype(BF16)
        v_ref[...] = v.astype(BF16)

    return pl.pallas_call(
        body, name=name,
        out_shape=(jax.ShapeDtypeStruct((MLA_HEADS, t, 128), BF16), jax.ShapeDtypeStruct((MLA_HEADS, t, 128), BF16),
                   jax.ShapeDtypeStruct((MLA_HEADS, t, 64), BF16)),
        grid=(t // TM,), in_specs=_mla_pre_specs(), out_specs=(_head_tile(128), _head_tile(128), _head_tile(64)),
        compiler_params=_cp(("parallel",)),
    )(proj, proj, proj, cos, sin, _row(qg), _row(kvg), wuq, wukv)


def mla_pre_bwd(proj, cos, sin, qg, kvg, wuq, wukv, dq, dk, dv, name):
    t = proj.shape[0]

    def body(cq_ref, ckv_ref, kr_ref, cos_ref, sin_ref, qg_ref, kvg_ref, wuq_ref, wukv_ref, dq_ref, dk_ref, dv_ref,
             dcq_ref, dckv_ref, dkr_ref, dwuq_ref, dwukv_ref, vec_ref):
        @pl.when(pl.program_id(0) == 0)
        def _():
            dwuq_ref[...] = jnp.zeros_like(dwuq_ref)
            dwukv_ref[...] = jnp.zeros_like(dwukv_ref)
            vec_ref[...] = jnp.zeros_like(vec_ref)

        cos, sin = cos_ref[...], sin_ref[...]
        f = lambda cq, ckv, kr, qg, kvg, wuq, wukv: _mla_pre(cq, ckv, kr, cos, sin, qg, kvg, wuq, wukv)
        _, vjp = jax.vjp(f, cq_ref[...], ckv_ref[...], kr_ref[...], qg_ref[...], kvg_ref[...], wuq_ref[...], wukv_ref[...])
        dcq, dckv, dkr, dqg, dkvg, dwuq, dwukv = vjp((dq_ref[...], dk_ref[...], dv_ref[...]))
        dcq_ref[...] = dcq
        dckv_ref[...] = dckv
        dkr_ref[...] = dkr
        dwuq_ref[...] += dwuq
        dwukv_ref[...] += dwukv
        vec_ref[0:1, :] += dqg
        vec_ref[1:2, 0:128] += dkvg

    return pl.pallas_call(
        body, name=name,
        out_shape=(jax.ShapeDtypeStruct((t, 256), F32), jax.ShapeDtypeStruct((t, 128), F32), jax.ShapeDtypeStruct((t, 128), F32),
                   jax.ShapeDtypeStruct((256, 512), F32), jax.ShapeDtypeStruct((128, 512), F32), jax.ShapeDtypeStruct((8, 256), F32)),
        grid=(t // TM,), in_specs=_mla_pre_specs() + [_head_tile(128), _head_tile(128), _head_tile(64)],
        out_specs=(pl.BlockSpec((TM, 256), lambda i: (i, 0)), pl.BlockSpec((TM, 128), lambda i: (i, 0)),
                   pl.BlockSpec((TM, 128), lambda i: (i, 0)), _full((256, 512)), _full((128, 512)), _full((8, 256))),
        compiler_params=_cp(("arbitrary",)),
    )(proj, proj, proj, cos, sin, _row(qg), _row(kvg), wuq, wukv, dq, dk, dv)


def _causal_mask(i, j):
    qpos = i * TA + lax.broadcasted_iota(jnp.int32, (TA, TA), 0)
    kpos = j * TA + lax.broadcasted_iota(jnp.int32, (TA, TA), 1)
    return kpos <= qpos


def mla_flash_fwd(q, k, v, name):
    h, t, _ = q.shape

    def body(q_ref, k_ref, v_ref, o_ref, lse_ref, m_ref, l_ref, acc_ref):
        i = pl.program_id(1)
        m_ref[...] = jnp.full_like(m_ref, NEG)
        l_ref[...] = jnp.zeros_like(l_ref)
        acc_ref[...] = jnp.zeros_like(acc_ref)
        qb = q_ref[0]

        def step(j, carry, diagonal=False):
            rows = pl.ds(pl.multiple_of(j * TA, TA), TA)
            s = _dot(qb, k_ref[0, rows, :], _NT) * MLA_SCALE
            if diagonal:
                s = jnp.where(_causal_mask(i, j), s, NEG)
            m_new = jnp.maximum(m_ref[...], jnp.max(s, axis=-1, keepdims=True))
            p = jnp.exp(s - m_new)
            alpha = jnp.exp(m_ref[...] - m_new)
            l_ref[...] = alpha * l_ref[...] + jnp.sum(p, axis=-1, keepdims=True)
            acc_ref[...] = alpha * acc_ref[...] + _dot(p, v_ref[0, rows, :], _NN)
            m_ref[...] = m_new
            return carry

        lax.fori_loop(0, i, step, 0)
        step(i, 0, diagonal=True)
        o_ref[0] = acc_ref[...] / l_ref[...]
        lse_ref[0] = m_ref[...] + jnp.log(l_ref[...])

    return pl.pallas_call(
        body, name=name,
        out_shape=(jax.ShapeDtypeStruct((h, t, 64), F32), jax.ShapeDtypeStruct((h, t, 1), F32)), grid=(h, t // TA),
        in_specs=[pl.BlockSpec((1, TA, 128), lambda hh, i: (hh, i, 0)), pl.BlockSpec((1, t, 128), lambda hh, i: (hh, 0, 0)),
                  pl.BlockSpec((1, t, 64), lambda hh, i: (hh, 0, 0))],
        out_specs=(pl.BlockSpec((1, TA, 64), lambda hh, i: (hh, i, 0)), pl.BlockSpec((1, TA, 1), lambda hh, i: (hh, i, 0))),
        scratch_shapes=[pltpu.VMEM((TA, 1), F32), pltpu.VMEM((TA, 1), F32), pltpu.VMEM((TA, 64), F32)],
        compiler_params=_cp(("parallel", "parallel")),
    )(q, k, v)


def mla_flash_bwd(q, k, v, o, lse, do, name):
    h, t, _ = q.shape
    nb = t // TA

    def body(q_ref, k_ref, v_ref, o_ref, lse_ref, do_ref, dq_ref, dk_ref, dv_ref):
        j = pl.program_id(1)

        @pl.when(j == 0)
        def _():
            dq_ref[...] = jnp.zeros_like(dq_ref)

        dk_ref[...] = jnp.zeros_like(dk_ref)
        dv_ref[...] = jnp.zeros_like(dv_ref)
        kb, vb = k_ref[0], v_ref[0]

        def step(i, carry, diagonal=False):
            rows = pl.ds(pl.multiple_of(i * TA, TA), TA)
            qb, dob = q_ref[0, rows, :], do_ref[0, rows, :]
            s = _dot(qb, kb, _NT) * MLA_SCALE
            p = jnp.exp(s - lse_ref[0, rows, :])
            if diagonal:
                p = jnp.where(_causal_mask(i, j), p, 0.0)
            delta = jnp.sum(dob * o_ref[0, rows, :], axis=-1, keepdims=True)
            dv_ref[0] += _dot(p, dob, _TN)
            ds = p * (_dot(dob, vb, _NT) - delta) * MLA_SCALE
            dk_ref[0] += _dot(ds, qb, _TN)
            dq_ref[0, rows, :] += _dot(ds, kb, _NN)
            return carry

        step(j, 0, diagonal=True)
        lax.fori_loop(j + 1, nb, step, 0)

    def whole(w):
        return pl.BlockSpec((1, t, w), lambda hh, j: (hh, 0, 0))

    def blk(w):
        return pl.BlockSpec((1, TA, w), lambda hh, j: (hh, j, 0))

    return pl.pallas_call(
        body, name=name,
        out_shape=(jax.ShapeDtypeStruct((h, t, 128), F32), jax.ShapeDtypeStruct((h, t, 128), F32), jax.ShapeDtypeStruct((h, t, 64), F32)),
        grid=(h, nb), in_specs=[whole(128), blk(128), blk(64), whole(64), whole(1), whole(64)],
        out_specs=(whole(128), blk(128), blk(64)), compiler_params=_cp(("parallel", "arbitrary")),
    )(q, k, v, o, lse, do)


SWA_SCALE = 1.0 / 8.0


def _swa_block(q, kp, kc, vp, vc, sinks, has_prev):
    k2 = jnp.concatenate([kp, kc], axis=0)
    v2 = jnp.concatenate([vp, vc], axis=0)
    rel = Q + lax.broadcasted_iota(jnp.int32, (Q, 2 * Q), 0) - lax.broadcasted_iota(jnp.int32, (Q, 2 * Q), 1)
    valid = (rel >= 0) & (rel < Q) & ((lax.broadcasted_iota(jnp.int32, (Q, 2 * Q), 1) >= Q) | has_prev)
    outs = []
    for h in range(4):
        g = h // 2
        s = mm_nt(q[:, 64 * h:64 * h + 64], k2[:, 64 * g:64 * g + 64]) * SWA_SCALE
        s = jnp.where(valid, s, NEG)
        sink = sinks[:, h:h + 1]
        m = jnp.maximum(jnp.max(s, axis=-1, keepdims=True), sink)
        e = jnp.exp(s - m)
        p = e / (jnp.sum(e, axis=-1, keepdims=True) + jnp.exp(sink - m))
        outs.append(mm(p, v2[:, 64 * g:64 * g + 64]))
    return jnp.concatenate(outs, axis=1)


def _swa_specs(blk_of):
    def prev(i):
        return jnp.maximum(blk_of(i) - 1, 0)

    return [pl.BlockSpec((Q, 256), lambda i: (blk_of(i), P_SQ // 256)),
            pl.BlockSpec((Q, 128), lambda i: (prev(i), P_SK // 128)), pl.BlockSpec((Q, 128), lambda i: (blk_of(i), P_SK // 128)),
            pl.BlockSpec((Q, 128), lambda i: (prev(i), P_SV // 128)), pl.BlockSpec((Q, 128), lambda i: (blk_of(i), P_SV // 128)),
            _full((1, 128))]


def swa_fwd(proj, sinks, name):
    t = proj.shape[0]

    def body(q_ref, kp_ref, kc_ref, vp_ref, vc_ref, s_ref, o_ref):
        o_ref[...] = _swa_block(q_ref[...], kp_ref[...], kc_ref[...], vp_ref[...], vc_ref[...], s_ref[...], pl.program_id(0) > 0)

    return pl.pallas_call(
        body, name=name, out_shape=jax.ShapeDtypeStruct((t, 256), F32), grid=(t // Q,), in_specs=_swa_specs(lambda i: i),
        out_specs=pl.BlockSpec((Q, 256), lambda i: (i, 0)), compiler_params=_cp(("parallel",)),
    )(proj, proj, proj, proj, proj, _pad_lanes(sinks))


def swa_bwd(proj, sinks, do, name):
    t = proj.shape[0]
    nb = t // Q

    def body(q_ref, kp_ref, kc_ref, vp_ref, vc_ref, s_ref, do_ref, dq_ref, dk_ref, dv_ref, ds_ref, ck_ref, cv_ref):
        step = pl.program_id(0)

        @pl.when(step == 0)
        def _():
            ck_ref[...] = jnp.zeros_like(ck_ref)
            cv_ref[...] = jnp.zeros_like(cv_ref)
            ds_ref[...] = jnp.zeros_like(ds_ref)

        has_prev = step < nb - 1
        f = lambda q, kp, kc, vp, vc, s: _swa_block(q, kp, kc, vp, vc, s, has_prev)
        _, vjp = jax.vjp(f, q_ref[...], kp_ref[...], kc_ref[...], vp_ref[...], vc_ref[...], s_ref[...])
        dq, dkp, dkc, dvp, dvc, dsk = vjp(do_ref[...])
        dq_ref[...] = dq
        dk_ref[...] = dkc + ck_ref[...]
        dv_ref[...] = dvc + cv_ref[...]
        ck_ref[...] = dkp
        cv_ref[...] = dvp
        ds_ref[0:1, :] += dsk

    def rev(i):
        return nb - 1 - i

    return pl.pallas_call(
        body, name=name,
        out_shape=(jax.ShapeDtypeStruct((t, 256), F32), jax.ShapeDtypeStruct((t, 128), F32), jax.ShapeDtypeStruct((t, 128), F32),
                   jax.ShapeDtypeStruct((8, 128), F32)),
        grid=(nb,), in_specs=_swa_specs(rev) + [pl.BlockSpec((Q, 256), lambda i: (rev(i), 0))],
        out_specs=(pl.BlockSpec((Q, 256), lambda i: (rev(i), 0)), pl.BlockSpec((Q, 128), lambda i: (rev(i), 0)),
                   pl.BlockSpec((Q, 128), lambda i: (rev(i), 0)), _full((8, 128))),
        scratch_shapes=[pltpu.VMEM((Q, 128), F32), pltpu.VMEM((Q, 128), F32)], compiler_params=_cp(("arbitrary",)),
    )(proj, proj, proj, proj, proj, _pad_lanes(sinks), do)


def _loss_tile(x, g, tgt):
    err = jnp.square(_rms(x, g) - tgt)
    return 0.5 * jnp.sum(jnp.mean(err, axis=-1, keepdims=True), axis=0, keepdims=True)


def loss_fwd_bwd(x, g, tgt, name):
    t = x.shape[0]

    def body(x_ref, g_ref, t_ref, loss_ref, dx_ref, dg_ref):
        @pl.when(pl.program_id(0) == 0)
        def _():
            loss_ref[...] = jnp.zeros_like(loss_ref)
            dg_ref[...] = jnp.zeros_like(dg_ref)

        tgt = t_ref[...]
        val, vjp = jax.vjp(lambda x, g: _loss_tile(x, g, tgt), x_ref[...], g_ref[...])
        dx, dg = vjp(jnp.ones((1, 1), F32))
        dx_ref[...] = dx
        dg_ref[0:1, :] += dg
        loss_ref[...] += val

    tile = pl.BlockSpec((TM, D), lambda i: (i, 0))
    return pl.pallas_call(
        body, name=name,
        out_shape=(jax.ShapeDtypeStruct((8, 128), F32), jax.ShapeDtypeStruct((t, D), F32), jax.ShapeDtypeStruct((8, D), F32)),
        grid=(t // TM,), in_specs=[tile, _full((1, D)), tile], out_specs=(_full((8, 128)), tile, _full((8, D))),
        compiler_params=_cp(("arbitrary",)),
    )(x, _row(g), tgt)


def adamw(w, g, m, v, name):
    shape = w.shape
    cols = shape[-1] if w.ndim > 1 else shape[0]
    w2, g2, m2, v2 = (a.reshape(-1, cols) for a in (w, g, m, v))
    rows = w2.shape[0]
    br = _blk(rows, max(8, (1 << 19) // cols), 8)

    def body(w_ref, g_ref, m_ref, v_ref, d_ref, nm_ref, nv_ref):
        gg = g_ref[...]
        nm = ADAM_B1 * m_ref[...] + (1.0 - ADAM_B1) * gg
        nv = ADAM_B2 * v_ref[...] + (1.0 - ADAM_B2) * jnp.square(gg)
        m_hat = nm / (1.0 - ADAM_B1 ** ADAM_STEP)
        v_hat = nv / (1.0 - ADAM_B2 ** ADAM_STEP)
        d_ref[...] = -ADAM_LR * (m_hat / (jnp.sqrt(v_hat) + ADAM_EPS) + ADAM_WD * w_ref[...])
        nm_ref[...] = nm
        nv_ref[...] = nv

    spec = pl.BlockSpec((br, cols), lambda i: (i, 0))
    out = jax.ShapeDtypeStruct((rows, cols), F32)
    res = pl.pallas_call(body, name=name, out_shape=(out, out, out), grid=(rows // br,), in_specs=[spec] * 4,
                         out_specs=(spec, spec, spec), compiler_params=_cp(("parallel",)))(w2, g2, m2, v2)
    return tuple(r.reshape(shape) for r in res)


MESH = pl.DeviceIdType.MESH
CHIP_FLIPS = ((1, 0), (0, 1), (1, 1))


def _place():
    return lax.axis_index("x"), lax.axis_index("y"), lax.axis_index("c")


def allgather8(blk, name, in_vmem):
    space = pltpu.VMEM if in_vmem else pl.ANY

    def body(x_ref, out_ref, send_sems, recv_sems, local_sem):
        x, y, c = _place()
        me, sibling = (x, y, c), (x, y, 1 - c)
        chips = [(x ^ fx, y ^ fy) for fx, fy in CHIP_FLIPS]

        def slot(px, py, pc):
            return out_ref.at[4 * px + 2 * py + pc]

        def copy(k, block, to, src=None):
            return pltpu.make_async_remote_copy(
                src_ref=slot(*block) if src is None else src, dst_ref=slot(*block),
                send_sem=send_sems.at[k], recv_sem=recv_sems.at[k], device_id=to, device_id_type=MESH)

        mine = pltpu.make_async_copy(x_ref, slot(*me), local_sem)
        mine.start()
        first = [copy(0, me, sibling, src=x_ref)]
        first += [copy(1 + j, me, (*chip, c), src=x_ref) for j, chip in enumerate(chips)]
        for cp in first:
            cp.start()
        passed = [copy(4 + j, (*chip, c), sibling) for j, chip in enumerate(chips)]
        for j, chip in enumerate(chips):
            copy(1 + j, (*chip, c), me).wait_recv()
            passed[j].start()
        copy(0, sibling, me).wait_recv()
        for j, chip in enumerate(chips):
            copy(4 + j, (*chip, 1 - c), me).wait_recv()
        for cp in first + passed:
            cp.wait_send()
        mine.wait()

    return pl.pallas_call(
        body, name=name, out_shape=jax.ShapeDtypeStruct((N_DEV,) + blk.shape, blk.dtype),
        in_specs=[pl.BlockSpec(memory_space=space)], out_specs=pl.BlockSpec(memory_space=space),
        scratch_shapes=[pltpu.SemaphoreType.DMA((7,)), pltpu.SemaphoreType.DMA((7,)), pltpu.SemaphoreType.DMA],
        compiler_params=pltpu.CompilerParams(vmem_limit_bytes=VMEM_LIMIT),
    )(blk)


def flip_exchange(src, plan, n_out, name):
    def body(x_ref, out_ref, send_sems, recv_sems):
        x, y, c = _place()
        copies = []
        for k, (flip, src_index, dst_slot) in enumerate(plan):
            s, d = x_ref.at[src_index(x, y, c)], out_ref.at[dst_slot(x, y, c)]
            if flip is None:
                copies.append(pltpu.make_async_copy(s, d, send_sems.at[k]))
            else:
                copies.append(pltpu.make_async_remote_copy(
                    src_ref=s, dst_ref=d, send_sem=send_sems.at[k], recv_sem=recv_sems.at[k],
                    device_id=(x ^ flip[0], y ^ flip[1], c ^ flip[2]), device_id_type=MESH))
        for cp in copies:
            cp.start()
        for (flip, _, _), cp in zip(plan, copies):
            if flip is None:
                cp.wait()
            else:
                cp.wait_recv()
                cp.wait_send()

    n = len(plan)
    return pl.pallas_call(
        body, name=name, out_shape=jax.ShapeDtypeStruct((n_out,) + src.shape[1:], src.dtype),
        in_specs=[pl.BlockSpec(memory_space=pl.ANY)], out_specs=pl.BlockSpec(memory_space=pl.ANY),
        scratch_shapes=[pltpu.SemaphoreType.DMA((n,)), pltpu.SemaphoreType.DMA((n,))],
    )(src)


ROWS_ADD = 2048


def add_pairs(a, b, out_dtype, name):
    n = a.shape[0]
    br = _blk(n, ROWS_ADD, 16)

    def body(a_ref, b_ref, o_ref):
        o_ref[...] = (a_ref[...].astype(F32) + b_ref[...].astype(F32)).astype(o_ref.dtype)

    spec = pl.BlockSpec((br, 128), lambda i: (i, 0))
    return pl.pallas_call(body, name=name, out_shape=jax.ShapeDtypeStruct((n, 128), out_dtype), grid=(n // br,),
                          in_specs=[spec, spec], out_specs=spec, compiler_params=_cp(("parallel",)))(a, b)


def add_slots(own, others, name):
    n = own.shape[0]
    ns = others.shape[0]
    br = _blk(n, ROWS_ADD, 16)

    def body(a_ref, b_ref, o_ref):
        acc = a_ref[...].astype(F32)
        for s in range(ns):
            acc = acc + b_ref[s].astype(F32)
        o_ref[...] = acc

    return pl.pallas_call(body, name=name, out_shape=jax.ShapeDtypeStruct((n, 128), F32), grid=(n // br,),
                          in_specs=[pl.BlockSpec((br, 128), lambda i: (i, 0)), pl.BlockSpec((ns, br, 128), lambda i: (0, i, 0))],
                          out_specs=pl.BlockSpec((br, 128), lambda i: (i, 0)), compiler_params=_cp(("parallel",)))(own, others)


def sum8(g, name):
    r = g.shape[1]

    def body(g_ref, o_ref):
        acc = g_ref[0]
        for s in range(1, N_DEV):
            acc = acc + g_ref[s]
        o_ref[...] = acc

    return pl.pallas_call(body, name=name, out_shape=jax.ShapeDtypeStruct((r, 128), F32))(g)


def ada_mod(c_all, ada_w, ada_b_cols, name):
    def body(c_ref, w_ref, b_ref, o_ref):
        o_ref[0] = mm(_silu(c_ref[...]), w_ref[0]) + b_ref[0]

    n = ada_w.shape[2]
    return pl.pallas_call(
        body, name=name, out_shape=jax.ShapeDtypeStruct((DEPTH, N_DEV, n), F32), grid=(DEPTH,),
        in_specs=[pl.BlockSpec((N_DEV, D), lambda l: (0, 0)), pl.BlockSpec((1, D, n), lambda l: (l, 0, 0)),
                  pl.BlockSpec((1, 1, n), lambda l: (l, 0, 0))],
        out_specs=pl.BlockSpec((1, N_DEV, n), lambda l: (l, 0, 0)), compiler_params=_cp(("parallel",)),
    )(c_all, ada_w, ada_b_cols.reshape(DEPTH, 1, n))


def ada_grad(c_all, dmod_cols, name):
    def body(c_ref, d_ref, o_ref):
        o_ref[0] = mm_tn(_silu(c_ref[...]), d_ref[0])

    n = dmod_cols.shape[2]
    return pl.pallas_call(
        body, name=name, out_shape=jax.ShapeDtypeStruct((DEPTH, D, n), F32), grid=(DEPTH,),
        in_specs=[pl.BlockSpec((N_DEV, D), lambda l: (0, 0)), pl.BlockSpec((1, N_DEV, n), lambda l: (l, 0, 0))],
        out_specs=pl.BlockSpec((1, D, n), lambda l: (l, 0, 0)), compiler_params=_cp(("parallel",)),
    )(c_all, dmod_cols)


def pack_w_in(w):
    out = jnp.zeros(w.shape[:-1] + (NP,), w.dtype)
    for p_off, o_off, width in _PACK:
        out = out.at[..., p_off:p_off + width].set(w[..., o_off:o_off + width])
    return out


def unpack_w_in(w):
    return jnp.concatenate([w[..., p_off:p_off + width] for p_off, _, width in _PACK], axis=-1)


def pack_w_uq(w):
    return jnp.pad(w.reshape(w.shape[:-1] + (MLA_HEADS, MLA_QK)), [(0, 0)] * (w.ndim - 1) + [(0, 0), (0, 32)]).reshape(w.shape[:-1] + (512,))


def unpack_w_uq(w):
    return w.reshape(w.shape[:-1] + (MLA_HEADS, 128))[..., :MLA_QK].reshape(w.shape[:-1] + (MLA_HEADS * MLA_QK,))


def layer_fwd(x, mod, w, cos, sin, tag):
    h1 = modnorm_fwd(x, w["norm1_g"], mod[0], mod[1], tag + "norm1")
    proj = matmul(h1, w["w_in"], "nn", F32, tag + "w_in")
    y_ssd, hs = ssd_fwd(proj, w["ssd_conv_w"], w["ssd_conv_b"], w["ssd_dt_bias"], w["ssd_a_log"], w["ssd_d"], w["ssd_norm_g"], tag + "ssd")
    q, k, v = mla_pre_fwd(proj, cos, sin, w["mla_q_norm_g"], w["mla_kv_norm_g"], w["mla_w_uq"], w["mla_w_ukv"], tag + "mla_pre")
    o, lse = mla_flash_fwd(q, k, v, tag + "mla_attn")
    y_swa = swa_fwd(proj, w["swa_sinks"], tag + "swa")
    t = x.shape[0]
    ycat = jnp.concatenate([y_ssd, jnp.transpose(o, (1, 0, 2)).reshape(t, 256), y_swa], axis=1).astype(BF16)
    y = matmul(ycat, w["w_out"], "nn", F32, tag + "w_out")
    xm = resid_fwd(x, y, mod[2], tag + "res1")
    h2 = modnorm_fwd(xm, w["norm2_g"], mod[3], mod[4], tag + "norm2")
    u0 = matmul(h2, w["ffn_w_up"], "nn", F32, tag + "w_up")
    gact = convglu_fwd(u0, w["ffn_conv_w"], w["ffn_conv_b"], tag + "glu")
    yd = matmul(gact, w["ffn_w_down"], "nn", F32, tag + "w_down")
    xo = resid_fwd(xm, yd, mod[5], tag + "res2")
    return xo, dict(x=x, h1=h1, proj=proj, hs=hs, q=q, k=k, v=v, o=o, lse=lse, ycat=ycat, y=y, xm=xm, h2=h2, u0=u0, gact=gact, yd=yd)


def layer_bwd(dxo, s, mod, w, cos, sin, tag):
    t = dxo.shape[0]
    g = {}
    dyd, dg2 = resid_bwd(dxo, s["yd"], mod[5], tag + "res2_b")
    dgact = matmul(dyd, w["ffn_w_down"], "nt", BF16, tag + "w_down_dx")
    g["ffn_w_down"] = matmul(s["gact"], dyd, "tn", F32, tag + "w_down_dw")
    du0, g["ffn_conv_w"], dcb = convglu_bwd(s["u0"], w["ffn_conv_w"], w["ffn_conv_b"], dgact, tag + "glu_b")
    g["ffn_conv_b"] = dcb[0]
    dh2 = matmul(du0, w["ffn_w_up"], "nt", F32, tag + "w_up_dx")
    g["ffn_w_up"] = matmul(s["h2"], du0, "tn", F32, tag + "w_up_dw")
    dxm, sums2 = modnorm_bwd(s["xm"], w["norm2_g"], mod[3], mod[4], dh2, dxo, tag + "norm2_b")
    g["norm2_g"] = sums2[0]
    dy, dg1 = resid_bwd(dxm, s["y"], mod[2], tag + "res1_b")
    dycat = matmul(dy, w["w_out"], "nt", F32, tag + "w_out_dx")
    g["w_out"] = matmul(s["ycat"], dy, "tn", F32, tag + "w_out_dw")
    proj = s["proj"]
    dz, dxbc, ddt, g["ssd_conv_w"], vec = ssd_bwd(proj, s["hs"], dycat[:, 0:512], w["ssd_conv_w"], w["ssd_conv_b"], w["ssd_dt_bias"],
                                                 w["ssd_a_log"], w["ssd_d"], w["ssd_norm_g"], tag + "ssd_b")
    g["ssd_conv_b"], g["ssd_dt_bias"], g["ssd_a_log"], g["ssd_d"], g["ssd_norm_g"] = vec[0], vec[1, :8], vec[2, :8], vec[3, :8], vec[4, :512]
    do = jnp.transpose(dycat[:, 512:768].reshape(t, MLA_HEADS, 64), (1, 0, 2))
    dq, dk, dv = mla_flash_bwd(s["q"], s["k"], s["v"], s["o"], s["lse"], do, tag + "mla_attn_b")
    dcq, dckv, dkr, g["mla_w_uq"], g["mla_w_ukv"], mvec = mla_pre_bwd(proj, cos, sin, w["mla_q_norm_g"], w["mla_kv_norm_g"],
                                                                    w["mla_w_uq"], w["mla_w_ukv"], dq, dk, dv, tag + "mla_pre_b")
    g["mla_q_norm_g"], g["mla_kv_norm_g"] = mvec[0], mvec[1, :128]
    dsq, dsk, dsv, dsink = swa_bwd(proj, w["swa_sinks"], dycat[:, 768:1024], tag + "swa_b")
    g["swa_sinks"] = dsink[0, :4]
    dproj = jnp.concatenate([dxbc, dz, dcq, dsq, dckv, ddt, dkr, dsk, dsv], axis=1).astype(BF16)
    dh1 = matmul(dproj, w["w_in"], "nt", F32, tag + "w_in_dx")
    g["w_in"] = matmul(s["h1"], dproj, "tn", F32, tag + "w_in_dw")
    dx, sums1 = modnorm_bwd(s["x"], w["norm1_g"], mod[0], mod[1], dh1, dxm, tag + "norm1_b")
    g["norm1_g"] = sums1[0]
    dmod = jnp.stack([sums1[1], sums1[2], dg1[0], sums2[1], sums2[2], dg2[0]])
    return dx, dmod, g


def local_step(x, tgt, mods, ws, final_norm_g, cos, sin):
    saved = []
    for l in range(len(ws)):
        x, s = layer_fwd(x, mods[l], ws[l], cos, sin, f"l{l}_")
        saved.append(s)
    loss, dx, dfg = loss_fwd_bwd(x, final_norm_g, tgt, "loss")
    dmods, grads = [None] * len(ws), [None] * len(ws)
    for l in reversed(range(len(ws))):
        dx, dmods[l], grads[l] = layer_bwd(dx, saved[l], mods[l], ws[l], cos, sin, f"l{l}_")
    return loss, dx, dfg[0], jnp.stack(dmods), grads


WEIGHTS = ("ada_w", "ada_b", "norm1_g", "norm2_g", "w_in", "ssd_conv_w", "ssd_conv_b", "ssd_dt_bias", "ssd_a_log", "ssd_d",
           "ssd_norm_g", "mla_q_norm_g", "mla_w_uq", "mla_kv_norm_g", "mla_w_ukv", "swa_sinks", "w_out", "ffn_w_up",
           "ffn_conv_w", "ffn_conv_b", "ffn_w_down", "final_norm_g")
BIG = (("w_in", 2), ("w_out", 1), ("ffn_w_up", 2), ("ffn_w_down", 1), ("mla_w_uq", 2), ("mla_w_ukv", 2))
SMALL = (("dmod", 6 * D), ("norm1_g", D), ("norm2_g", D), ("ssd_conv_w", 4 * SSD_XBC), ("ssd_conv_b", SSD_XBC), ("ssd_dt_bias", 128),
         ("ssd_a_log", 128), ("ssd_d", 128), ("ssd_norm_g", SSD_INNER), ("mla_q_norm_g", 256), ("mla_kv_norm_g", 128),
         ("swa_sinks", 128), ("ffn_conv_w", 3 * 2 * D_FF), ("ffn_conv_b", 2 * D_FF))
SMALL_LAYER = sum(n for _, n in SMALL)
SMALL_SHAPES = {"norm1_g": (D,), "norm2_g": (D,), "ssd_conv_w": (4, SSD_XBC), "ssd_conv_b": (SSD_XBC,), "ssd_dt_bias": (8,),
                "ssd_a_log": (8,), "ssd_d": (8,), "ssd_norm_g": (SSD_INNER,), "mla_q_norm_g": (256,), "mla_kv_norm_g": (128,),
                "swa_sinks": (4,), "ffn_conv_w": (3, 2 * D_FF), "ffn_conv_b": (2 * D_FF,)}


def _lanes(v, n):
    v = v.reshape(-1)
    return jnp.pad(v, (0, n - v.shape[0]))


def _tile_rows(flat):
    n = -(-flat.shape[0] // 1024) * 1024
    return jnp.pad(flat, (0, n - flat.shape[0])).reshape(-1, 128)


def _rope_tables(positions):
    inv_freq = 10000.0 ** (-jnp.arange(0, 32, 2, dtype=F32) / 32)
    ang = positions.astype(F32).reshape(-1, 1) * inv_freq
    return jnp.cos(ang), jnp.sin(ang)


def _gather_big(shards, c):
    rb = sum(s.size for s in shards) // 256
    flat = jnp.concatenate([s.astype(BF16).reshape(-1, 128) for s in shards], axis=0).reshape(2, rb, 128)
    got = allgather8(lax.dynamic_index_in_dim(flat, c, 0, keepdims=False), "ag_weights", False).reshape(N_CHIP, 2 * rb, 128)
    fulls, off = [], 0
    for (_, axis), s in zip(BIG, shards):
        rows = s.size // 128
        fulls.append(jnp.concatenate([got[k, off:off + rows].reshape(s.shape) for k in range(N_CHIP)], axis=axis))
        off += rows
    return fulls


def _reduce_big(grads, x, y, c):
    chips = []
    for k in range(N_CHIP):
        parts = []
        for (_, axis), g in zip(BIG, grads):
            n = g.shape[axis] // N_CHIP
            parts.append(lax.slice_in_dim(g, k * n, (k + 1) * n, axis=axis).astype(BF16).reshape(-1, 128))
        chips.append(jnp.concatenate(parts, axis=0))
    rb = chips[0].shape[0] // 2
    halves = jnp.stack(chips).reshape(N_CHIP, 2, rb, 128).transpose(1, 0, 2, 3).reshape(2, N_CHIP * rb, 128)
    theirs = flip_exchange(halves, [((0, 0, 1), lambda x, y, c: 1 - c, lambda x, y, c: 0)], 1, "rs_sibling")
    mine = lax.dynamic_index_in_dim(halves, c, 0, keepdims=False)
    chip_sum = add_pairs(mine, theirs[0], BF16, "rs_add_sibling").reshape(N_CHIP, rb, 128)
    plan = [((fx, fy, 0), (lambda x, y, c, fx=fx, fy=fy: 2 * (x ^ fx) + (y ^ fy)), (lambda x, y, c, s=s: s))
            for s, (fx, fy) in enumerate(CHIP_FLIPS)]
    others = flip_exchange(chip_sum, plan, len(CHIP_FLIPS), "rs_chips")
    own = lax.dynamic_index_in_dim(chip_sum, 2 * x + y, 0, keepdims=False)
    half = add_slots(own, others, "rs_add_chips")
    other = flip_exchange(half[None], [((0, 0, 1), lambda x, y, c: 0, lambda x, y, c: 0)], 1, "rs_share")[0]
    flat = jnp.where(c == 0, jnp.concatenate([half, other], axis=0), jnp.concatenate([other, half], axis=0))
    out, off = [], 0
    for (_, axis), g in zip(BIG, grads):
        shape = list(g.shape)
        shape[axis] //= N_CHIP
        rows = math.prod(shape) // 128
        out.append(flat[off:off + rows].reshape(shape))
        off += rows
    return out


def kernel(x, c, positions, ada_w, ada_b, norm1_g, norm2_g, w_in, ssd_conv_w, ssd_conv_b, ssd_dt_bias, ssd_a_log, ssd_d, ssd_norm_g, mla_q_norm_g, mla_w_uq, mla_kv_norm_g, mla_w_ukv, swa_sinks, w_out, ffn_w_up, ffn_conv_w, ffn_conv_b, ffn_w_down, final_norm_g, loss_target, m_ada_w, m_ada_b, m_norm1_g, m_norm2_g, m_w_in, m_ssd_conv_w, m_ssd_conv_b, m_ssd_dt_bias, m_ssd_a_log, m_ssd_d, m_ssd_norm_g, m_mla_q_norm_g, m_mla_w_uq, m_mla_kv_norm_g, m_mla_w_ukv, m_swa_sinks, m_w_out, m_ffn_w_up, m_ffn_conv_w, m_ffn_conv_b, m_ffn_w_down, m_final_norm_g, v_ada_w, v_ada_b, v_norm1_g, v_norm2_g, v_w_in, v_ssd_conv_w, v_ssd_conv_b, v_ssd_dt_bias, v_ssd_a_log, v_ssd_d, v_ssd_norm_g, v_mla_q_norm_g, v_mla_w_uq, v_mla_kv_norm_g, v_mla_w_ukv, v_swa_sinks, v_w_out, v_ffn_w_up, v_ffn_conv_w, v_ffn_conv_b, v_ffn_w_down, v_final_norm_g):
    args = locals()
    wt = {n: args[n] for n in WEIGHTS}
    mx, my, mc = _place()
    chip = 2 * mx + my
    dev = 2 * chip + mc
    n_ada = ada_w.shape[2]

    pack = _tile_rows(jnp.concatenate([c.reshape(-1), ssd_conv_w.reshape(-1), ffn_conv_w.reshape(-1)]))
    got = allgather8(pack, "ag_small_in", True).reshape(N_DEV, -1)
    c_all = got[:, :D]
    per_chip = got[0::2]
    n_scw = ssd_conv_w.size
    ssd_cw = jnp.concatenate([per_chip[k, D:D + n_scw].reshape(ssd_conv_w.shape) for k in range(N_CHIP)], axis=2)
    n_fcw = ffn_conv_w.size
    ffn_cw = jnp.concatenate([per_chip[k, D + n_scw:D + n_scw + n_fcw].reshape(ffn_conv_w.shape) for k in range(N_CHIP)], axis=2)

    ada_b_cols = lax.dynamic_slice_in_dim(ada_b, chip * n_ada, n_ada, axis=1)
    mod_cols = ada_mod(c_all, ada_w, ada_b_cols, "ada_mod")
    mod_all = allgather8(mod_cols.reshape(-1, 128), "ag_mod", True)[0::2].reshape(N_CHIP, DEPTH, N_DEV, n_ada)
    mods = lax.dynamic_index_in_dim(mod_all, dev, 2, keepdims=False).transpose(1, 0, 2).reshape(DEPTH, 6, D)

    full = dict(zip([n for n, _ in BIG], _gather_big([wt[n] for n, _ in BIG], mc)))
    full["w_in"] = pack_w_in(full["w_in"])
    full["mla_w_uq"] = pack_w_uq(full["mla_w_uq"])
    ws = []
    for l in range(DEPTH):
        w = {n: full[n][l] for n, _ in BIG}
        w.update(ssd_conv_w=ssd_cw[l], ffn_conv_w=ffn_cw[l])
        for n in ("norm1_g", "norm2_g", "ssd_conv_b", "ssd_dt_bias", "ssd_a_log", "ssd_d", "ssd_norm_g", "mla_q_norm_g",
                  "mla_kv_norm_g", "swa_sinks", "ffn_conv_b"):
            w[n] = wt[n][l]
        ws.append(w)

    cos, sin = _rope_tables(positions)
    t = x.shape[1]
    loss8, dx, dfg, dmods, lg = local_step(x.reshape(t, D), loss_target.reshape(t, D), mods, ws, final_norm_g, cos, sin)
    loss = lax.psum(loss8[0, 0], ("x", "y", "c"))

    rows = []
    for l in range(DEPTH):
        for name, n in SMALL:
            rows.append(_lanes(dmods[l] if name == "dmod" else lg[l][name], n))
    rows.append(dfg)
    small = allgather8(_tile_rows(jnp.concatenate(rows)), "ag_small_grads", True)
    total = sum8(small, "sum_small_grads").reshape(-1)
    grads = {}
    per_layer = {name: [] for name, _ in SMALL}
    for l in range(DEPTH):
        off = l * SMALL_LAYER
        for name, n in SMALL:
            per_layer[name].append(total[off:off + n])
            off += n
    grads["ada_b"] = jnp.stack(per_layer["dmod"])
    for name, shape in SMALL_SHAPES.items():
        grads[name] = jnp.stack([v[:math.prod(shape)].reshape(shape) for v in per_layer[name]])
    grads["final_norm_g"] = total[DEPTH * SMALL_LAYER:DEPTH * SMALL_LAYER + D]
    for name in ("ssd_conv_w", "ffn_conv_w"):
        n = grads[name].shape[2] // N_CHIP
        grads[name] = lax.dynamic_slice_in_dim(grads[name], chip * n, n, axis=2)
    dmod_all = small.reshape(N_DEV, -1)[:, :DEPTH * SMALL_LAYER].reshape(N_DEV, DEPTH, SMALL_LAYER)[:, :, :6 * D]
    dmod_cols = lax.dynamic_slice_in_dim(dmod_all, chip * n_ada, n_ada, axis=2).transpose(1, 0, 2)
    grads["ada_w"] = ada_grad(c_all, dmod_cols, "ada_grad")

    stacked = []
    for name, _ in BIG:
        g = jnp.stack([lg[l][name] for l in range(DEPTH)])
        if name == "w_in":
            g = unpack_w_in(g)
        if name == "mla_w_uq":
            g = unpack_w_uq(g)
        stacked.append(g)
    for (name, _), g in zip(BIG, _reduce_big(stacked, mx, my, mc)):
        grads[name] = g

    deltas, new_m, new_v = {}, {}, {}
    for n in WEIGHTS:
        deltas[n], new_m[n], new_v[n] = adamw(wt[n], grads[n], args["m_" + n], args["v_" + n], "adamw_" + n)
    return (loss, dx.reshape(x.shape), *[grads[n] for n in WEIGHTS], *[deltas[n] for n in WEIGHTS],
            *[new_m[n] for n in WEIGHTS], *[new_v[n] for n in WEIGHTS])
```

```python
import functools
import math

import jax
import jax.numpy as jnp
from jax import lax
from jax.experimental import pallas as pl
from jax.experimental.pallas import tpu as pltpu

F32 = jnp.float32
BF16 = jnp.bfloat16
MXU_DTYPE = BF16

D = 1024
DEPTH = 4
EPS = 1e-6
N_DEV = 8
N_CHIP = 4

SSD_HEADS = 8
SSD_INNER = 512
SSD_STATE = 128
SSD_XBC = 1024
Q = 128
MLA_HEADS = 4
MLA_QK = 96
D_FF = 2816
D_IN = 2472

P_XBC, P_Z, P_CQ, P_SQ, P_CKV, P_DT, P_KR, P_SK, P_SV = 0, 1024, 1536, 1792, 2048, 2176, 2304, 2432, 2560
NP = 2688
_PACK = ((P_Z, 0, 512), (P_XBC, 512, 1024), (P_DT, 1536, 8), (P_CQ, 1544, 256), (P_CKV, 1800, 128),
         (P_KR, 1928, 32), (P_SQ, 1960, 256), (P_SK, 2216, 128), (P_SV, 2344, 128))

ADAM_LR, ADAM_B1, ADAM_B2, ADAM_EPS, ADAM_WD, ADAM_STEP = 0.001, 0.9, 0.999, 1e-08, 0.01, 10

VMEM_LIMIT = 56 * 1024 * 1024
NEG = -1e30


def _cp(sem=None):
    return pltpu.CompilerParams(dimension_semantics=sem, vmem_limit_bytes=VMEM_LIMIT)


def _dot(a, b, dims):
    return lax.dot_general(a.astype(MXU_DTYPE), b.astype(MXU_DTYPE), (dims, ((), ())), preferred_element_type=F32)


_NN = ((1,), (0,))
_NT = ((1,), (1,))
_TN = ((0,), (0,))


@jax.custom_vjp
def mm(a, b):
    return _dot(a, b, _NN)


mm.defvjp(lambda a, b: (_dot(a, b, _NN), (a, b)),
          lambda r, g: (_dot(g, r[1], _NT), _dot(r[0], g, _TN)))


@jax.custom_vjp
def mm_nt(a, b):
    return _dot(a, b, _NT)


mm_nt.defvjp(lambda a, b: (_dot(a, b, _NT), (a, b)),
             lambda r, g: (_dot(g, r[1], _NN), _dot(g, r[0], _TN)))


@jax.custom_vjp
def mm_tn(a, b):
    return _dot(a, b, _TN)


mm_tn.defvjp(lambda a, b: (_dot(a, b, _TN), (a, b)),
             lambda r, g: (_dot(r[1], g, _NT), _dot(r[0], g, _NN)))


def _silu(x):
    return x * jax.nn.sigmoid(x)


def _rms(x, g):
    return x * lax.rsqrt(jnp.mean(x * x, axis=-1, keepdims=True) + EPS) * g


def _modnorm(x, g, sh, sc):
    return _rms(x, g) * (1.0 + sc) + sh


def _blk(dim, target, mult=128):
    best = None
    for b in range(mult, min(dim, target) + 1, mult):
        if dim % b == 0:
            best = b
    return best if best is not None else dim


def matmul(a, b, mode, out_dtype, name):
    if mode == "nn":
        (m, k), n = a.shape, b.shape[1]
    elif mode == "nt":
        (m, k), n = a.shape, b.shape[0]
    else:
        (k, m), n = a.shape, b.shape[1]
    bm, bn, bk = _blk(m, 512), _blk(n, 1408), _blk(k, 2816)
    nk = k // bk
    dims = {"nn": _NN, "nt": _NT, "tn": _TN}[mode]

    def body(a_ref, b_ref, o_ref, acc_ref):
        kk = pl.program_id(2)
        part = _dot(a_ref[...], b_ref[...], dims)
        if nk == 1:
            o_ref[...] = part.astype(o_ref.dtype)
            return

        @pl.when(kk == 0)
        def _():
            acc_ref[...] = part

        @pl.when((kk > 0) & (kk < nk - 1))
        def _():
            acc_ref[...] += part

        @pl.when(kk == nk - 1)
        def _():
            o_ref[...] = (acc_ref[...] + part).astype(o_ref.dtype)

    a_spec = pl.BlockSpec((bk, bm), lambda i, j, kk: (kk, i)) if mode == "tn" else pl.BlockSpec((bm, bk), lambda i, j, kk: (i, kk))
    b_spec = pl.BlockSpec((bn, bk), lambda i, j, kk: (j, kk)) if mode == "nt" else pl.BlockSpec((bk, bn), lambda i, j, kk: (kk, j))
    return pl.pallas_call(
        body, name=name, out_shape=jax.ShapeDtypeStruct((m, n), out_dtype), grid=(m // bm, n // bn, nk),
        in_specs=[a_spec, b_spec], out_specs=pl.BlockSpec((bm, bn), lambda i, j, kk: (i, j)),
        scratch_shapes=[pltpu.VMEM((bm, bn), F32)], compiler_params=_cp(("parallel", "parallel", "arbitrary")),
    )(a, b)


TM = 512


def _row(v):
    return v.reshape(1, -1)


def modnorm_fwd(x, g, sh, sc, name):
    t = x.shape[0]

    def body(x_ref, g_ref, sh_ref, sc_ref, o_ref):
        o_ref[...] = _modnorm(x_ref[...], g_ref[...], sh_ref[...], sc_ref[...]).astype(o_ref.dtype)

    vec = pl.BlockSpec((1, D), lambda i: (0, 0))
    return pl.pallas_call(
        body, name=name, out_shape=jax.ShapeDtypeStruct((t, D), BF16), grid=(t // TM,),
        in_specs=[pl.BlockSpec((TM, D), lambda i: (i, 0)), vec, vec, vec],
        out_specs=pl.BlockSpec((TM, D), lambda i: (i, 0)), compiler_params=_cp(("parallel",)),
    )(x, _row(g), _row(sh), _row(sc))


def modnorm_bwd(x, g, sh, sc, dh, dres, name):
    t = x.shape[0]

    def body(x_ref, g_ref, sh_ref, sc_ref, dh_ref, dres_ref, dx_ref, sums_ref):
        _, vjp = jax.vjp(_modnorm, x_ref[...], g_ref[...], sh_ref[...], sc_ref[...])
        dx, dg, dsh, dsc = vjp(dh_ref[...].astype(F32))
        dx_ref[...] = dx + dres_ref[...]

        @pl.when(pl.program_id(0) == 0)
        def _():
            sums_ref[...] = jnp.zeros_like(sums_ref)

        sums_ref[0:1, :] += dg
        sums_ref[1:2, :] += dsh
        sums_ref[2:3, :] += dsc

    vec = pl.BlockSpec((1, D), lambda i: (0, 0))
    tile = pl.BlockSpec((TM, D), lambda i: (i, 0))
    return pl.pallas_call(
        body, name=name, out_shape=(jax.ShapeDtypeStruct((t, D), F32), jax.ShapeDtypeStruct((8, D), F32)), grid=(t // TM,),
        in_specs=[tile, vec, vec, vec, tile, tile], out_specs=(tile, pl.BlockSpec((8, D), lambda i: (0, 0))),
        compiler_params=_cp(("arbitrary",)),
    )(x, _row(g), _row(sh), _row(sc), dh, dres)


def resid_fwd(x, y, gate, name):
    t = x.shape[0]

    def body(x_ref, y_ref, g_ref, o_ref):
        o_ref[...] = x_ref[...] + g_ref[...] * y_ref[...]

    tile = pl.BlockSpec((TM, D), lambda i: (i, 0))
    return pl.pallas_call(
        body, name=name, out_shape=jax.ShapeDtypeStruct((t, D), F32), grid=(t // TM,),
        in_specs=[tile, tile, pl.BlockSpec((1, D), lambda i: (0, 0))], out_specs=tile, compiler_params=_cp(("parallel",)),
    )(x, y, _row(gate))


def resid_bwd(dxo, y, gate, name):
    t = dxo.shape[0]

    def body(d_ref, y_ref, g_ref, dy_ref, dg_ref):
        d = d_ref[...]
        dy_ref[...] = (d * g_ref[...]).astype(BF16)

        @pl.when(pl.program_id(0) == 0)
        def _():
            dg_ref[...] = jnp.zeros_like(dg_ref)

        dg_ref[0:1, :] += jnp.sum(d * y_ref[...], axis=0, keepdims=True)

    tile = pl.BlockSpec((TM, D), lambda i: (i, 0))
    return pl.pallas_call(
        body, name=name, out_shape=(jax.ShapeDtypeStruct((t, D), BF16), jax.ShapeDtypeStruct((8, D), F32)), grid=(t // TM,),
        in_specs=[tile, tile, pl.BlockSpec((1, D), lambda i: (0, 0))], out_specs=(tile, pl.BlockSpec((8, D), lambda i: (0, 0))),
        compiler_params=_cp(("arbitrary",)),
    )(dxo, y, _row(gate))


CW = 256
NCW = D_FF // CW


def _conv3(u, halo, w, b):
    n = u.shape[0]
    win = jnp.concatenate([halo, u], axis=0)
    return b + w[0:1] * win[6:6 + n] + w[1:2] * win[7:7 + n] + w[2:3] * win[8:8 + n]


def _convglu(ua, ub, ha, hb, wa, wb, ba, bb):
    return _silu(_conv3(ua, ha, wa, ba)) * _conv3(ub, hb, wb, bb)


def _halo_specs(tm, cw, off):
    r = tm // 8
    return pl.BlockSpec((8, cw), lambda j, i, o=off: (jnp.maximum(i * r - 1, 0), j + o))


def convglu_fwd(u0, cw, cb, name):
    t = u0.shape[0]

    def body(ua_ref, ub_ref, ha_ref, hb_ref, wa_ref, wb_ref, ba_ref, bb_ref, o_ref):
        keep = (pl.program_id(1) > 0).astype(F32)
        o_ref[...] = _convglu(ua_ref[...], ub_ref[...], ha_ref[...] * keep, hb_ref[...] * keep,
                              wa_ref[...], wb_ref[...], ba_ref[...], bb_ref[...]).astype(o_ref.dtype)

    def col(rows, off):
        return pl.BlockSpec((rows, CW), lambda j, i, o=off: (0, j + o))

    return pl.pallas_call(
        body, name=name, out_shape=jax.ShapeDtypeStruct((t, D_FF), BF16), grid=(NCW, t // TM),
        in_specs=[pl.BlockSpec((TM, CW), lambda j, i: (i, j)), pl.BlockSpec((TM, CW), lambda j, i: (i, j + NCW)),
                  _halo_specs(TM, CW, 0), _halo_specs(TM, CW, NCW), col(3, 0), col(3, NCW), col(1, 0), col(1, NCW)],
        out_specs=pl.BlockSpec((TM, CW), lambda j, i: (i, j)), compiler_params=_cp(("parallel", "parallel")),
    )(u0, u0, u0, u0, cw, cw, _row(cb), _row(cb))


def convglu_bwd(u0, cw, cb, dgact, name):
    t = u0.shape[0]
    nt = t // TM

    def body(ua_ref, ub_ref, ha_ref, hb_ref, wa_ref, wb_ref, ba_ref, bb_ref, dg_ref,
             dua_ref, dub_ref, dwa_ref, dwb_ref, dba_ref, dbb_ref, ca_ref, cb_ref):
        step = pl.program_id(1)
        keep = (step < nt - 1).astype(F32)

        @pl.when(step == 0)
        def _():
            ca_ref[...] = jnp.zeros_like(ca_ref)
            cb_ref[...] = jnp.zeros_like(cb_ref)
            dwa_ref[...] = jnp.zeros_like(dwa_ref)
            dwb_ref[...] = jnp.zeros_like(dwb_ref)
            dba_ref[...] = jnp.zeros_like(dba_ref)
            dbb_ref[...] = jnp.zeros_like(dbb_ref)

        _, vjp = jax.vjp(_convglu, ua_ref[...], ub_ref[...], ha_ref[...] * keep, hb_ref[...] * keep,
                         wa_ref[...], wb_ref[...], ba_ref[...], bb_ref[...])
        dua, dub, dha, dhb, dwa, dwb, dba, dbb = vjp(dg_ref[...].astype(F32))
        zeros = jnp.zeros((TM - 8, CW), F32)
        dua_ref[...] = (dua + jnp.concatenate([zeros, ca_ref[...]], axis=0)).astype(BF16)
        dub_ref[...] = (dub + jnp.concatenate([zeros, cb_ref[...]], axis=0)).astype(BF16)
        ca_ref[...] = dha * keep
        cb_ref[...] = dhb * keep
        dwa_ref[...] += dwa
        dwb_ref[...] += dwb
        dba_ref[...] += dba
        dbb_ref[...] += dbb

    def rev(i):
        return nt - 1 - i

    def tile(off):
        return pl.BlockSpec((TM, CW), lambda j, i, o=off: (rev(i), j + o))

    def halo(off):
        r = TM // 8
        return pl.BlockSpec((8, CW), lambda j, i, o=off: (jnp.maximum(rev(i) * r - 1, 0), j + o))

    def col(rows, off):
        return pl.BlockSpec((rows, CW), lambda j, i, o=off: (0, j + o))

    outs = pl.pallas_call(
        body, name=name,
        out_shape=(jax.ShapeDtypeStruct((t, D_FF), BF16), jax.ShapeDtypeStruct((t, D_FF), BF16),
                   jax.ShapeDtypeStruct((3, D_FF), F32), jax.ShapeDtypeStruct((3, D_FF), F32),
                   jax.ShapeDtypeStruct((1, D_FF), F32), jax.ShapeDtypeStruct((1, D_FF), F32)),
        grid=(NCW, nt),
        in_specs=[tile(0), tile(NCW), halo(0), halo(NCW), col(3, 0), col(3, NCW), col(1, 0), col(1, NCW), tile(0)],
        out_specs=(tile(0), tile(0), col(3, 0), col(3, 0), col(1, 0), col(1, 0)),
        scratch_shapes=[pltpu.VMEM((8, CW), F32), pltpu.VMEM((8, CW), F32)],
        compiler_params=_cp(("parallel", "arbitrary")),
    )(u0, u0, u0, u0, cw, cw, _row(cb), _row(cb), dgact)
    dua, dub, dwa, dwb, dba, dbb = outs
    return (jnp.concatenate([dua, dub], axis=1), jnp.concatenate([dwa, dwb], axis=1), jnp.concatenate([dba, dbb], axis=1))


def _pick(v, h, axis):
    return v[:, h:h + 1] if axis == 1 else v[h:h + 1, :]


def _ssd_chunk(z, xh, xc, dtp, hin, cw, cb, dtb, alog, dsk, ng):
    lane_hi = lax.broadcasted_iota(jnp.int32, (Q, Q), 1) >= 64
    row_hi = lax.broadcasted_iota(jnp.int32, (Q, Q), 0) >= 64
    causal = lax.broadcasted_iota(jnp.int32, (Q, Q), 0) >= lax.broadcasted_iota(jnp.int32, (Q, Q), 1)
    win = jnp.concatenate([xh, xc], axis=0)
    xbc = cb
    for k in range(4):
        xbc = xbc + cw[k:k + 1] * win[5 + k:5 + k + Q]
    xbc = _silu(xbc)
    xs, bm, cm = xbc[:, 0:512], xbc[:, 512:768], xbc[:, 768:1024]
    dt = jax.nn.softplus(dtp + dtb)
    da = dt * (-jnp.exp(alog))
    ah = jnp.dot(causal.astype(F32), da, precision=lax.Precision.HIGHEST, preferred_element_type=F32)
    aht = ah.T
    alast = ah[Q - 1:Q, :]
    eah = jnp.exp(ah)
    dte = jnp.exp(alast - ah)
    elast = jnp.exp(alast)
    ys, houts = [], []
    for g in range(2):
        bg, cg = bm[:, 128 * g:128 * g + 128], cm[:, 128 * g:128 * g + 128]
        cbm = mm_nt(cg, bg)
        for jp in range(2):
            j = 2 * g + jp
            h0, h1 = 2 * j, 2 * j + 1
            xp = xs[:, 128 * j:128 * j + 128]
            xdt = xp * jnp.where(lane_hi, _pick(dt, h1, 1), _pick(dt, h0, 1))
            yd, st = [], []
            for h in (h0, h1):
                seg = _pick(ah, h, 1) - _pick(aht, h, 0)
                decay = jnp.exp(jnp.where(causal, seg, NEG))
                yd.append(mm(cbm * decay, xdt))
                st.append(mm_tn(xdt * _pick(dte, h, 1), bg))
            hj = hin[j]
            hout = hj * jnp.where(row_hi, _pick(elast, h1, 1), _pick(elast, h0, 1)) + jnp.where(row_hi, st[1], st[0])
            yoff = mm_nt(cg, hj) * jnp.where(lane_hi, _pick(eah, h1, 1), _pick(eah, h0, 1))
            skip = xp * jnp.where(lane_hi[0:1], _pick(dsk, h1, 1), _pick(dsk, h0, 1))
            ys.append(jnp.where(lane_hi, yd[1], yd[0]) + yoff + skip)
            houts.append(hout)
    y = jnp.concatenate(ys, axis=1) * _silu(z)
    yn = []
    for g in range(2):
        yg = y[:, 256 * g:256 * g + 256]
        yn.append(yg * lax.rsqrt(jnp.mean(yg * yg, axis=-1, keepdims=True) + EPS))
    return jnp.concatenate(yn, axis=1) * ng, jnp.stack(houts)


def _pad_lanes(v, n=128):
    v = v.reshape(1, -1)
    return jnp.pad(v, ((0, 0), (0, n - v.shape[1])))


def _ssd_in_specs(chunk_of):
    return [pl.BlockSpec((Q, 512), lambda i: (chunk_of(i), P_Z // 512)),
            pl.BlockSpec((8, 1024), lambda i: (jnp.maximum(chunk_of(i) * (Q // 8) - 1, 0), P_XBC // 1024)),
            pl.BlockSpec((Q, 1024), lambda i: (chunk_of(i), P_XBC // 1024)),
            pl.BlockSpec((Q, 128), lambda i: (chunk_of(i), P_DT // 128))]


def _full(shape):
    nd = len(shape)
    return pl.BlockSpec(shape, lambda i: (0,) * nd)


def ssd_fwd(proj, cw, cb, dtb, alog, dsk, ng, name):
    t = proj.shape[0]
    nc = t // Q

    def body(z_ref, xh_ref, xc_ref, dt_ref, cw_ref, cb_ref, dtb_ref, al_ref, dsk_ref, ng_ref, y_ref, hs_ref, h_ref):
        i = pl.program_id(0)

        @pl.when(i == 0)
        def _():
            h_ref[...] = jnp.zeros_like(h_ref)

        hin = h_ref[...]
        hs_ref[0] = hin
        y, hout = _ssd_chunk(z_ref[...], xh_ref[...] * (i > 0).astype(F32), xc_ref[...], dt_ref[...], hin,
                             cw_ref[...], cb_ref[...], dtb_ref[...], al_ref[...], dsk_ref[...], ng_ref[...])
        y_ref[...] = y
        h_ref[...] = hout

    return pl.pallas_call(
        body, name=name,
        out_shape=(jax.ShapeDtypeStruct((t, 512), F32), jax.ShapeDtypeStruct((nc, 4, 128, 128), F32)), grid=(nc,),
        in_specs=_ssd_in_specs(lambda i: i) + [_full((4, 1024)), _full((1, 1024)), _full((1, 128)), _full((1, 128)),
                                               _full((1, 128)), _full((1, 512))],
        out_specs=(pl.BlockSpec((Q, 512), lambda i: (i, 0)), pl.BlockSpec((1, 4, 128, 128), lambda i: (i, 0, 0, 0))),
        scratch_shapes=[pltpu.VMEM((4, 128, 128), F32)], compiler_params=_cp(("arbitrary",)),
    )(proj, proj, proj, proj, cw, _row(cb), _pad_lanes(dtb), _pad_lanes(alog), _pad_lanes(dsk), _row(ng))


def ssd_bwd(proj, hs, dy, cw, cb, dtb, alog, dsk, ng, name):
    t = proj.shape[0]
    nc = t // Q

    def body(z_ref, xh_ref, xc_ref, dt_ref, hs_ref, dy_ref, cw_ref, cb_ref, dtb_ref, al_ref, dsk_ref, ng_ref,
             dz_ref, dx_ref, ddt_ref, dcw_ref, vec_ref, dh_ref, carry_ref):
        step = pl.program_id(0)
        keep = (step < nc - 1).astype(F32)

        @pl.when(step == 0)
        def _():
            dh_ref[...] = jnp.zeros_like(dh_ref)
            carry_ref[...] = jnp.zeros_like(carry_ref)
            dcw_ref[...] = jnp.zeros_like(dcw_ref)
            vec_ref[...] = jnp.zeros_like(vec_ref)

        _, vjp = jax.vjp(_ssd_chunk, z_ref[...], xh_ref[...] * keep, xc_ref[...], dt_ref[...], hs_ref[0],
                         cw_ref[...], cb_ref[...], dtb_ref[...], al_ref[...], dsk_ref[...], ng_ref[...])
        dz, dxh, dxc, ddt, dhin, dcw, dcb, ddtb, dal, ddsk, dng = vjp((dy_ref[...], dh_ref[...]))
        dz_ref[...] = dz
        dx_ref[...] = dxc + jnp.concatenate([jnp.zeros((Q - 8, 1024), F32), carry_ref[...]], axis=0)
        ddt_ref[...] = ddt
        carry_ref[...] = dxh * keep
        dh_ref[...] = dhin
        dcw_ref[...] += dcw
        vec_ref[0:1, :] += dcb
        vec_ref[1:2, 0:128] += ddtb
        vec_ref[2:3, 0:128] += dal
        vec_ref[3:4, 0:128] += ddsk
        vec_ref[4:5, 0:512] += dng

    def rev(i):
        return nc - 1 - i

    return pl.pallas_call(
        body, name=name,
        out_shape=(jax.ShapeDtypeStruct((t, 512), F32), jax.ShapeDtypeStruct((t, 1024), F32), jax.ShapeDtypeStruct((t, 128), F32),
                   jax.ShapeDtypeStruct((4, 1024), F32), jax.ShapeDtypeStruct((8, 1024), F32)),
        grid=(nc,),
        in_specs=_ssd_in_specs(rev) + [pl.BlockSpec((1, 4, 128, 128), lambda i: (rev(i), 0, 0, 0)),
                                       pl.BlockSpec((Q, 512), lambda i: (rev(i), 0)),
                                       _full((4, 1024)), _full((1, 1024)), _full((1, 128)), _full((1, 128)), _full((1, 128)),
                                       _full((1, 512))],
        out_specs=(pl.BlockSpec((Q, 512), lambda i: (rev(i), 0)), pl.BlockSpec((Q, 1024), lambda i: (rev(i), 0)),
                   pl.BlockSpec((Q, 128), lambda i: (rev(i), 0)), _full((4, 1024)), _full((8, 1024))),
        scratch_shapes=[pltpu.VMEM((4, 128, 128), F32), pltpu.VMEM((8, 1024), F32)], compiler_params=_cp(("arbitrary",)),
    )(proj, proj, proj, proj, hs, dy, cw, _row(cb), _pad_lanes(dtb), _pad_lanes(alog), _pad_lanes(dsk), _row(ng))


TA = 1024
MLA_SCALE = 1.0 / math.sqrt(MLA_QK)


def _rope(x1, x2, cos, sin):
    return x1 * cos - x2 * sin, x1 * sin + x2 * cos


def _mla_pre(cq, ckv, kr, cos, sin, qg, kvg, wuq, wukv):
    n = cq.shape[0]
    qh = mm(_rms(cq, qg), wuq)
    kv = mm(_rms(ckv, kvg), wukv)
    kr1, kr2 = _rope(kr[:, 0:16], kr[:, 16:32], cos, sin)
    pad = jnp.zeros((n, 32), F32)
    qs, ks, vs = [], [], []
    for h in range(MLA_HEADS):
        b = qh[:, 128 * h:128 * h + 128]
        q1, q2 = _rope(b[:, 64:80], b[:, 80:96], cos, sin)
        qs.append(jnp.concatenate([b[:, 0:64], q1, q2, pad], axis=1))
        ks.append(jnp.concatenate([kv[:, 128 * h:128 * h + 64], kr1, kr2, pad], axis=1))
        vs.append(kv[:, 128 * h + 64:128 * h + 128])
    return jnp.stack(qs), jnp.stack(ks), jnp.stack(vs)


def _mla_pre_specs():
    return [pl.BlockSpec((TM, 256), lambda i: (i, P_CQ // 256)), pl.BlockSpec((TM, 128), lambda i: (i, P_CKV // 128)),
            pl.BlockSpec((TM, 128), lambda i: (i, P_KR // 128)), pl.BlockSpec((TM, 16), lambda i: (i, 0)),
            pl.BlockSpec((TM, 16), lambda i: (i, 0)), _full((1, 256)), _full((1, 128)), _full((256, 512)), _full((128, 512))]


def _head_tile(w):
    return pl.BlockSpec((MLA_HEADS, TM, w), lambda i: (0, i, 0))


def mla_pre_fwd(proj, cos, sin, qg, kvg, wuq, wukv, name):
    t = proj.shape[0]

    def body(cq_ref, ckv_ref, kr_ref, cos_ref, sin_ref, qg_ref, kvg_ref, wuq_ref, wukv_ref, q_ref, k_ref, v_ref):
        q, k, v = _mla_pre(cq_ref[...], ckv_ref[...], kr_ref[...], cos_ref[...], sin_ref[...], qg_ref[...], kvg_ref[...],
                           wuq_ref[...], wukv_ref[...])
        q_ref[...] = q.astype(BF16)
        k_ref[...] = k.astype(BF16)
        v_ref[...] = v.astype(BF16)

    return pl.pallas_call(
        body, name=name,
        out_shape=(jax.ShapeDtypeStruct((MLA_HEADS, t, 128), BF16), jax.ShapeDtypeStruct((MLA_HEADS, t, 128), BF16),
                   jax.ShapeDtypeStruct((MLA_HEADS, t, 64), BF16)),
        grid=(t // TM,), in_specs=_mla_pre_specs(), out_specs=(_head_tile(128), _head_tile(128), _head_tile(64)),
        compiler_params=_cp(("parallel",)),
    )(proj, proj, proj, cos, sin, _row(qg), _row(kvg), wuq, wukv)


def mla_pre_bwd(proj, cos, sin, qg, kvg, wuq, wukv, dq, dk, dv, name):
    t = proj.shape[0]

    def body(cq_ref, ckv_ref, kr_ref, cos_ref, sin_ref, qg_ref, kvg_ref, wuq_ref, wukv_ref, dq_ref, dk_ref, dv_ref,
             dcq_ref, dckv_ref, dkr_ref, dwuq_ref, dwukv_ref, vec_ref):
        @pl.when(pl.program_id(0) == 0)
        def _():
            dwuq_ref[...] = jnp.zeros_like(dwuq_ref)
            dwukv_ref[...] = jnp.zeros_like(dwukv_ref)
            vec_ref[...] = jnp.zeros_like(vec_ref)

        cos, sin = cos_ref[...], sin_ref[...]
        f = lambda cq, ckv, kr, qg, kvg, wuq, wukv: _mla_pre(cq, ckv, kr, cos, sin, qg, kvg, wuq, wukv)
        _, vjp = jax.vjp(f, cq_ref[...], ckv_ref[...], kr_ref[...], qg_ref[...], kvg_ref[...], wuq_ref[...], wukv_ref[...])
        dcq, dckv, dkr, dqg, dkvg, dwuq, dwukv = vjp((dq_ref[...], dk_ref[...], dv_ref[...]))
        dcq_ref[...] = dcq
        dckv_ref[...] = dckv
        dkr_ref[...] = dkr
        dwuq_ref[...] += dwuq
        dwukv_ref[...] += dwukv
        vec_ref[0:1, :] += dqg
        vec_ref[1:2, 0:128] += dkvg

    return pl.pallas_call(
        body, name=name,
        out_shape=(jax.ShapeDtypeStruct((t, 256), F32), jax.ShapeDtypeStruct((t, 128), F32), jax.ShapeDtypeStruct((t, 128), F32),
                   jax.ShapeDtypeStruct((256, 512), F32), jax.ShapeDtypeStruct((128, 512), F32), jax.ShapeDtypeStruct((8, 256), F32)),
        grid=(t // TM,), in_specs=_mla_pre_specs() + [_head_tile(128), _head_tile(128), _head_tile(64)],
        out_specs=(pl.BlockSpec((TM, 256), lambda i: (i, 0)), pl.BlockSpec((TM, 128), lambda i: (i, 0)),
                   pl.BlockSpec((TM, 128), lambda i: (i, 0)), _full((256, 512)), _full((128, 512)), _full((8, 256))),
        compiler_params=_cp(("arbitrary",)),
    )(proj, proj, proj, cos, sin, _row(qg), _row(kvg), wuq, wukv, dq, dk, dv)


def _causal_mask(i, j):
    qpos = i * TA + lax.broadcasted_iota(jnp.int32, (TA, TA), 0)
    kpos = j * TA + lax.broadcasted_iota(jnp.int32, (TA, TA), 1)
    return kpos <= qpos


def mla_flash_fwd(q, k, v, name):
    h, t, _ = q.shape

    def body(q_ref, k_ref, v_ref, o_ref, lse_ref, m_ref, l_ref, acc_ref):
        i = pl.program_id(1)
        m_ref[...] = jnp.full_like(m_ref, NEG)
        l_ref[...] = jnp.zeros_like(l_ref)
        acc_ref[...] = jnp.zeros_like(acc_ref)
        qb = q_ref[0]

        def step(j, carry, diagonal=False):
            rows = pl.ds(pl.multiple_of(j * TA, TA), TA)
            s = _dot(qb, k_ref[0, rows, :], _NT) * MLA_SCALE
            if diagonal:
                s = jnp.where(_causal_mask(i, j), s, NEG)
            m_new = jnp.maximum(m_ref[...], jnp.max(s, axis=-1, keepdims=True))
            p = jnp.exp(s - m_new)
            alpha = jnp.exp(m_ref[...] - m_new)
            l_ref[...] = alpha * l_ref[...] + jnp.sum(p, axis=-1, keepdims=True)
            acc_ref[...] = alpha * acc_ref[...] + _dot(p, v_ref[0, rows, :], _NN)
            m_ref[...] = m_new
            return carry

        lax.fori_loop(0, i, step, 0)
        step(i, 0, diagonal=True)
        o_ref[0] = acc_ref[...] / l_ref[...]
        lse_ref[0] = m_ref[...] + jnp.log(l_ref[...])

    return pl.pallas_call(
        body, name=name,
        out_shape=(jax.ShapeDtypeStruct((h, t, 64), F32), jax.ShapeDtypeStruct((h, t, 1), F32)), grid=(h, t // TA),
        in_specs=[pl.BlockSpec((1, TA, 128), lambda hh, i: (hh, i, 0)), pl.BlockSpec((1, t, 128), lambda hh, i: (hh, 0, 0)),
                  pl.BlockSpec((1, t, 64), lambda hh, i: (hh, 0, 0))],
        out_specs=(pl.BlockSpec((1, TA, 64), lambda hh, i: (hh, i, 0)), pl.BlockSpec((1, TA, 1), lambda hh, i: (hh, i, 0))),
        scratch_shapes=[pltpu.VMEM((TA, 1), F32), pltpu.VMEM((TA, 1), F32), pltpu.VMEM((TA, 64), F32)],
        compiler_params=_cp(("parallel", "parallel")),
    )(q, k, v)


def mla_flash_bwd(q, k, v, o, lse, do, name):
    h, t, _ = q.shape
    nb = t // TA

    def body(q_ref, k_ref, v_ref, o_ref, lse_ref, do_ref, dq_ref, dk_ref, dv_ref):
        j = pl.program_id(1)

        @pl.when(j == 0)
        def _():
            dq_ref[...] = jnp.zeros_like(dq_ref)

        dk_ref[...] = jnp.zeros_like(dk_ref)
        dv_ref[...] = jnp.zeros_like(dv_ref)
        kb, vb = k_ref[0], v_ref[0]

        def step(i, carry, diagonal=False):
            rows = pl.ds(pl.multiple_of(i * TA, TA), TA)
            qb, dob = q_ref[0, rows, :], do_ref[0, rows, :]
            s = _dot(qb, kb, _NT) * MLA_SCALE
            p = jnp.exp(s - lse_ref[0, rows, :])
            if diagonal:
                p = jnp.where(_causal_mask(i, j), p, 0.0)
            delta = jnp.sum(dob * o_ref[0, rows, :], axis=-1, keepdims=True)
            dv_ref[0] += _dot(p, dob, _TN)
            ds = p * (_dot(dob, vb, _NT) - delta) * MLA_SCALE
            dk_ref[0] += _dot(ds, qb, _TN)
            dq_ref[0, rows, :] += _dot(ds, kb, _NN)
            return carry

        step(j, 0, diagonal=True)
        lax.fori_loop(j + 1, nb, step, 0)

    def whole(w):
        return pl.BlockSpec((1, t, w), lambda hh, j: (hh, 0, 0))

    def blk(w):
        return pl.BlockSpec((1, TA, w), lambda hh, j: (hh, j, 0))

    return pl.pallas_call(
        body, name=name,
        out_shape=(jax.ShapeDtypeStruct((h, t, 128), F32), jax.ShapeDtypeStruct((h, t, 128), F32), jax.ShapeDtypeStruct((h, t, 64), F32)),
        grid=(h, nb), in_specs=[whole(128), blk(128), blk(64), whole(64), whole(1), whole(64)],
        out_specs=(whole(128), blk(128), blk(64)), compiler_params=_cp(("parallel", "arbitrary")),
    )(q, k, v, o, lse, do)


SWA_SCALE = 1.0 / 8.0


def _swa_block(q, kp, kc, vp, vc, sinks, has_prev):
    k2 = jnp.concatenate([kp, kc], axis=0)
    v2 = jnp.concatenate([vp, vc], axis=0)
    rel = Q + lax.broadcasted_iota(jnp.int32, (Q, 2 * Q), 0) - lax.broadcasted_iota(jnp.int32, (Q, 2 * Q), 1)
    valid = (rel >= 0) & (rel < Q) & ((lax.broadcasted_iota(jnp.int32, (Q, 2 * Q), 1) >= Q) | has_prev)
    outs = []
    for h in range(4):
        g = h // 2
        s = mm_nt(q[:, 64 * h:64 * h + 64], k2[:, 64 * g:64 * g + 64]) * SWA_SCALE
        s = jnp.where(valid, s, NEG)
        sink = sinks[:, h:h + 1]
        m = jnp.maximum(jnp.max(s, axis=-1, keepdims=True), sink)
        e = jnp.exp(s - m)
        p = e / (jnp.sum(e, axis=-1, keepdims=True) + jnp.exp(sink - m))
        outs.append(mm(p, v2[:, 64 * g:64 * g + 64]))
    return jnp.concatenate(outs, axis=1)


def _swa_specs(blk_of):
    def prev(i):
        return jnp.maximum(blk_of(i) - 1, 0)

    return [pl.BlockSpec((Q, 256), lambda i: (blk_of(i), P_SQ // 256)),
            pl.BlockSpec((Q, 128), lambda i: (prev(i), P_SK // 128)), pl.BlockSpec((Q, 128), lambda i: (blk_of(i), P_SK // 128)),
            pl.BlockSpec((Q, 128), lambda i: (prev(i), P_SV // 128)), pl.BlockSpec((Q, 128), lambda i: (blk_of(i), P_SV // 128)),
            _full((1, 128))]


def swa_fwd(proj, sinks, name):
    t = proj.shape[0]

    def body(q_ref, kp_ref, kc_ref, vp_ref, vc_ref, s_ref, o_ref):
        o_ref[...] = _swa_block(q_ref[...], kp_ref[...], kc_ref[...], vp_ref[...], vc_ref[...], s_ref[...], pl.program_id(0) > 0)

    return pl.pallas_call(
        body, name=name, out_shape=jax.ShapeDtypeStruct((t, 256), F32), grid=(t // Q,), in_specs=_swa_specs(lambda i: i),
        out_specs=pl.BlockSpec((Q, 256), lambda i: (i, 0)), compiler_params=_cp(("parallel",)),
    )(proj, proj, proj, proj, proj, _pad_lanes(sinks))


def swa_bwd(proj, sinks, do, name):
    t = proj.shape[0]
    nb = t // Q

    def body(q_ref, kp_ref, kc_ref, vp_ref, vc_ref, s_ref, do_ref, dq_ref, dk_ref, dv_ref, ds_ref, ck_ref, cv_ref):
        step = pl.program_id(0)

        @pl.when(step == 0)
        def _():
            ck_ref[...] = jnp.zeros_like(ck_ref)
            cv_ref[...] = jnp.zeros_like(cv_ref)
            ds_ref[...] = jnp.zeros_like(ds_ref)

        has_prev = step < nb - 1
        f = lambda q, kp, kc, vp, vc, s: _swa_block(q, kp, kc, vp, vc, s, has_prev)
        _, vjp = jax.vjp(f, q_ref[...], kp_ref[...], kc_ref[...], vp_ref[...], vc_ref[...], s_ref[...])
        dq, dkp, dkc, dvp, dvc, dsk = vjp(do_ref[...])
        dq_ref[...] = dq
        dk_ref[...] = dkc + ck_ref[...]
        dv_ref[...] = dvc + cv_ref[...]
        ck_ref[...] = dkp
        cv_ref[...] = dvp
        ds_ref[0:1, :] += dsk

    def rev(i):
        return nb - 1 - i

    return pl.pallas_call(
        body, name=name,
        out_shape=(jax.ShapeDtypeStruct((t, 256), F32), jax.ShapeDtypeStruct((t, 128), F32), jax.ShapeDtypeStruct((t, 128), F32),
                   jax.ShapeDtypeStruct((8, 128), F32)),
        grid=(nb,), in_specs=_swa_specs(rev) + [pl.BlockSpec((Q, 256), lambda i: (rev(i), 0))],
        out_specs=(pl.BlockSpec((Q, 256), lambda i: (rev(i), 0)), pl.BlockSpec((Q, 128), lambda i: (rev(i), 0)),
                   pl.BlockSpec((Q, 128), lambda i: (rev(i), 0)), _full((8, 128))),
        scratch_shapes=[pltpu.VMEM((Q, 128), F32), pltpu.VMEM((Q, 128), F32)], compiler_params=_cp(("arbitrary",)),
    )(proj, proj, proj, proj, proj, _pad_lanes(sinks), do)


def _loss_tile(x, g, tgt):
    err = jnp.square(_rms(x, g) - tgt)
    return 0.5 * jnp.sum(jnp.mean(err, axis=-1, keepdims=True), axis=0, keepdims=True)


def loss_fwd_bwd(x, g, tgt, name):
    t = x.shape[0]

    def body(x_ref, g_ref, t_ref, loss_ref, dx_ref, dg_ref):
        @pl.when(pl.program_id(0) == 0)
        def _():
            loss_ref[...] = jnp.zeros_like(loss_ref)
            dg_ref[...] = jnp.zeros_like(dg_ref)

        tgt = t_ref[...]
        val, vjp = jax.vjp(lambda x, g: _loss_tile(x, g, tgt), x_ref[...], g_ref[...])
        dx, dg = vjp(jnp.ones((1, 1), F32))
        dx_ref[...] = dx
        dg_ref[0:1, :] += dg
        loss_ref[...] += val

    tile = pl.BlockSpec((TM, D), lambda i: (i, 0))
    return pl.pallas_call(
        body, name=name,
        out_shape=(jax.ShapeDtypeStruct((8, 128), F32), jax.ShapeDtypeStruct((t, D), F32), jax.ShapeDtypeStruct((8, D), F32)),
        grid=(t // TM,), in_specs=[tile, _full((1, D)), tile], out_specs=(_full((8, 128)), tile, _full((8, D))),
        compiler_params=_cp(("arbitrary",)),
    )(x, _row(g), tgt)


def adamw(w, g, m, v, name):
    shape = w.shape
    cols = shape[-1] if w.ndim > 1 else shape[0]
    w2, g2, m2, v2 = (a.reshape(-1, cols) for a in (w, g, m, v))
    rows = w2.shape[0]
    br = _blk(rows, max(8, (1 << 19) // cols), 8)

    def body(w_ref, g_ref, m_ref, v_ref, d_ref, nm_ref, nv_ref):
        gg = g_ref[...]
        nm = ADAM_B1 * m_ref[...] + (1.0 - ADAM_B1) * gg
        nv = ADAM_B2 * v_ref[...] + (1.0 - ADAM_B2) * jnp.square(gg)
        m_hat = nm / (1.0 - ADAM_B1 ** ADAM_STEP)
        v_hat = nv / (1.0 - ADAM_B2 ** ADAM_STEP)
        d_ref[...] = -ADAM_LR * (m_hat / (jnp.sqrt(v_hat) + ADAM_EPS) + ADAM_WD * w_ref[...])
        nm_ref[...] = nm
        nv_ref[...] = nv

    spec = pl.BlockSpec((br, cols), lambda i: (i, 0))
    out = jax.ShapeDtypeStruct((rows, cols), F32)
    res = pl.pallas_call(body, name=name, out_shape=(out, out, out), grid=(rows // br,), in_specs=[spec] * 4,
                         out_specs=(spec, spec, spec), compiler_params=_cp(("parallel",)))(w2, g2, m2, v2)
    return tuple(r.reshape(shape) for r in res)


MESH = pl.DeviceIdType.MESH
CHIP_FLIPS = ((1, 0), (0, 1), (1, 1))


def _place():
    return lax.axis_index("x"), lax.axis_index("y"), lax.axis_index("c")


def allgather8(blk, name, in_vmem):
    space = pltpu.VMEM if in_vmem else pl.ANY

    def body(x_ref, out_ref, send_sems, recv_sems, local_sem):
        x, y, c = _place()
        me, sibling = (x, y, c), (x, y, 1 - c)
        chips = [(x ^ fx, y ^ fy) for fx, fy in CHIP_FLIPS]

        def slot(px, py, pc):
            return out_ref.at[4 * px + 2 * py + pc]

        def copy(k, block, to, src=None):
            return pltpu.make_async_remote_copy(
                src_ref=slot(*block) if src is None else src, dst_ref=slot(*block),
                send_sem=send_sems.at[k], recv_sem=recv_sems.at[k], device_id=to, device_id_type=MESH)

        mine = pltpu.make_async_copy(x_ref, slot(*me), local_sem)
        mine.start()
        first = [copy(0, me, sibling, src=x_ref)]
        first += [copy(1 + j, me, (*chip, c), src=x_ref) for j, chip in enumerate(chips)]
        for cp in first:
            cp.start()
        passed = [copy(4 + j, (*chip, c), sibling) for j, chip in enumerate(chips)]
        for j, chip in enumerate(chips):
            copy(1 + j, (*chip, c), me).wait_recv()
            passed[j].start()
        copy(0, sibling, me).wait_recv()
        for j, chip in enumerate(chips):
            copy(4 + j, (*chip, 1 - c), me).wait_recv()
        for cp in first + passed:
            cp.wait_send()
        mine.wait()

    return pl.pallas_call(
        body, name=name, out_shape=jax.ShapeDtypeStruct((N_DEV,) + blk.shape, blk.dtype),
        in_specs=[pl.BlockSpec(memory_space=space)], out_specs=pl.BlockSpec(memory_space=space),
        scratch_shapes=[pltpu.SemaphoreType.DMA((7,)), pltpu.SemaphoreType.DMA((7,)), pltpu.SemaphoreType.DMA],
        compiler_params=pltpu.CompilerParams(vmem_limit_bytes=VMEM_LIMIT),
    )(blk)


def flip_exchange(src, plan, n_out, name):
    def body(x_ref, out_ref, send_sems, recv_sems):
        x, y, c = _place()
        copies = []
        for k, (flip, src_index, dst_slot) in enumerate(plan):
            s, d = x_ref.at[src_index(x, y, c)], out_ref.at[dst_slot(x, y, c)]
            if flip is None:
                copies.append(pltpu.make_async_copy(s, d, send_sems.at[k]))
            else:
                copies.append(pltpu.make_async_remote_copy(
                    src_ref=s, dst_ref=d, send_sem=send_sems.at[k], recv_sem=recv_sems.at[k],
                    device_id=(x ^ flip[0], y ^ flip[1], c ^ flip[2]), device_id_type=MESH))
        for cp in copies:
            cp.start()
        for (flip, _, _), cp in zip(plan, copies):
            if flip is None:
                cp.wait()
            else:
                cp.wait_recv()
                cp.wait_send()

    n = len(plan)
    return pl.pallas_call(
        body, name=name, out_shape=jax.ShapeDtypeStruct((n_out,) + src.shape[1:], src.dtype),
        in_specs=[pl.BlockSpec(memory_space=pl.ANY)], out_specs=pl.BlockSpec(memory_space=pl.ANY),
        scratch_shapes=[pltpu.SemaphoreType.DMA((n,)), pltpu.SemaphoreType.DMA((n,))],
    )(src)


ROWS_ADD = 2048


def add_pairs(a, b, out_dtype, name):
    n = a.shape[0]
    br = _blk(n, ROWS_ADD, 16)

    def body(a_ref, b_ref, o_ref):
        o_ref[...] = (a_ref[...].astype(F32) + b_ref[...].astype(F32)).astype(o_ref.dtype)

    spec = pl.BlockSpec((br, 128), lambda i: (i, 0))
    return pl.pallas_call(body, name=name, out_shape=jax.ShapeDtypeStruct((n, 128), out_dtype), grid=(n // br,),
                          in_specs=[spec, spec], out_specs=spec, compiler_params=_cp(("parallel",)))(a, b)


def add_slots(own, others, name):
    n = own.shape[0]
    ns = others.shape[0]
    br = _blk(n, ROWS_ADD, 16)

    def body(a_ref, b_ref, o_ref):
        acc = a_ref[...].astype(F32)
        for s in range(ns):
            acc = acc + b_ref[s].astype(F32)
        o_ref[...] = acc

    return pl.pallas_call(body, name=name, out_shape=jax.ShapeDtypeStruct((n, 128), F32), grid=(n // br,),
                          in_specs=[pl.BlockSpec((br, 128), lambda i: (i, 0)), pl.BlockSpec((ns, br, 128), lambda i: (0, i, 0))],
                          out_specs=pl.BlockSpec((br, 128), lambda i: (i, 0)), compiler_params=_cp(("parallel",)))(own, others)


def sum8(g, name):
    r = g.shape[1]

    def body(g_ref, o_ref):
        acc = g_ref[0]
        for s in range(1, N_DEV):
            acc = acc + g_ref[s]
        o_ref[...] = acc

    return pl.pallas_call(body, name=name, out_shape=jax.ShapeDtypeStruct((r, 128), F32))(g)


def ada_mod(c_all, ada_w, ada_b_cols, name):
    def body(c_ref, w_ref, b_ref, o_ref):
        o_ref[0] = mm(_silu(c_ref[...]), w_ref[0]) + b_ref[0]

    n = ada_w.shape[2]
    return pl.pallas_call(
        body, name=name, out_shape=jax.ShapeDtypeStruct((DEPTH, N_DEV, n), F32), grid=(DEPTH,),
        in_specs=[pl.BlockSpec((N_DEV, D), lambda l: (0, 0)), pl.BlockSpec((1, D, n), lambda l: (l, 0, 0)),
                  pl.BlockSpec((1, 1, n), lambda l: (l, 0, 0))],
        out_specs=pl.BlockSpec((1, N_DEV, n), lambda l: (l, 0, 0)), compiler_params=_cp(("parallel",)),
    )(c_all, ada_w, ada_b_cols.reshape(DEPTH, 1, n))


def ada_grad(c_all, dmod_cols, name):
    def body(c_ref, d_ref, o_ref):
        o_ref[0] = mm_tn(_silu(c_ref[...]), d_ref[0])

    n = dmod_cols.shape[2]
    return pl.pallas_call(
        body, name=name, out_shape=jax.ShapeDtypeStruct((DEPTH, D, n), F32), grid=(DEPTH,),
        in_specs=[pl.BlockSpec((N_DEV, D), lambda l: (0, 0)), pl.BlockSpec((1, N_DEV, n), lambda l: (l, 0, 0))],
        out_specs=pl.BlockSpec((1, D, n), lambda l: (l, 0, 0)), compiler_params=_cp(("parallel",)),
    )(c_all, dmod_cols)


def pack_w_in(w):
    out = jnp.zeros(w.shape[:-1] + (NP,), w.dtype)
    for p_off, o_off, width in _PACK:
        out = out.at[..., p_off:p_off + width].set(w[..., o_off:o_off + width])
    return out


def unpack_w_in(w):
    return jnp.concatenate([w[..., p_off:p_off + width] for p_off, _, width in _PACK], axis=-1)


def pack_w_uq(w):
    return jnp.pad(w.reshape(w.shape[:-1] + (MLA_HEADS, MLA_QK)), [(0, 0)] * (w.ndim - 1) + [(0, 0), (0, 32)]).reshape(w.shape[:-1] + (512,))


def unpack_w_uq(w):
    return w.reshape(w.shape[:-1] + (MLA_HEADS, 128))[..., :MLA_QK].reshape(w.shape[:-1] + (MLA_HEADS * MLA_QK,))


def layer_fwd(x, mod, w, cos, sin, tag):
    h1 = modnorm_fwd(x, w["norm1_g"], mod[0], mod[1], tag + "norm1")
    proj = matmul(h1, w["w_in"], "nn", F32, tag + "w_in")
    y_ssd, hs = ssd_fwd(proj, w["ssd_conv_w"], w["ssd_conv_b"], w["ssd_dt_bias"], w["ssd_a_log"], w["ssd_d"], w["ssd_norm_g"], tag + "ssd")
    q, k, v = mla_pre_fwd(proj, cos, sin, w["mla_q_norm_g"], w["mla_kv_norm_g"], w["mla_w_uq"], w["mla_w_ukv"], tag + "mla_pre")
    o, lse = mla_flash_fwd(q, k, v, tag + "mla_attn")
    y_swa = swa_fwd(proj, w["swa_sinks"], tag + "swa")
    t = x.shape[0]
    ycat = jnp.concatenate([y_ssd, jnp.transpose(o, (1, 0, 2)).reshape(t, 256), y_swa], axis=1).astype(BF16)
    y = matmul(ycat, w["w_out"], "nn", F32, tag + "w_out")
    xm = resid_fwd(x, y, mod[2], tag + "res1")
    h2 = modnorm_fwd(xm, w["norm2_g"], mod[3], mod[4], tag + "norm2")
    u0 = matmul(h2, w["ffn_w_up"], "nn", F32, tag + "w_up")
    gact = convglu_fwd(u0, w["ffn_conv_w"], w["ffn_conv_b"], tag + "glu")
    yd = matmul(gact, w["ffn_w_down"], "nn", F32, tag + "w_down")
    xo = resid_fwd(xm, yd, mod[5], tag + "res2")
    return xo, dict(x=x, h1=h1, proj=proj, hs=hs, q=q, k=k, v=v, o=o, lse=lse, ycat=ycat, y=y, xm=xm, h2=h2, u0=u0, gact=gact, yd=yd)


def layer_bwd(dxo, s, mod, w, cos, sin, tag):
    t = dxo.shape[0]
    g = {}
    dyd, dg2 = resid_bwd(dxo, s["yd"], mod[5], tag + "res2_b")
    dgact = matmul(dyd, w["ffn_w_down"], "nt", BF16, tag + "w_down_dx")
    g["ffn_w_down"] = matmul(s["gact"], dyd, "tn", F32, tag + "w_down_dw")
    du0, g["ffn_conv_w"], dcb = convglu_bwd(s["u0"], w["ffn_conv_w"], w["ffn_conv_b"], dgact, tag + "glu_b")
    g["ffn_conv_b"] = dcb[0]
    dh2 = matmul(du0, w["ffn_w_up"], "nt", F32, tag + "w_up_dx")
    g["ffn_w_up"] = matmul(s["h2"], du0, "tn", F32, tag + "w_up_dw")
    dxm, sums2 = modnorm_bwd(s["xm"], w["norm2_g"], mod[3], mod[4], dh2, dxo, tag + "norm2_b")
    g["norm2_g"] = sums2[0]
    dy, dg1 = resid_bwd(dxm, s["y"], mod[2], tag + "res1_b")
    dycat = matmul(dy, w["w_out"], "nt", F32, tag + "w_out_dx")
    g["w_out"] = matmul(s["ycat"], dy, "tn", F32, tag + "w_out_dw")
    proj = s["proj"]
    dz, dxbc, ddt, g["ssd_conv_w"], vec = ssd_bwd(proj, s["hs"], dycat[:, 0:512], w["ssd_conv_w"], w["ssd_conv_b"], w["ssd_dt_bias"],
                                                 w["ssd_a_log"], w["ssd_d"], w["ssd_norm_g"], tag + "ssd_b")
    g["ssd_conv_b"], g["ssd_dt_bias"], g["ssd_a_log"], g["ssd_d"], g["ssd_norm_g"] = vec[0], vec[1, :8], vec[2, :8], vec[3, :8], vec[4, :512]
    do = jnp.transpose(dycat[:, 512:768].reshape(t, MLA_HEADS, 64), (1, 0, 2))
    dq, dk, dv = mla_flash_bwd(s["q"], s["k"], s["v"], s["o"], s["lse"], do, tag + "mla_attn_b")
    dcq, dckv, dkr, g["mla_w_uq"], g["mla_w_ukv"], mvec = mla_pre_bwd(proj, cos, sin, w["mla_q_norm_g"], w["mla_kv_norm_g"],
                                                                    w["mla_w_uq"], w["mla_w_ukv"], dq, dk, dv, tag + "mla_pre_b")
    g["mla_q_norm_g"], g["mla_kv_norm_g"] = mvec[0], mvec[1, :128]
    dsq, dsk, dsv, dsink = swa_bwd(proj, w["swa_sinks"], dycat[:, 768:1024], tag + "swa_b")
    g["swa_sinks"] = dsink[0, :4]
    dproj = jnp.concatenate([dxbc, dz, dcq, dsq, dckv, ddt, dkr, dsk, dsv], axis=1).astype(BF16)
    dh1 = matmul(dproj, w["w_in"], "nt", F32, tag + "w_in_dx")
    g["w_in"] = matmul(s["h1"], dproj, "tn", F32, tag + "w_in_dw")
    dx, sums1 = modnorm_bwd(s["x"], w["norm1_g"], mod[0], mod[1], dh1, dxm, tag + "norm1_b")
    g["norm1_g"] = sums1[0]
    dmod = jnp.stack([sums1[1], sums1[2], dg1[0], sums2[1], sums2[2], dg2[0]])
    return dx, dmod, g


def local_step(x, tgt, mods, ws, final_norm_g, cos, sin):
    saved = []
    for l in range(len(ws)):
        x, s = layer_fwd(x, mods[l], ws[l], cos, sin, f"l{l}_")
        saved.append(s)
    loss, dx, dfg = loss_fwd_bwd(x, final_norm_g, tgt, "loss")
    dmods, grads = [None] * len(ws), [None] * len(ws)
    for l in reversed(range(len(ws))):
        dx, dmods[l], grads[l] = layer_bwd(dx, saved[l], mods[l], ws[l], cos, sin, f"l{l}_")
    return loss, dx, dfg[0], jnp.stack(dmods), grads


WEIGHTS = ("ada_w", "ada_b", "norm1_g", "norm2_g", "w_in", "ssd_conv_w", "ssd_conv_b", "ssd_dt_bias", "ssd_a_log", "ssd_d",
           "ssd_norm_g", "mla_q_norm_g", "mla_w_uq", "mla_kv_norm_g", "mla_w_ukv", "swa_sinks", "w_out", "ffn_w_up",
           "ffn_conv_w", "ffn_conv_b", "ffn_w_down", "final_norm_g")
BIG = (("w_in", 2), ("w_out", 1), ("ffn_w_up", 2), ("ffn_w_down", 1), ("mla_w_uq", 2), ("mla_w_ukv", 2))
SMALL = (("dmod", 6 * D), ("norm1_g", D), ("norm2_g", D), ("ssd_conv_w", 4 * SSD_XBC), ("ssd_conv_b", SSD_XBC), ("ssd_dt_bias", 128),
         ("ssd_a_log", 128), ("ssd_d", 128), ("ssd_norm_g", SSD_INNER), ("mla_q_norm_g", 256), ("mla_kv_norm_g", 128),
         ("swa_sinks", 128), ("ffn_conv_w", 3 * 2 * D_FF), ("ffn_conv_b", 2 * D_FF))
SMALL_LAYER = sum(n for _, n in SMALL)
SMALL_SHAPES = {"norm1_g": (D,), "norm2_g": (D,), "ssd_conv_w": (4, SSD_XBC), "ssd_conv_b": (SSD_XBC,), "ssd_dt_bias": (8,),
                "ssd_a_log": (8,), "ssd_d": (8,), "ssd_norm_g": (SSD_INNER,), "mla_q_norm_g": (256,), "mla_kv_norm_g": (128,),
                "swa_sinks": (4,), "ffn_conv_w": (3, 2 * D_FF), "ffn_conv_b": (2 * D_FF,)}


def _lanes(v, n):
    v = v.reshape(-1)
    return jnp.pad(v, (0, n - v.shape[0]))


def _tile_rows(flat):
    n = -(-flat.shape[0] // 1024) * 1024
    return jnp.pad(flat, (0, n - flat.shape[0])).reshape(-1, 128)


def _rope_tables(positions):
    inv_freq = 10000.0 ** (-jnp.arange(0, 32, 2, dtype=F32) / 32)
    ang = positions.astype(F32).reshape(-1, 1) * inv_freq
    return jnp.cos(ang), jnp.sin(ang)


def _gather_big(shards, c):
    rb = sum(s.size for s in shards) // 256
    flat = jnp.concatenate([s.astype(BF16).reshape(-1, 128) for s in shards], axis=0).reshape(2, rb, 128)
    got = allgather8(lax.dynamic_index_in_dim(flat, c, 0, keepdims=False), "ag_weights", False).reshape(N_CHIP, 2 * rb, 128)
    fulls, off = [], 0
    for (_, axis), s in zip(BIG, shards):
        rows = s.size // 128
        fulls.append(jnp.concatenate([got[k, off:off + rows].reshape(s.shape) for k in range(N_CHIP)], axis=axis))
        off += rows
    return fulls


def _reduce_big(grads, x, y, c):
    chips = []
    for k in range(N_CHIP):
        parts = []
        for (_, axis), g in zip(BIG, grads):
            n = g.shape[axis] // N_CHIP
            parts.append(lax.slice_in_dim(g, k * n, (k + 1) * n, axis=axis).astype(BF16).reshape(-1, 128))
        chips.append(jnp.concatenate(parts, axis=0))
    rb = chips[0].shape[0] // 2
    halves = jnp.stack(chips).reshape(N_CHIP, 2, rb, 128).transpose(1, 0, 2, 3).reshape(2, N_CHIP * rb, 128)
    theirs = flip_exchange(halves, [((0, 0, 1), lambda x, y, c: 1 - c, lambda x, y, c: 0)], 1, "rs_sibling")
    mine = lax.dynamic_index_in_dim(halves, c, 0, keepdims=False)
    chip_sum = add_pairs(mine, theirs[0], BF16, "rs_add_sibling").reshape(N_CHIP, rb, 128)
    plan = [((fx, fy, 0), (lambda x, y, c, fx=fx, fy=fy: 2 * (x ^ fx) + (y ^ fy)), (lambda x, y, c, s=s: s))
            for s, (fx, fy) in enumerate(CHIP_FLIPS)]
    others = flip_exchange(chip_sum, plan, len(CHIP_FLIPS), "rs_chips")
    own = lax.dynamic_index_in_dim(chip_sum, 2 * x + y, 0, keepdims=False)
    half = add_slots(own, others, "rs_add_chips")
    other = flip_exchange(half[None], [((0, 0, 1), lambda x, y, c: 0, lambda x, y, c: 0)], 1, "rs_share")[0]
    flat = jnp.where(c == 0, jnp.concatenate([half, other], axis=0), jnp.concatenate([other, half], axis=0))
    out, off = [], 0
    for (_, axis), g in zip(BIG, grads):
        shape = list(g.shape)
        shape[axis] //= N_CHIP
        rows = math.prod(shape) // 128
        out.append(flat[off:off + rows].reshape(shape))
        off += rows
    return out


def kernel(x, c, positions, ada_w, ada_b, norm1_g, norm2_g, w_in, ssd_conv_w, ssd_conv_b, ssd_dt_bias, ssd_a_log, ssd_d, ssd_norm_g, mla_q_norm_g, mla_w_uq, mla_kv_norm_g, mla_w_ukv, swa_sinks, w_out, ffn_w_up, ffn_conv_w, ffn_conv_b, ffn_w_down, final_norm_g, loss_target, m_ada_w, m_ada_b, m_norm1_g, m_norm2_g, m_w_in, m_ssd_conv_w, m_ssd_conv_b, m_ssd_dt_bias, m_ssd_a_log, m_ssd_d, m_ssd_norm_g, m_mla_q_norm_g, m_mla_w_uq, m_mla_kv_norm_g, m_mla_w_ukv, m_swa_sinks, m_w_out, m_ffn_w_up, m_ffn_conv_w, m_ffn_conv_b, m_ffn_w_down, m_final_norm_g, v_ada_w, v_ada_b, v_norm1_g, v_norm2_g, v_w_in, v_ssd_conv_w, v_ssd_conv_b, v_ssd_dt_bias, v_ssd_a_log, v_ssd_d, v_ssd_norm_g, v_mla_q_norm_g, v_mla_w_uq, v_mla_kv_norm_g, v_mla_w_ukv, v_swa_sinks, v_w_out, v_ffn_w_up, v_ffn_conv_w, v_ffn_conv_b, v_ffn_w_down, v_final_norm_g):
    args = locals()
    wt = {n: args[n] for n in WEIGHTS}
    mx, my, mc = _place()
    chip = 2 * mx + my
    dev = 2 * chip + mc
    n_ada = ada_w.shape[2]

    pack = _tile_rows(jnp.concatenate([c.reshape(-1), ssd_conv_w.reshape(-1), ffn_conv_w.reshape(-1)]))
    got = allgather8(pack, "ag_small_in", True).reshape(N_DEV, -1)
    c_all = got[:, :D]
    per_chip = got[0::2]
    n_scw = ssd_conv_w.size
    ssd_cw = jnp.concatenate([per_chip[k, D:D + n_scw].reshape(ssd_conv_w.shape) for k in range(N_CHIP)], axis=2)
    n_fcw = ffn_conv_w.size
    ffn_cw = jnp.concatenate([per_chip[k, D + n_scw:D + n_scw + n_fcw].reshape(ffn_conv_w.shape) for k in range(N_CHIP)], axis=2)

    ada_b_cols = lax.dynamic_slice_in_dim(ada_b, chip * n_ada, n_ada, axis=1)
    mod_cols = ada_mod(c_all, ada_w, ada_b_cols, "ada_mod")
    mod_all = allgather8(mod_cols.reshape(-1, 128), "ag_mod", True)[0::2].reshape(N_CHIP, DEPTH, N_DEV, n_ada)
    mods = lax.dynamic_index_in_dim(mod_all, dev, 2, keepdims=False).transpose(1, 0, 2).reshape(DEPTH, 6, D)

    full = dict(zip([n for n, _ in BIG], _gather_big([wt[n] for n, _ in BIG], mc)))
    full["w_in"] = pack_w_in(full["w_in"])
    full["mla_w_uq"] = pack_w_uq(full["mla_w_uq"])
    ws = []
    for l in range(DEPTH):
        w = {n: full[n][l] for n, _ in BIG}
        w.update(ssd_conv_w=ssd_cw[l], ffn_conv_w=ffn_cw[l])
        for n in ("norm1_g", "norm2_g", "ssd_conv_b", "ssd_dt_bias", "ssd_a_log", "ssd_d", "ssd_norm_g", "mla_q_norm_g",
                  "mla_kv_norm_g", "swa_sinks", "ffn_conv_b"):
            w[n] = wt[n][l]
        ws.append(w)

    cos, sin = _rope_tables(positions)
    t = x.shape[1]
    loss8, dx, dfg, dmods, lg = local_step(x.reshape(t, D), loss_target.reshape(t, D), mods, ws, final_norm_g, cos, sin)
    loss = lax.psum(loss8[0, 0], ("x", "y", "c"))

    rows = []
    for l in range(DEPTH):
        for name, n in SMALL:
            rows.append(_lanes(dmods[l] if name == "dmod" else lg[l][name], n))
    rows.append(dfg)
    small = allgather8(_tile_rows(jnp.concatenate(rows)), "ag_small_grads", True)
    total = sum8(small, "sum_small_grads").reshape(-1)
    grads = {}
    per_layer = {name: [] for name, _ in SMALL}
    for l in range(DEPTH):
        off = l * SMALL_LAYER
        for name, n in SMALL:
            per_layer[name].append(total[off:off + n])
            off += n
    grads["ada_b"] = jnp.stack(per_layer["dmod"])
    for name, shape in SMALL_SHAPES.items():
        grads[name] = jnp.stack([v[:math.prod(shape)].reshape(shape) for v in per_layer[name]])
    grads["final_norm_g"] = total[DEPTH * SMALL_LAYER:DEPTH * SMALL_LAYER + D]
    for name in ("ssd_conv_w", "ffn_conv_w"):
        n = grads[name].shape[2] // N_CHIP
        grads[name] = lax.dynamic_slice_in_dim(grads[name], chip * n, n, axis=2)
    dmod_all = small.reshape(N_DEV, -1)[:, :DEPTH * SMALL_LAYER].reshape(N_DEV, DEPTH, SMALL_LAYER)[:, :, :6 * D]
    dmod_cols = lax.dynamic_slice_in_dim(dmod_all, chip * n_ada, n_ada, axis=2).transpose(1, 0, 2)
    grads["ada_w"] = ada_grad(c_all, dmod_cols, "ada_grad")

    stacked = []
    for name, _ in BIG:
        g = jnp.stack([lg[l][name] for l in range(DEPTH)])
        if name == "w_in":
            g = unpack_w_in(g)
        if name == "mla_w_uq":
            g = unpack_w_uq(g)
        stacked.append(g)
    for (name, _), g in zip(BIG, _reduce_big(stacked, mx, my, mc)):
        grads[name] = g

    deltas, new_m, new_v = {}, {}, {}
    for n in WEIGHTS:
        deltas[n], new_m[n], new_v[n] = adamw(wt[n], grads[n], args["m_" + n], args["v_" + n], "adamw_" + n)
    return (loss, dx.reshape(x.shape), *[grads[n] for n in WEIGHTS], *[deltas[n] for n in WEIGHTS],
            *[new_m[n] for n in WEIGHTS], *[new_v[n] for n in WEIGHTS])
```
